```python
import math
import jax, jax.numpy as jnp
from jax import lax
import numpy as np

D_MODEL = 2048
BATCH = 4
SEQ = 2048
DEPTH = 2
DEC_BATCH = 128
DEC_SEQ = 8
PAST_LEN = 8192
PAGE_SIZE = 128

N_A_LAYERS = DEPTH // 2
N_B_LAYERS = DEPTH - N_A_LAYERS
A_HEADS = 8
A_DK = D_MODEL // (2 * A_HEADS)
A_DV = D_MODEL // A_HEADS
A_CHUNK = 64
GATE_CAP = 15.0
A_IN_WIDTH = 2 * A_HEADS * A_DK + A_HEADS * A_DV + D_MODEL + 2 * A_HEADS
B_HEADS = 32
B_KV_HEADS = 4
B_HEAD_DIM = D_MODEL // B_HEADS
B_GROUP = B_HEADS // B_KV_HEADS
WINDOW = 128
ROPE_THETA = 10000.0
N_EXPERTS = 16
N_GROUPS = 4
EXPERTS_PER_GROUP = N_EXPERTS // N_GROUPS
GROUP_TOP = 2
TOP_K = 2
D_EXPERT = 1408
MOE_BLOCK = 128
DN_ALPHA = (2 * DEPTH) ** 0.25
DN_BETA = (8 * DEPTH) ** -0.25
LN_EPS = 1e-5

kernel_name = 'yoco_mlstm_swa_sink_grouped_moe_step'


def layer_norm(x, g, b):
    xf = x.astype(jnp.float32)
    mu = xf.mean(-1, keepdims=True)
    var = jnp.square(xf - mu).mean(-1, keepdims=True)
    return ((xf - mu) * lax.rsqrt(var + LN_EPS) * g.astype(jnp.float32) + b.astype(jnp.float32)).astype(x.dtype)


def rope(x, pos):
    half = x.shape[-1] // 2
    inv = ROPE_THETA ** (-jnp.arange(half, dtype=jnp.float32) / half)
    ang = pos.astype(jnp.float32)[:, None] * inv[None, :]
    cos = jnp.cos(ang)[None, :, None, :]
    sin = jnp.sin(ang)[None, :, None, :]
    xf = x.astype(jnp.float32)
    x1, x2 = xf[..., :half], xf[..., half:]
    return jnp.concatenate([x1 * cos - x2 * sin, x2 * cos + x1 * sin], -1).astype(x.dtype)


def softcap(z):
    return GATE_CAP * jnp.tanh(z / GATE_CAP)


def _to_chunks(a, nc, lc):
    Bsz, T, H = a.shape[:3]
    a = a.reshape((Bsz, nc, lc, H) + a.shape[3:])
    return a.transpose((1, 0, 3, 2) + tuple(range(4, a.ndim)))


def mlstm_recurrence(q, k, v, ig, lf, C0, n0, m0):
    Bsz, T, H, _ = q.shape
    lc = math.gcd(T, A_CHUNK)
    nc = T // lc
    causal = jnp.tril(jnp.ones((lc, lc), dtype=bool))
    xs = tuple(_to_chunks(a, nc, lc) for a in (q, k, v, ig, lf))

    def step(carry, inp):
        C, n, m = carry
        qc, kc, vc, ic, fc = inp
        b = jnp.cumsum(fc, axis=-1)
        a = b + m[..., None]
        dmat = jnp.where(causal, b[..., :, None] - b[..., None, :] + ic[..., None, :], -jnp.inf)
        m_t = jnp.maximum(a, dmat.max(-1))
        w = jnp.exp(dmat - m_t[..., None]) * jnp.einsum('bhtk,bhsk->bhts', qc, kc)
        inter = jnp.exp(a - m_t)
        num = inter[..., None] * jnp.einsum('bhtk,bhkv->bhtv', qc, C) + jnp.einsum('bhts,bhsv->bhtv', w, vc)
        den = inter * jnp.einsum('bhtk,bhk->bht', qc, n) + w.sum(-1)
        h = num / jnp.maximum(jnp.abs(den), jnp.exp(-m_t))[..., None]
        m_new = m_t[..., -1]
        decay = jnp.exp(b[..., -1:] - b + ic - m_new[..., None])
        carry_scale = jnp.exp(b[..., -1] + m - m_new)
        C_new = carry_scale[..., None, None] * C + jnp.einsum('bhs,bhsk,bhsv->bhkv', decay, kc, vc)
        n_new = carry_scale[..., None] * n + jnp.einsum('bhs,bhsk->bhk', decay, kc)
        return (C_new, n_new, m_new), h

    (C, n, m), hs = lax.scan(step, (C0, n0, m0), xs)
    hs = hs.transpose(1, 0, 3, 2, 4).reshape(Bsz, T, H, hs.shape[-1])
    return hs, (C, n, m)


def mlstm_mixer(x, C0, n0, m0, w_in, b_i, b_f, norm_w, w_out):
    f32 = jnp.float32
    Bsz, T, _ = x.shape
    qk = A_HEADS * A_DK
    vw = A_HEADS * A_DV
    cuts = [qk, 2 * qk, 2 * qk + vw, 2 * qk + vw + D_MODEL, 2 * qk + vw + D_MODEL + A_HEADS]
    q, k, v, og, ig, fg = jnp.split(x @ w_in, cuts, axis=-1)
    q = q.reshape(Bsz, T, A_HEADS, A_DK).astype(f32)
    k = k.reshape(Bsz, T, A_HEADS, A_DK).astype(f32) * (A_DK ** -0.5)
    v = v.reshape(Bsz, T, A_HEADS, A_DV).astype(f32)
    ig = softcap(ig.astype(f32) + b_i.astype(f32))
    lf = jax.nn.log_sigmoid(softcap(fg.astype(f32) + b_f.astype(f32)))
    h, (C, n, m) = mlstm_recurrence(q, k, v, ig, lf, C0.astype(f32), n0.astype(f32), m0.astype(f32))
    h = h * lax.rsqrt(jnp.square(h).mean(-1, keepdims=True) + LN_EPS)
    h = h.reshape(Bsz, T, vw) * norm_w.astype(f32) * jax.nn.sigmoid(og.astype(f32))
    return h.astype(x.dtype) @ w_out, (C.astype(x.dtype), n.astype(x.dtype), m.astype(x.dtype))


def sink_softmax(s, mask, sink):
    s = jnp.where(mask, s, -jnp.inf)
    mx = jnp.maximum(s.max(-1), sink)
    e = jnp.exp(s - mx[..., None])
    den = e.sum(-1) + jnp.exp(sink - mx)
    return e / den[..., None]


def swa_prompt(q, k, v, sink):
    Bsz, T = q.shape[:2]
    nb = T // WINDOW
    qb = q.reshape(Bsz, nb, WINDOW, B_KV_HEADS, B_GROUP, B_HEAD_DIM)
    kb = k.reshape(Bsz, nb, WINDOW, B_KV_HEADS, B_HEAD_DIM)
    vb = v.reshape(Bsz, nb, WINDOW, B_KV_HEADS, B_HEAD_DIM)
    pad = jnp.zeros_like(kb[:, :1])
    kk = jnp.concatenate([jnp.concatenate([pad, kb[:, :-1]], 1), kb], 2)
    vv = jnp.concatenate([jnp.concatenate([pad, vb[:, :-1]], 1), vb], 2)
    qi = jnp.arange(WINDOW)[:, None]
    kj = jnp.arange(2 * WINDOW)[None, :]
    diff = qi + WINDOW - kj
    band = (diff >= 0) & (diff < WINDOW)
    mask = band[None] & ((jnp.arange(nb)[:, None, None] > 0) | (kj >= WINDOW)[None])
    s = jnp.einsum('bnqhgd,bnkhd->bnhgqk', qb, kk, preferred_element_type=jnp.float32) * (B_HEAD_DIM ** -0.5)
    p = sink_softmax(s, mask[None, :, None, None], sink.astype(jnp.float32).reshape(B_KV_HEADS, B_GROUP, 1))
    o = jnp.einsum('bnhgqk,bnkhd->bnqhgd', p.astype(v.dtype), vv)
    return o.reshape(Bsz, T, B_HEADS * B_HEAD_DIM)


def swa_sample(q, k_all, v_all, sink):
    Bsz, S = q.shape[:2]
    L = k_all.shape[1]
    Wb = L - S
    qg = q.reshape(Bsz, S, B_KV_HEADS, B_GROUP, B_HEAD_DIM)
    s = jnp.einsum('bqhgd,bkhd->bhgqk', qg, k_all, preferred_element_type=jnp.float32) * (B_HEAD_DIM ** -0.5)
    diff = (Wb + jnp.arange(S))[:, None] - jnp.arange(L)[None, :]
    mask = (diff >= 0) & (diff < WINDOW)
    p = sink_softmax(s, mask, sink.astype(jnp.float32).reshape(B_KV_HEADS, B_GROUP, 1))
    o = jnp.einsum('bhgqk,bkhd->bqhgd', p.astype(v_all.dtype), v_all)
    return o.reshape(Bsz, S, B_HEADS * B_HEAD_DIM)


def moe(x, w_router, b_router, w_gate, w_up, w_down):
    shape = x.shape
    x2d = x.reshape(-1, shape[-1])
    T = x2d.shape[0]
    probs = jax.nn.softmax((x2d @ w_router).astype(jnp.float32), axis=-1)
    sel = (probs + b_router.astype(jnp.float32)).reshape(T, N_GROUPS, EXPERTS_PER_GROUP)
    group_score = lax.top_k(sel, GROUP_TOP)[0].sum(-1)
    g_idx = jnp.argmax(group_score, axis=-1)
    in_group = jnp.take_along_axis(sel, g_idx[:, None, None], axis=1)[:, 0]
    _, local = lax.top_k(in_group, TOP_K)
    expert = g_idx[:, None] * EXPERTS_PER_GROUP + local
    gate = jnp.take_along_axis(probs, expert, axis=-1)
    gate = gate / gate.sum(-1, keepdims=True)
    A = T * TOP_K
    e_flat = expert.reshape(-1)
    tok_flat = jnp.repeat(jnp.arange(T, dtype=jnp.int32), TOP_K)
    g_flat = gate.reshape(-1).astype(x.dtype)
    order = jnp.argsort(e_flat)
    e_sorted, tok_sorted, g_sorted = e_flat[order], tok_flat[order], g_flat[order]
    counts = jnp.bincount(e_flat, length=N_EXPERTS)
    start = jnp.cumsum(counts) - counts
    padded = (counts + MOE_BLOCK - 1) // MOE_BLOCK * MOE_BLOCK
    pend = jnp.cumsum(padded)
    pstart = pend - padded
    slot = pstart[e_sorted] + (jnp.arange(A) - start[e_sorted])
    n_blocks = -(-(A + N_EXPERTS * (MOE_BLOCK - 1)) // MOE_BLOCK)
    n_slots = n_blocks * MOE_BLOCK
    slot_tok = jnp.zeros((n_slots,), jnp.int32).at[slot].set(tok_sorted)
    slot_gate = jnp.zeros((n_slots,), x.dtype).at[slot].set(g_sorted)
    block_expert = jnp.minimum(jnp.searchsorted(pend, jnp.arange(n_blocks) * MOE_BLOCK, side='right'), N_EXPERTS - 1)

    def block_ffn(args):
        tok, e = args
        xb = x2d[tok]
        hb = jax.nn.silu(xb @ w_gate[e]) * (xb @ w_up[e])
        return hb @ w_down[e]

    y_slots = lax.map(block_ffn, (slot_tok.reshape(n_blocks, MOE_BLOCK), block_expert))
    y = jnp.zeros_like(x2d).at[slot_tok].add(y_slots.reshape(n_slots, -1) * slot_gate[:, None])
    return y.reshape(shape)


def trunk(x, pos, a_C, a_n, a_m, kv_past_k, kv_past_v, w_in_a, b_i_a, b_f_a, norm_a, w_out_a,
          w_kv, w_q_b, sinks_b, w_out_b, w_router, b_router, w_gate_e, w_up_e, w_down_e, ln_g, ln_b):
    Bsz, T, _ = x.shape
    w_buf = min(WINDOW, PAST_LEN)
    new_C, new_n, new_m = [], [], []
    k_use = v_use = new_k = new_v = None
    for l in range(DEPTH):
        if l < N_A_LAYERS:
            mix, (C, n, m) = mlstm_mixer(x, a_C[l], a_n[l], a_m[l], w_in_a[l], b_i_a[l], b_f_a[l], norm_a[l], w_out_a[l])
            new_C.append(C)
            new_n.append(n)
            new_m.append(m)
        else:
            j = l - N_A_LAYERS
            if j == 0:
                kv = (x @ w_kv).reshape(Bsz, T, 2, B_KV_HEADS, B_HEAD_DIM)
                k_new = rope(kv[:, :, 0], pos)
                v_new = kv[:, :, 1]
                if kv_past_k is None:
                    k_use, v_use = k_new, v_new
                else:
                    k_use = jnp.concatenate([kv_past_k.astype(x.dtype), k_new], axis=1)
                    v_use = jnp.concatenate([kv_past_v.astype(x.dtype), v_new], axis=1)
                new_k = k_use[:, -w_buf:]
                new_v = v_use[:, -w_buf:]
            q = rope((x @ w_q_b[j]).reshape(Bsz, T, B_HEADS, B_HEAD_DIM), pos)
            if kv_past_k is None:
                o = swa_prompt(q, k_use, v_use, sinks_b[j])
            else:
                o = swa_sample(q, k_use, v_use, sinks_b[j])
            mix = o @ w_out_b[j]
        x = layer_norm(DN_ALPHA * x + mix, ln_g[l, 0], ln_b[l, 0])
        x = layer_norm(DN_ALPHA * x + moe(x, w_router, b_router, w_gate_e[l], w_up_e[l], w_down_e[l]), ln_g[l, 1], ln_b[l, 1])
    return x, jnp.stack(new_C), jnp.stack(new_n), jnp.stack(new_m), new_k, new_v


def setup_inputs(seed: int = 0) -> dict:
    key = jax.random.key(seed)
    ks = jax.random.split(key, 24)
    f32 = jnp.float32

    def nrm(k, shape, scale):
        return jax.random.normal(k, shape, f32) * scale

    w_buf = min(WINDOW, PAST_LEN)
    x_prompt = nrm(ks[0], (BATCH, SEQ, D_MODEL), 1.0)
    x_sample = nrm(ks[1], (DEC_BATCH, DEC_SEQ, D_MODEL), 1.0)
    state_C = nrm(ks[2], (N_A_LAYERS, DEC_BATCH, A_HEADS, A_DK, A_DV), 1.0)
    state_n = nrm(ks[3], (N_A_LAYERS, DEC_BATCH, A_HEADS, A_DK), 1.0)
    state_m = nrm(ks[4], (N_A_LAYERS, DEC_BATCH, A_HEADS), 0.5)
    cache_k = nrm(ks[5], (DEC_BATCH, w_buf, B_KV_HEADS, B_HEAD_DIM), 1.0)
    cache_v = nrm(ks[6], (DEC_BATCH, w_buf, B_KV_HEADS, B_HEAD_DIM), DN_BETA)
    col_scale = jnp.concatenate([jnp.ones((2 * A_HEADS * A_DK,), f32), jnp.full((A_HEADS * A_DV,), DN_BETA, f32),
                                 jnp.ones((D_MODEL + 2 * A_HEADS,), f32)])
    w_in_a = nrm(ks[7], (N_A_LAYERS, D_MODEL, A_IN_WIDTH), D_MODEL ** -0.5) * col_scale
    b_i_a = nrm(ks[8], (N_A_LAYERS, A_HEADS), 0.1)
    b_f_a = jnp.linspace(3.0, 6.0, A_HEADS, dtype=f32)[None, :] + nrm(ks[9], (N_A_LAYERS, A_HEADS), 0.1)
    norm_a = 1.0 + nrm(ks[10], (N_A_LAYERS, A_HEADS * A_DV), 0.02)
    w_out_a = nrm(ks[11], (N_A_LAYERS, A_HEADS * A_DV, D_MODEL), (A_HEADS * A_DV) ** -0.5 * DN_BETA)
    kv_scale = jnp.concatenate([jnp.ones((B_KV_HEADS * B_HEAD_DIM,), f32), jnp.full((B_KV_HEADS * B_HEAD_DIM,), DN_BETA, f32)])
    w_kv = nrm(ks[12], (D_MODEL, 2 * B_KV_HEADS * B_HEAD_DIM), D_MODEL ** -0.5) * kv_scale
    w_q_b = nrm(ks[13], (N_B_LAYERS, D_MODEL, B_HEADS * B_HEAD_DIM), D_MODEL ** -0.5)
    sinks_b = nrm(ks[14], (N_B_LAYERS, B_HEADS), 0.5)
    w_out_b = nrm(ks[15], (N_B_LAYERS, B_HEADS * B_HEAD_DIM, D_MODEL), (B_HEADS * B_HEAD_DIM) ** -0.5 * DN_BETA)
    w_router = nrm(ks[16], (D_MODEL, N_EXPERTS), D_MODEL ** -0.5)
    b_router = nrm(ks[17], (N_EXPERTS,), 0.01)
    w_gate_e = nrm(ks[18], (DEPTH, N_EXPERTS, D_MODEL, D_EXPERT), D_MODEL ** -0.5)
    w_up_e = nrm(ks[19], (DEPTH, N_EXPERTS, D_MODEL, D_EXPERT), D_MODEL ** -0.5)
    w_down_e = nrm(ks[20], (DEPTH, N_EXPERTS, D_EXPERT, D_MODEL), D_EXPERT ** -0.5 * DN_BETA)
    ln_g = 1.0 + nrm(ks[21], (DEPTH, 2, D_MODEL), 0.02)
    ln_b = nrm(ks[22], (DEPTH, 2, D_MODEL), 0.02)
    return {'x_prompt': x_prompt, 'x_sample': x_sample, 'state_C': state_C, 'state_n': state_n,
            'state_m': state_m, 'cache_k': cache_k, 'cache_v': cache_v, 'w_in_a': w_in_a, 'b_i_a': b_i_a,
            'b_f_a': b_f_a, 'norm_a': norm_a, 'w_out_a': w_out_a, 'w_kv': w_kv, 'w_q_b': w_q_b,
            'sinks_b': sinks_b, 'w_out_b': w_out_b, 'w_router': w_router, 'b_router': b_router,
            'w_gate_e': w_gate_e, 'w_up_e': w_up_e, 'w_down_e': w_down_e, 'ln_g': ln_g, 'ln_b': ln_b}


def reference(x_prompt, x_sample, state_C, state_n, state_m, cache_k, cache_v, w_in_a, b_i_a, b_f_a, norm_a,
              w_out_a, w_kv, w_q_b, sinks_b, w_out_b, w_router, b_router, w_gate_e, w_up_e, w_down_e, ln_g, ln_b):
    Bp, Tp = x_prompt.shape[:2]
    Ts = x_sample.shape[1]
    C0 = jnp.zeros((N_A_LAYERS, Bp, A_HEADS, A_DK, A_DV), jnp.float32)
    n0 = jnp.zeros((N_A_LAYERS, Bp, A_HEADS, A_DK), jnp.float32)
    m0 = jnp.zeros((N_A_LAYERS, Bp, A_HEADS), jnp.float32)
    pos_p = jnp.arange(Tp, dtype=jnp.int32)
    pos_s = PAST_LEN + jnp.arange(Ts, dtype=jnp.int32)
    y_prompt, C_p, n_p, m_p, k_p, v_p = trunk(
        x_prompt, pos_p, C0, n0, m0, None, None, w_in_a, b_i_a, b_f_a, norm_a, w_out_a, w_kv, w_q_b,
        sinks_b, w_out_b, w_router, b_router, w_gate_e, w_up_e, w_down_e, ln_g, ln_b)
    y_sample, C_s, n_s, m_s, k_s, v_s = trunk(
        x_sample, pos_s, state_C, state_n, state_m, cache_k, cache_v, w_in_a, b_i_a, b_f_a, norm_a, w_out_a,
        w_kv, w_q_b, sinks_b, w_out_b, w_router, b_router, w_gate_e, w_up_e, w_down_e, ln_g, ln_b)
    return (y_prompt, y_sample, C_p, n_p, m_p, k_p, v_p, C_s, n_s, m_s, k_s, v_s)
```

```python
import functools

import jax
import jax.numpy as jnp
from jax import lax
from jax.experimental import pallas as pl
from jax.experimental.pallas import tpu as pltpu

F32 = jnp.float32
BF16 = jnp.bfloat16

GATE_CAP = 15.0
LN_EPS = 1e-5
ROPE_THETA = 10000.0
WINDOW = 128
PAST_LEN = 8192
N_GROUPS = 4

VMEM_LIMIT_BYTES = 56 * 1024 * 1024
MLSTM_CHUNK = 128
MOE_TM = 256
SAMPLE_BS = 2
ROW_TILES = (512, 256, 128)

_NT = (((1,), (1,)), ((), ()))
_TN = (((0,), (0,)), ((), ()))


def _pick(n, cands):
    for c in cands:
        if n % c == 0:
            return c
    raise ValueError(f"no tile in {cands} divides {n}")


def _cparams(*sem):
    return pltpu.CompilerParams(dimension_semantics=sem, vmem_limit_bytes=VMEM_LIMIT_BYTES)


def _dot(a, b):
    return jnp.dot(a, b, preferred_element_type=F32)


def _dg(a, b, dims):
    return lax.dot_general(a, b, dims, preferred_element_type=F32)


def _mm_scale_body(a_ref, b_ref, s_ref, o_ref):
    o_ref[...] = (_dot(a_ref[...], b_ref[...]) * s_ref[...]).astype(o_ref.dtype)


def _matmul(a, b, col_scale, out_dtype):
    M, K = a.shape
    N = b.shape[1]
    tm = _pick(M, ROW_TILES)
    tn = _pick(N, (1024, 512, 256))
    return pl.pallas_call(
        _mm_scale_body,
        grid=(M // tm, N // tn),
        in_specs=[pl.BlockSpec((tm, K), lambda i, j: (i, 0)),
                  pl.BlockSpec((K, tn), lambda i, j: (0, j)),
                  pl.BlockSpec((1, tn), lambda i, j: (0, j))],
        out_specs=pl.BlockSpec((tm, tn), lambda i, j: (i, j)),
        out_shape=jax.ShapeDtypeStruct((M, N), out_dtype),
        compiler_params=_cparams("parallel", "parallel"),
        name="matmul",
    )(a, b, col_scale)


def _mm_rope_body(a_ref, b_ref, cos_ref, sin_ref, o_ref, *, half):
    acc = _dot(a_ref[...], b_ref[...])
    tn = acc.shape[1]
    reps = tn // cos_ref.shape[1]
    cos = jnp.concatenate([cos_ref[...]] * reps, axis=1)
    sin = jnp.concatenate([sin_ref[...]] * reps, axis=1)
    lane = lax.broadcasted_iota(jnp.int32, acc.shape, 1)
    partner = jnp.where(lane % (2 * half) < half, -pltpu.roll(acc, tn - half, 1), pltpu.roll(acc, half, 1))
    o_ref[...] = (acc * cos + partner * sin).astype(o_ref.dtype)


def _matmul_rope(a, b, cos, sin, head_dim, out_dtype):
    M, K = a.shape
    N = b.shape[1]
    tm = _pick(M, ROW_TILES)
    tn = _pick(N, (1024, 512, 256))
    return pl.pallas_call(
        functools.partial(_mm_rope_body, half=head_dim // 2),
        grid=(M // tm, N // tn),
        in_specs=[pl.BlockSpec((tm, K), lambda i, j: (i, 0)),
                  pl.BlockSpec((K, tn), lambda i, j: (0, j)),
                  pl.BlockSpec((tm, cos.shape[1]), lambda i, j: (i, 0)),
                  pl.BlockSpec((tm, sin.shape[1]), lambda i, j: (i, 0))],
        out_specs=pl.BlockSpec((tm, tn), lambda i, j: (i, j)),
        out_shape=jax.ShapeDtypeStruct((M, N), out_dtype),
        compiler_params=_cparams("parallel", "parallel"),
        name="matmul_rope",
    )(a, b, cos, sin)


def _split_bf16(x):
    hi = x.astype(BF16)
    lo = (x - hi.astype(F32)).astype(BF16)
    return hi, lo


def _dot3(a, b, dims):
    ah, al = _split_bf16(a)
    bh, bl = _split_bf16(b)
    return _dg(ah, bh, dims) + _dg(al, bh, dims) + _dg(ah, bl, dims)


def _softcap(z):
    return GATE_CAP * jnp.tanh(z / GATE_CAP)


def _log_sigmoid(z):
    return jnp.minimum(z, 0.0) - jnp.log1p(jnp.exp(-jnp.abs(z)))


def _gates_body(x_ref, w_ref, wt_ref, brow_ref, bcol_ref, g_ref, gt_ref, *, H):
    x = x_ref[...]
    pre = _dot3(x, w_ref[...], (((1,), (0,)), ((), ()))) + brow_ref[...]
    pre_t = _dot3(wt_ref[...], x, _NT) + bcol_ref[...]
    z = _softcap(pre)
    zt = _softcap(pre_t)
    is_in = lax.broadcasted_iota(jnp.int32, z.shape, 1) < H
    is_in_t = lax.broadcasted_iota(jnp.int32, zt.shape, 0) < H
    g_ref[...] = jnp.where(is_in, z, _log_sigmoid(z))
    gt_ref[...] = jnp.where(is_in_t, zt, _log_sigmoid(zt))


def _mlstm_gates(x, w_gates, b_i, b_f):
    N, D = x.shape
    H = b_i.shape[0]
    tm = _pick(N, ROW_TILES)
    bias = jnp.concatenate([b_i, b_f]).astype(F32)
    return pl.pallas_call(
        functools.partial(_gates_body, H=H),
        grid=(N // tm,),
        in_specs=[pl.BlockSpec((tm, D), lambda i: (i, 0)),
                  pl.BlockSpec((D, 2 * H), lambda i: (0, 0)),
                  pl.BlockSpec((2 * H, D), lambda i: (0, 0)),
                  pl.BlockSpec((1, 2 * H), lambda i: (0, 0)),
                  pl.BlockSpec((2 * H, 1), lambda i: (0, 0))],
        out_specs=[pl.BlockSpec((tm, 2 * H), lambda i: (i, 0)),
                   pl.BlockSpec((2 * H, tm), lambda i: (0, i))],
        out_shape=[jax.ShapeDtypeStruct((N, 2 * H), F32), jax.ShapeDtypeStruct((2 * H, N), F32)],
        compiler_params=_cparams("parallel"),
        name="mlstm_gates",
    )(x, w_gates, w_gates.T, bias[None, :], bias[:, None])


def _mlstm_head(qh, kh, vh, ig_col, lf_col, ig_row, lf_row, C, n_row, m_prev, mxu_dtype):
    L = qh.shape[0]
    t_idx = lax.broadcasted_iota(jnp.int32, (L, L), 0)
    s_idx = lax.broadcasted_iota(jnp.int32, (L, L), 1)
    causal = s_idx <= t_idx

    def mxu(x):
        return x.astype(BF16).astype(mxu_dtype)

    b_col = jnp.sum(jnp.where(causal, lf_row, 0.0), axis=1, keepdims=True)
    b_row = jnp.sum(jnp.where(t_idx <= s_idx, lf_col, 0.0), axis=0, keepdims=True)
    dmat = jnp.where(causal, b_col - b_row + ig_row, -jnp.inf)
    a_col = b_col + m_prev
    m_t = jnp.maximum(a_col, jnp.max(dmat, axis=1, keepdims=True))
    wmat = jnp.exp(dmat - m_t) * _dg(qh, kh, _NT)
    inter = jnp.exp(a_col - m_t)
    qf = qh.astype(F32)
    kf = kh.astype(F32)
    num = inter * _dot(qh, mxu(C)) + _dot(mxu(wmat), vh)
    den = inter * jnp.sum(qf * n_row, axis=1, keepdims=True) + jnp.sum(wmat, axis=1, keepdims=True)
    hh = num * (1.0 / jnp.maximum(jnp.abs(den), jnp.exp(-m_t)))
    m_new = m_t[L - 1:L, :]
    b_last = b_col[L - 1:L, :]
    decay = jnp.exp(b_last - b_col + ig_col - m_new)
    carry = jnp.exp(b_last + m_prev - m_new)
    kd = kf * decay
    C_new = carry * C + _dg(mxu(kd), vh, _TN)
    n_new = carry * n_row + jnp.sum(kd, axis=0, keepdims=True)
    return hh, C_new, n_new, m_new


def _head_out(hh, nw, og):
    hn = hh * lax.rsqrt(jnp.mean(hh * hh, axis=1, keepdims=True) + LN_EPS)
    return (hn * nw * jax.nn.sigmoid(og)).astype(BF16)


def _mlstm_prompt_body(q_ref, k_ref, v_ref, og_ref, g_ref, gt_ref, nw_ref,
                       h_ref, c_ref, n_ref, m_ref, *, H, DK, DV):
    @pl.when(pl.program_id(1) == 0)
    def _():
        c_ref[...] = jnp.zeros_like(c_ref)
        n_ref[...] = jnp.zeros_like(n_ref)
        m_ref[...] = jnp.zeros_like(m_ref)

    g = g_ref[...]
    gt = gt_ref[...]
    for h in range(H):
        kq = slice(h * DK, (h + 1) * DK)
        vv = slice(h * DV, (h + 1) * DV)
        hh, C_new, n_new, m_new = _mlstm_head(
            q_ref[:, kq], k_ref[:, kq], v_ref[:, vv],
            g[:, h:h + 1], g[:, H + h:H + h + 1], gt[h:h + 1, :], gt[H + h:H + h + 1, :],
            c_ref[0, h], n_ref[0, h:h + 1, :], m_ref[0, :, h:h + 1], BF16)
        c_ref[0, h] = C_new
        n_ref[0, h:h + 1, :] = n_new
        m_ref[0, :, h:h + 1] = m_new
        h_ref[:, vv] = _head_out(hh, nw_ref[:, vv], og_ref[:, vv])


def _mlstm_prompt(q, k, v, og, g, gt, norm_w, B, T, H):
    DK = q.shape[1] // H
    DV = v.shape[1] // H
    L = MLSTM_CHUNK
    assert T % L == 0
    nc = T // L
    row = lambda b, c: (b * nc + c, 0)
    return pl.pallas_call(
        functools.partial(_mlstm_prompt_body, H=H, DK=DK, DV=DV),
        grid=(B, nc),
        in_specs=[pl.BlockSpec((L, H * DK), row), pl.BlockSpec((L, H * DK), row),
                  pl.BlockSpec((L, H * DV), row), pl.BlockSpec((L, H * DV), row),
                  pl.BlockSpec((L, 2 * H), row),
                  pl.BlockSpec((2 * H, L), lambda b, c: (0, b * nc + c)),
                  pl.BlockSpec((1, H * DV), lambda b, c: (0, 0))],
        out_specs=[pl.BlockSpec((L, H * DV), row),
                   pl.BlockSpec((1, H, DK, DV), lambda b, c: (b, 0, 0, 0)),
                   pl.BlockSpec((1, H, DK), lambda b, c: (b, 0, 0)),
                   pl.BlockSpec((1, 1, H), lambda b, c: (b, 0, 0))],
        out_shape=[jax.ShapeDtypeStruct((B * T, H * DV), BF16),
                   jax.ShapeDtypeStruct((B, H, DK, DV), F32),
                   jax.ShapeDtypeStruct((B, H, DK), F32),
                   jax.ShapeDtypeStruct((B, 1, H), F32)],
        compiler_params=_cparams("parallel", "arbitrary"),
        name="mlstm_prompt",
    )(q, k, v, og, g, gt, norm_w)


def _mlstm_sample_body(q_ref, k_ref, v_ref, og_ref, g_ref, gt_ref, nw_ref, c0_ref, n0_ref, m0_ref,
                       h_ref, c_ref, n_ref, m_ref, *, H, DK, DV, TS):
    q = q_ref[...].astype(F32)
    k = k_ref[...].astype(F32)
    v = v_ref[...].astype(F32)
    g = g_ref[...]
    gt = gt_ref[0]
    for bb in range(SAMPLE_BS):
        rows = slice(bb * TS, (bb + 1) * TS)
        for h in range(H):
            kq = slice(h * DK, (h + 1) * DK)
            vv = slice(h * DV, (h + 1) * DV)
            hh, C_new, n_new, m_new = _mlstm_head(
                q[rows, kq], k[rows, kq], v[rows, vv],
                g[rows, h:h + 1], g[rows, H + h:H + h + 1], gt[h:h + 1, rows], gt[H + h:H + h + 1, rows],
                c0_ref[bb, h], n0_ref[bb, h:h + 1, :], m0_ref[0, bb:bb + 1, h:h + 1], F32)
            c_ref[bb, h] = C_new
            n_ref[bb, h:h + 1, :] = n_new
            m_ref[0, bb:bb + 1, h:h + 1] = m_new
            h_ref[rows, vv] = _head_out(hh, nw_ref[:, vv], og_ref[rows, vv])


def _mlstm_sample(q, k, v, og, g, gt, norm_w, C0, n0, m0, NP, TS):
    Bs, H, DK, DV = C0.shape
    BS = SAMPLE_BS
    assert Bs % BS == 0 and NP % (BS * TS) == 0
    R = BS * TS
    off = NP // R
    row = lambda i: (off + i, 0)
    gt_s = gt[:, NP:].reshape(2 * H, Bs // BS, R).transpose(1, 0, 2)
    m0_s = m0.reshape(Bs // BS, BS, H)
    outs = pl.pallas_call(
        functools.partial(_mlstm_sample_body, H=H, DK=DK, DV=DV, TS=TS),
        grid=(Bs // BS,),
        in_specs=[pl.BlockSpec((R, H * DK), row), pl.BlockSpec((R, H * DK), row),
                  pl.BlockSpec((R, H * DV), row), pl.BlockSpec((R, H * DV), row),
                  pl.BlockSpec((R, 2 * H), row),
                  pl.BlockSpec((1, 2 * H, R), lambda i: (i, 0, 0)),
                  pl.BlockSpec((1, H * DV), lambda i: (0, 0)),
                  pl.BlockSpec((BS, H, DK, DV), lambda i: (i, 0, 0, 0)),
                  pl.BlockSpec((BS, H, DK), lambda i: (i, 0, 0)),
                  pl.BlockSpec((1, BS, H), lambda i: (i, 0, 0))],
        out_specs=[pl.BlockSpec((R, H * DV), lambda i: (i, 0)),
                   pl.BlockSpec((BS, H, DK, DV), lambda i: (i, 0, 0, 0)),
                   pl.BlockSpec((BS, H, DK), lambda i: (i, 0, 0)),
                   pl.BlockSpec((1, BS, H), lambda i: (i, 0, 0))],
        out_shape=[jax.ShapeDtypeStruct((Bs * TS, H * DV), BF16),
                   jax.ShapeDtypeStruct((Bs, H, DK, DV), F32),
                   jax.ShapeDtypeStruct((Bs, H, DK), F32),
                   jax.ShapeDtypeStruct((Bs // BS, BS, H), F32)],
        compiler_params=_cparams("parallel"),
        name="mlstm_sample",
    )(q, k, v, og, g, gt_s, norm_w, C0, n0, m0_s)
    return outs[0], outs[1], outs[2], outs[3].reshape(Bs, H)


def _layer_norm(z, gain, bias):
    mu = jnp.mean(z, axis=1, keepdims=True)
    zc = z - mu
    var = jnp.mean(zc * zc, axis=1, keepdims=True)
    return zc * lax.rsqrt(var + LN_EPS) * gain + bias


def _proj_norm_body(ap_ref, as_ref, w_ref, x_ref, gain_ref, bias_ref, o_ref, ob_ref, *, alpha, npb):
    a = jnp.where(pl.program_id(0) < npb, ap_ref[...], as_ref[...])
    y = _layer_norm(alpha * x_ref[...] + _dot(a, w_ref[...]), gain_ref[...], bias_ref[...])
    o_ref[...] = y
    ob_ref[...] = y.astype(BF16)


def _proj_norm(a_prompt, a_sample, w, x, gain, bias, alpha):
    NP, K = a_prompt.shape
    NS = a_sample.shape[0]
    N, D = NP + NS, w.shape[1]
    tm = _pick(NS, (256, 128))
    assert NP % tm == 0
    npb = NP // tm
    return pl.pallas_call(
        functools.partial(_proj_norm_body, alpha=alpha, npb=npb),
        grid=(N // tm,),
        in_specs=[pl.BlockSpec((tm, K), lambda i: (jnp.minimum(i, npb - 1), 0)),
                  pl.BlockSpec((tm, K), lambda i: (jnp.maximum(i - npb, 0), 0)),
                  pl.BlockSpec((K, D), lambda i: (0, 0)),
                  pl.BlockSpec((tm, D), lambda i: (i, 0)),
                  pl.BlockSpec((1, D), lambda i: (0, 0)),
                  pl.BlockSpec((1, D), lambda i: (0, 0))],
        out_specs=[pl.BlockSpec((tm, D), lambda i: (i, 0)), pl.BlockSpec((tm, D), lambda i: (i, 0))],
        out_shape=[jax.ShapeDtypeStruct((N, D), F32), jax.ShapeDtypeStruct((N, D), BF16)],
        compiler_params=_cparams("parallel"),
        name="proj_norm",
    )(a_prompt, a_sample, w, x, gain[None, :], bias[None, :])


def _combine_norm_body(x_ref, y0_ref, y1_ref, gain_ref, bias_ref, o_ref, ob_ref, *, alpha):
    y = _layer_norm(alpha * x_ref[...] + (y0_ref[...] + y1_ref[...]), gain_ref[...], bias_ref[...])
    o_ref[...] = y
    ob_ref[...] = y.astype(BF16)


def _combine_norm(x, y2, gain, bias, alpha):
    N, D = x.shape
    tm = _pick(N, ROW_TILES)
    nb = N // tm
    return pl.pallas_call(
        functools.partial(_combine_norm_body, alpha=alpha),
        grid=(nb,),
        in_specs=[pl.BlockSpec((tm, D), lambda i: (i, 0)),
                  pl.BlockSpec((tm, D), lambda i: (i, 0)),
                  pl.BlockSpec((tm, D), lambda i: (i + nb, 0)),
                  pl.BlockSpec((1, D), lambda i: (0, 0)),
                  pl.BlockSpec((1, D), lambda i: (0, 0))],
        out_specs=[pl.BlockSpec((tm, D), lambda i: (i, 0)), pl.BlockSpec((tm, D), lambda i: (i, 0))],
        out_shape=[jax.ShapeDtypeStruct((N, D), F32), jax.ShapeDtypeStruct((N, D), BF16)],
        compiler_params=_cparams("parallel"),
        name="combine_norm",
    )(x, y2, y2, gain[None, :], bias[None, :])


def _router_body(x_ref, wt_ref, b_ref, eid_ref, gate_ref, *, E):
    logits = _dot3(wt_ref[...], x_ref[...], _NT)
    ex = jnp.exp(logits - jnp.max(logits, axis=0, keepdims=True))
    probs = ex / jnp.sum(ex, axis=0, keepdims=True)
    sel = probs + b_ref[...]
    per = E // N_GROUPS
    s = [sel[e:e + 1, :] for e in range(E)]
    p = [probs[e:e + 1, :] for e in range(E)]

    def top2_sum(vals):
        best = None
        for a in range(len(vals)):
            for b in range(a + 1, len(vals)):
                pair = vals[a] + vals[b]
                best = pair if best is None else jnp.maximum(best, pair)
        return best

    g_best = top2_sum(s[:per])
    g_idx = jnp.zeros_like(g_best, dtype=jnp.int32)
    for gi in range(1, N_GROUPS):
        score = top2_sum(s[gi * per:(gi + 1) * per])
        better = score > g_best
        g_idx = jnp.where(better, gi, g_idx)
        g_best = jnp.where(better, score, g_best)

    def in_group(vals, j):
        out = vals[j]
        for gi in range(1, N_GROUPS):
            out = jnp.where(g_idx == gi, vals[gi * per + j], out)
        return out

    sg = [in_group(s, j) for j in range(per)]
    pg = [in_group(p, j) for j in range(per)]
    i1 = jnp.zeros_like(g_idx)
    b1 = sg[0]
    for j in range(1, per):
        better = sg[j] > b1
        i1 = jnp.where(better, j, i1)
        b1 = jnp.where(better, sg[j], b1)
    i2 = jnp.zeros_like(g_idx)
    b2 = jnp.full_like(b1, -jnp.inf)
    for j in range(per):
        better = (i1 != j) & (sg[j] > b2)
        i2 = jnp.where(better, j, i2)
        b2 = jnp.where(better, sg[j], b2)
    p1 = pg[0]
    p2 = pg[0]
    for j in range(1, per):
        p1 = jnp.where(i1 == j, pg[j], p1)
        p2 = jnp.where(i2 == j, pg[j], p2)
    tot = p1 + p2
    eid_ref[...] = jnp.concatenate([g_idx * per + i1, g_idx * per + i2], axis=0)
    gate_ref[...] = jnp.concatenate([p1 / tot, p2 / tot], axis=0)


def _router(x, w_router, b_router):
    N, D = x.shape
    E = w_router.shape[1]
    assert E % N_GROUPS == 0 and E // N_GROUPS >= 2
    tm = _pick(N, ROW_TILES)
    return pl.pallas_call(
        functools.partial(_router_body, E=E),
        grid=(N // tm,),
        in_specs=[pl.BlockSpec((tm, D), lambda i: (i, 0)),
                  pl.BlockSpec((E, D), lambda i: (0, 0)),
                  pl.BlockSpec((E, 1), lambda i: (0, 0))],
        out_specs=[pl.BlockSpec((2, tm), lambda i: (0, i)), pl.BlockSpec((2, tm), lambda i: (0, i))],
        out_shape=[jax.ShapeDtypeStruct((2, N), jnp.int32), jax.ShapeDtypeStruct((2, N), F32)],
        compiler_params=_cparams("parallel"),
        name="router",
    )(x, w_router.T, b_router.astype(F32)[:, None])


def _route_plan(eid, gate, E, TM):
    N = eid.shape[1]
    A = 2 * N
    e_flat = eid.reshape(A)
    onehot = (e_flat[:, None] == jnp.arange(E, dtype=jnp.int32)[None, :]).astype(jnp.int32)
    before = jnp.cumsum(onehot, axis=0) - onehot
    rank = jnp.sum(before * onehot, axis=1)
    counts = jnp.sum(onehot, axis=0)
    padded = (counts + TM - 1) // TM * TM
    pend = jnp.cumsum(padded)
    pstart = pend - padded
    slot = pstart[e_flat] + rank
    n_blocks = -(-(A + E * (TM - 1)) // TM)
    n_slots = n_blocks * TM
    a_idx = jnp.arange(A, dtype=jnp.int32)
    slot_tok = jnp.zeros((n_slots,), jnp.int32).at[slot].set(a_idx % N)
    slot_dst = jnp.zeros((n_slots,), jnp.int32).at[slot].set(a_idx)
    slot_gate = jnp.zeros((n_slots,), F32).at[slot].set(gate.reshape(A))
    n_used = (pend[-1] // TM).astype(jnp.int32)
    blk = jnp.arange(n_blocks, dtype=jnp.int32)
    blk_expert = jnp.minimum(jnp.searchsorted(pend, blk * TM, side="right"), E - 1).astype(jnp.int32)
    blk_rows = jnp.clip((pstart + counts)[blk_expert] - blk * TM, 0, TM).astype(jnp.int32)
    blk_rows = jnp.where(blk < n_used, blk_rows, 0)
    blk_expert = jnp.where(blk < n_used, blk_expert, blk_expert[jnp.maximum(n_used - 1, 0)])
    return (blk_expert, blk_rows, n_used.reshape(1), slot_tok.reshape(n_blocks, 1, TM),
            slot_dst.reshape(n_blocks, 1, TM), slot_gate.reshape(n_slots, 1))


def _moe_body(be_ref, rows_ref, nu_ref, tok_ref, tok_next_ref, dst_ref, gate_ref, x_hbm, wg_ref, wu_ref, wd_ref,
              y_hbm, xbuf, obuf, gsem, ssem, *, TM):
    del be_ref
    i = pl.program_id(0)
    nu = nu_ref[0]
    slot = lax.rem(i, 2)

    def gather_row(idx_ref, r, s):
        return pltpu.make_async_copy(x_hbm.at[pl.ds(idx_ref[0, 0, r], 1), :], xbuf.at[s, pl.ds(r, 1), :], gsem.at[s])

    def scatter_row(r, s):
        return pltpu.make_async_copy(obuf.at[s, pl.ds(r, 1), :], y_hbm.at[pl.ds(dst_ref[0, 0, r], 1), :], ssem.at[s])

    def for_rows(n, fn, **kw):
        def body(r, c):
            fn(r)
            return c
        lax.fori_loop(0, n, body, 0, **kw)

    def gather_all(fn):
        for_rows(TM, fn, unroll=8)

    def scatter_wait(blk, s):
        for_rows(rows_ref[blk], lambda r: scatter_row(r, s).wait())

    @pl.when(i == 0)
    def _():
        gather_all(lambda r: gather_row(tok_ref, r, 0).start())

    @pl.when(i + 1 < nu)
    def _():
        gather_all(lambda r: gather_row(tok_next_ref, r, 1 - slot).start())

    @pl.when(i < nu)
    def _():
        gather_all(lambda r: gather_row(tok_ref, r, slot).wait())
        xb = xbuf[slot].astype(BF16)
        hidden = jax.nn.silu(_dot(xb, wg_ref[...])) * _dot(xb, wu_ref[...])
        y = _dot(hidden.astype(BF16), wd_ref[...]) * gate_ref[...]

        @pl.when(i >= 2)
        def _():
            scatter_wait(i - 2, slot)

        obuf[slot] = y
        for_rows(rows_ref[i], lambda r: scatter_row(r, slot).start())

        @pl.when(i == nu - 1)
        def _():
            scatter_wait(i, slot)

            @pl.when(i >= 1)
            def _():
                scatter_wait(i - 1, 1 - slot)


def _moe(x, eid, gate, w_gate, w_up, w_down):
    N, D = x.shape
    E, _, F = w_gate.shape
    TM = MOE_TM
    blk_expert, blk_rows, n_used, slot_tok, slot_dst, slot_gate = _route_plan(eid, gate, E, TM)
    n_blocks = slot_tok.shape[0]
    idx_spec = lambda off: pl.BlockSpec((1, 1, TM), lambda i, be, br, nu: (jnp.minimum(i + off, n_blocks - 1), 0, 0),
                                        memory_space=pltpu.SMEM)
    grid_spec = pltpu.PrefetchScalarGridSpec(
        num_scalar_prefetch=3,
        grid=(n_blocks,),
        in_specs=[idx_spec(0), idx_spec(1), idx_spec(0),
                  pl.BlockSpec((TM, 1), lambda i, be, br, nu: (i, 0)),
                  pl.BlockSpec(memory_space=pl.ANY),
                  pl.BlockSpec((None, D, F), lambda i, be, br, nu: (be[i], 0, 0)),
                  pl.BlockSpec((None, D, F), lambda i, be, br, nu: (be[i], 0, 0)),
                  pl.BlockSpec((None, F, D), lambda i, be, br, nu: (be[i], 0, 0))],
        out_specs=pl.BlockSpec(memory_space=pl.ANY),
        scratch_shapes=[pltpu.VMEM((2, TM, D), F32), pltpu.VMEM((2, TM, D), F32),
                        pltpu.SemaphoreType.DMA((2,)), pltpu.SemaphoreType.DMA((2,))],
    )
    return pl.pallas_call(
        functools.partial(_moe_body, TM=TM),
        grid_spec=grid_spec,
        out_shape=jax.ShapeDtypeStruct((2 * N, D), F32),
        compiler_params=_cparams("arbitrary"),
        name="moe_experts",
    )(blk_expert, blk_rows, n_used, slot_tok, slot_tok, slot_dst, slot_gate, x, w_gate, w_up, w_down)


def _moe_layer(x, w_router, b_router, w_gate, w_up, w_down, gain, bias, alpha):
    eid, gate = _router(x, w_router, b_router)
    y2 = _moe(x, eid, gate, w_gate.astype(BF16), w_up.astype(BF16), w_down.astype(BF16))
    return _combine_norm(x, y2, gain, bias, alpha)


def _sink_attention(qs, kc, vc, valid, sink_col, scale):
    s = jnp.where(valid, _dg(qs, kc, _NT) * scale, -jnp.inf)
    mx = jnp.maximum(jnp.max(s, axis=1, keepdims=True), sink_col)
    e = jnp.exp(s - mx)
    den = jnp.sum(e, axis=1, keepdims=True) + jnp.exp(sink_col - mx)
    p = e * (1.0 / den)
    return _dot(p.astype(BF16), vc)


def _swa_prompt_body(sink_ref, q_ref, kp_ref, kc_ref, vp_ref, vc_ref, o_ref, *, KVH, G, HD):
    W = q_ref.shape[0]
    first = pl.program_id(1) == 0
    r = lax.broadcasted_iota(jnp.int32, (G * W, 2 * W), 0)
    kj = lax.broadcasted_iota(jnp.int32, (G * W, 2 * W), 1)
    diff = r % W + W - kj
    valid = (diff >= 0) & (diff < W) & (jnp.logical_not(first) | (kj >= W))
    for g in range(KVH):
        ks = slice(g * HD, (g + 1) * HD)
        kc = jnp.concatenate([kp_ref[:, ks], kc_ref[:, ks]], axis=0)
        vc = jnp.concatenate([vp_ref[:, ks], vc_ref[:, ks]], axis=0)
        heads = [g * G + j for j in range(G)]
        qs = jnp.concatenate([q_ref[:, hq * HD:(hq + 1) * HD] for hq in heads], axis=0)
        sink_col = jnp.concatenate([jnp.full((W, 1), sink_ref[hq], F32) for hq in heads], axis=0)
        o = _sink_attention(qs, kc, vc, valid, sink_col, HD ** -0.5)
        for j, hq in enumerate(heads):
            o_ref[:, hq * HD:(hq + 1) * HD] = o[j * W:(j + 1) * W, :].astype(BF16)


def _swa_prompt(q, k, v, sinks, B, T, KVH, HD):
    Hq = sinks.shape[0]
    W = WINDOW
    assert T % W == 0
    nb = T // W
    cur = lambda b, n: (b * nb + n, 0)
    prev = lambda b, n: (b * nb + jnp.maximum(n - 1, 0), 0)
    return pl.pallas_call(
        functools.partial(_swa_prompt_body, KVH=KVH, G=Hq // KVH, HD=HD),
        grid=(B, nb),
        in_specs=[pl.BlockSpec(memory_space=pltpu.SMEM),
                  pl.BlockSpec((W, Hq * HD), cur),
                  pl.BlockSpec((W, KVH * HD), prev), pl.BlockSpec((W, KVH * HD), cur),
                  pl.BlockSpec((W, KVH * HD), prev), pl.BlockSpec((W, KVH * HD), cur)],
        out_specs=pl.BlockSpec((W, Hq * HD), cur),
        out_shape=jax.ShapeDtypeStruct((B * T, Hq * HD), BF16),
        compiler_params=_cparams("parallel", "parallel"),
        name="swa_prompt",
    )(sinks, q, k, k, v, v)


def _swa_sample_body(sink_ref, q_ref, k_ref, v_ref, o_ref, *, KVH, G, HD, TS, WB):
    K = k_ref.shape[1]
    q = q_ref[...].astype(F32)
    r = lax.broadcasted_iota(jnp.int32, (G * TS, K), 0)
    kj = lax.broadcasted_iota(jnp.int32, (G * TS, K), 1)
    diff = WB + r % TS - kj
    valid = (diff >= 0) & (diff < WINDOW)
    for bb in range(SAMPLE_BS):
        rows = slice(bb * TS, (bb + 1) * TS)
        for g in range(KVH):
            ks = slice(g * HD, (g + 1) * HD)
            heads = [g * G + j for j in range(G)]
            qs = jnp.concatenate([q[rows, hq * HD:(hq + 1) * HD] for hq in heads], axis=0).astype(BF16)
            sink_col = jnp.concatenate([jnp.full((TS, 1), sink_ref[hq], F32) for hq in heads], axis=0)
            o = _sink_attention(qs, k_ref[bb, :, ks], v_ref[bb, :, ks], valid, sink_col, HD ** -0.5)
            for j, hq in enumerate(heads):
                o_ref[rows, hq * HD:(hq + 1) * HD] = o[j * TS:(j + 1) * TS, :].astype(BF16)


def _swa_sample(q, k_all, v_all, sinks, NP, TS, WB, KVH, HD):
    Bs, K, _ = k_all.shape
    Hq = sinks.shape[0]
    BS = SAMPLE_BS
    R = BS * TS
    off = NP // R
    row = lambda i: (off + i, 0)
    return pl.pallas_call(
        functools.partial(_swa_sample_body, KVH=KVH, G=Hq // KVH, HD=HD, TS=TS, WB=WB),
        grid=(Bs // BS,),
        in_specs=[pl.BlockSpec(memory_space=pltpu.SMEM),
                  pl.BlockSpec((R, Hq * HD), row),
                  pl.BlockSpec((BS, K, KVH * HD), lambda i: (i, 0, 0)),
                  pl.BlockSpec((BS, K, KVH * HD), lambda i: (i, 0, 0))],
        out_specs=pl.BlockSpec((R, Hq * HD), lambda i: (i, 0)),
        out_shape=jax.ShapeDtypeStruct((Bs * TS, Hq * HD), BF16),
        compiler_params=_cparams("parallel"),
        name="swa_sample",
    )(sinks, q, k_all, v_all)


def _rope_tables(pos, head_dim, lanes):
    half = head_dim // 2
    inv = ROPE_THETA ** (-jnp.arange(half, dtype=F32) / half)
    ang = pos.astype(F32)[:, None] * inv[None, :]
    reps = lanes // half
    return jnp.tile(jnp.cos(ang), (1, reps)), jnp.tile(jnp.sin(ang), (1, reps))


def kernel(x_prompt, x_sample, state_C, state_n, state_m, cache_k, cache_v, w_in_a, b_i_a, b_f_a, norm_a, w_out_a,
           w_kv, w_q_b, sinks_b, w_out_b, w_router, b_router, w_gate_e, w_up_e, w_down_e, ln_g, ln_b):
    B, T, D = x_prompt.shape
    Bs, TS, _ = x_sample.shape
    depth = ln_g.shape[0]
    assert depth == 2 and w_in_a.shape[0] == 1 and w_q_b.shape[0] == 1
    alpha = (2 * depth) ** 0.25
    NP, NS = B * T, Bs * TS
    H = b_i_a.shape[1]
    DK, DV = state_C.shape[3], state_C.shape[4]
    WB, KVH, HD = cache_k.shape[1], cache_k.shape[2], cache_k.shape[3]
    E = w_router.shape[1]

    x0 = jnp.concatenate([x_prompt.reshape(NP, D), x_sample.reshape(NS, D)], axis=0).astype(F32)
    x0b = x0.astype(BF16)

    w_in = w_in_a[0]
    nqk, nv = H * DK, H * DV
    w_qkv = w_in[:, :2 * nqk + nv].astype(BF16)
    w_og = w_in[:, 2 * nqk + nv:2 * nqk + nv + D].astype(BF16)
    w_gates = w_in[:, 2 * nqk + nv + D:].astype(F32)
    qkv_scale = jnp.concatenate([jnp.ones((nqk,), F32), jnp.full((nqk,), DK ** -0.5, F32), jnp.ones((nv,), F32)])
    qkv = _matmul(x0b, w_qkv, qkv_scale[None, :], BF16)
    q, k, v = qkv[:, :nqk], qkv[:, nqk:2 * nqk], qkv[:, 2 * nqk:]
    og = _matmul(x0b, w_og, jnp.ones((1, D), F32), F32)
    g, gt = _mlstm_gates(x0, w_gates, b_i_a[0], b_f_a[0])
    norm_w = norm_a[0].astype(F32)[None, :]
    h_p, C_p, n_p, m_p = _mlstm_prompt(q, k, v, og, g, gt, norm_w, B, T, H)
    h_s, C_s, n_s, m_s = _mlstm_sample(q, k, v, og, g, gt, norm_w, state_C[0].astype(F32), state_n[0].astype(F32),
                                       state_m[0].astype(F32), NP, TS)
    x1, _ = _proj_norm(h_p, h_s, w_out_a[0].astype(BF16), x0, ln_g[0, 0], ln_b[0, 0], alpha)
    x2, x2b = _moe_layer(x1, w_router, b_router, w_gate_e[0], w_up_e[0], w_down_e[0], ln_g[0, 1], ln_b[0, 1], alpha)

    pos = jnp.concatenate([jnp.tile(jnp.arange(T, dtype=jnp.int32), B),
                           jnp.tile(PAST_LEN + jnp.arange(TS, dtype=jnp.int32), Bs)])
    cos, sin = _rope_tables(pos, HD, 128)
    nkv = KVH * HD
    k_new = _matmul_rope(x2b, w_kv[:, :nkv].astype(BF16), cos, sin, HD, F32)
    v_new = _matmul(x2b, w_kv[:, nkv:].astype(BF16), jnp.ones((1, nkv), F32), F32)
    qb = _matmul_rope(x2b, w_q_b[0].astype(BF16), cos, sin, HD, BF16)
    o_p = _swa_prompt(qb, k_new.astype(BF16), v_new.astype(BF16), sinks_b[0].astype(F32), B, T, KVH, HD)
    k_s_new = k_new[NP:].reshape(Bs, TS, nkv)
    v_s_new = v_new[NP:].reshape(Bs, TS, nkv)
    k_cat = jnp.concatenate([cache_k.reshape(Bs, WB, nkv).astype(F32), k_s_new], axis=1)
    v_cat = jnp.concatenate([cache_v.reshape(Bs, WB, nkv).astype(F32), v_s_new], axis=1)
    kpad = -(-(WB + TS) // 128) * 128 - (WB + TS)
    k_all = jnp.pad(k_cat, ((0, 0), (0, kpad), (0, 0))).astype(BF16)
    v_all = jnp.pad(v_cat, ((0, 0), (0, kpad), (0, 0))).astype(BF16)
    o_s = _swa_sample(qb, k_all, v_all, sinks_b[0].astype(F32), NP, TS, WB, KVH, HD)
    x3, _ = _proj_norm(o_p, o_s, w_out_b[0].astype(BF16), x2, ln_g[1, 0], ln_b[1, 0], alpha)
    x4, _ = _moe_layer(x3, w_router, b_router, w_gate_e[1], w_up_e[1], w_down_e[1], ln_g[1, 1], ln_b[1, 1], alpha)

    y_prompt = x4[:NP].reshape(B, T, D)
    y_sample = x4[NP:].reshape(Bs, TS, D)
    k_p = k_new[:NP].reshape(B, T, KVH, HD)[:, -WB:]
    v_p = v_new[:NP].reshape(B, T, KVH, HD)[:, -WB:]
    k_s = k_cat[:, -WB:].reshape(Bs, WB, KVH, HD)
    v_s = v_cat[:, -WB:].reshape(Bs, WB, KVH, HD)
    return (y_prompt, y_sample, C_p[None], n_p[None], m_p.reshape(1, B, H), k_p, v_p,
            C_s[None], n_s[None], m_s[None], k_s, v_s)
```

```python
import functools
import math

import jax
import jax.numpy as jnp
from jax import lax
from jax.experimental import pallas as pl
from jax.experimental.pallas import tpu as pltpu

F32 = jnp.float32
BF16 = jnp.bfloat16

GATE_CAP = 15.0
LN_EPS = 1e-5
ROPE_THETA = 10000.0
WINDOW = 128
PAST_LEN = 8192
N_GROUPS = 4

VMEM_LIMIT_BYTES = 56 * 1024 * 1024
MLSTM_CHUNK = 128
MOE_TM = 256
MOE_GROUP_BLOCKS = 6
MOE_FC = 256
GATE_COLS = 8
LANES = 128
SAMPLE_BS = 2
ROW_TILES = (512, 256, 128)

_NT = (((1,), (1,)), ((), ()))
_TN = (((0,), (0,)), ((), ()))


def _pick(n, cands):
    for c in cands:
        if n % c == 0:
            return c
    raise ValueError(f"no tile in {cands} divides {n}")


def _cparams(*sem):
    return pltpu.CompilerParams(dimension_semantics=sem, vmem_limit_bytes=VMEM_LIMIT_BYTES)


def _dot(a, b):
    return jnp.dot(a, b, preferred_element_type=F32)


def _dg(a, b, dims):
    return lax.dot_general(a, b, dims, preferred_element_type=F32)


def _mm_scale_body(a_ref, b_ref, s_ref, o_ref):
    o_ref[...] = (_dot(a_ref[...], b_ref[...]) * s_ref[...]).astype(o_ref.dtype)


def _matmul(a, b, col_scale, out_dtype):
    M, K = a.shape
    N = b.shape[1]
    tm = _pick(M, ROW_TILES)
    tn = _pick(N, (1024, 512, 256))
    return pl.pallas_call(
        _mm_scale_body,
        grid=(M // tm, N // tn),
        in_specs=[pl.BlockSpec((tm, K), lambda i, j: (i, 0)),
                  pl.BlockSpec((K, tn), lambda i, j: (0, j)),
                  pl.BlockSpec((1, tn), lambda i, j: (0, j))],
        out_specs=pl.BlockSpec((tm, tn), lambda i, j: (i, j)),
        out_shape=jax.ShapeDtypeStruct((M, N), out_dtype),
        compiler_params=_cparams("parallel", "parallel"),
        name="matmul",
    )(a, b, col_scale)


def _mm_rope_body(a_ref, b_ref, cos_ref, sin_ref, o_ref, *, half, scale):
    acc = _dot(a_ref[...], b_ref[...])
    tn = acc.shape[1]
    reps = tn // cos_ref.shape[1]
    cos = jnp.concatenate([cos_ref[...]] * reps, axis=1)
    sin = jnp.concatenate([sin_ref[...]] * reps, axis=1)
    lane = lax.broadcasted_iota(jnp.int32, acc.shape, 1)
    partner = jnp.where(lane % (2 * half) < half, -pltpu.roll(acc, tn - half, 1), pltpu.roll(acc, half, 1))
    o_ref[...] = ((acc * cos + partner * sin) * scale).astype(o_ref.dtype)


def _matmul_rope(a, b, cos, sin, head_dim, out_dtype, scale=1.0):
    M, K = a.shape
    N = b.shape[1]
    tm = _pick(M, ROW_TILES)
    tn = _pick(N, (1024, 512, 256))
    assert math.frexp(scale)[0] == 0.5
    return pl.pallas_call(
        functools.partial(_mm_rope_body, half=head_dim // 2, scale=scale),
        grid=(M // tm, N // tn),
        in_specs=[pl.BlockSpec((tm, K), lambda i, j: (i, 0)),
                  pl.BlockSpec((K, tn), lambda i, j: (0, j)),
                  pl.BlockSpec((tm, cos.shape[1]), lambda i, j: (i, 0)),
                  pl.BlockSpec((tm, sin.shape[1]), lambda i, j: (i, 0))],
        out_specs=pl.BlockSpec((tm, tn), lambda i, j: (i, j)),
        out_shape=jax.ShapeDtypeStruct((M, N), out_dtype),
        compiler_params=_cparams("parallel", "parallel"),
        name="matmul_rope",
    )(a, b, cos, sin)


def _split_bf16(x):
    hi = x.astype(BF16)
    lo = (x - hi.astype(F32)).astype(BF16)
    return hi, lo


def _dot3(a, b, dims):
    ah, al = _split_bf16(a)
    bh, bl = _split_bf16(b)
    return _dg(ah, bh, dims) + _dg(al, bh, dims) + _dg(ah, bl, dims)


def _softcap(z):
    return GATE_CAP * jnp.tanh(z / GATE_CAP)


def _log_sigmoid(z):
    return jnp.minimum(z, 0.0) - jnp.log1p(jnp.exp(-jnp.abs(z)))


def _gates_body(x_ref, w_ref, wt_ref, brow_ref, bcol_ref, g_ref, gt_ref, *, H):
    x = x_ref[...]
    pre = _dot3(x, w_ref[...], (((1,), (0,)), ((), ()))) + brow_ref[...]
    pre_t = _dot3(wt_ref[...], x, _NT) + bcol_ref[...]
    z = _softcap(pre)
    zt = _softcap(pre_t)
    is_in = lax.broadcasted_iota(jnp.int32, z.shape, 1) < H
    is_in_t = lax.broadcasted_iota(jnp.int32, zt.shape, 0) < H
    g_ref[...] = jnp.where(is_in, z, _log_sigmoid(z))
    gt_ref[...] = jnp.where(is_in_t, zt, _log_sigmoid(zt))


def _mlstm_gates(x, w_gates, b_i, b_f):
    N, D = x.shape
    H = b_i.shape[0]
    tm = _pick(N, ROW_TILES)
    bias = jnp.concatenate([b_i, b_f]).astype(F32)
    return pl.pallas_call(
        functools.partial(_gates_body, H=H),
        grid=(N // tm,),
        in_specs=[pl.BlockSpec((tm, D), lambda i: (i, 0)),
                  pl.BlockSpec((D, 2 * H), lambda i: (0, 0)),
                  pl.BlockSpec((2 * H, D), lambda i: (0, 0)),
                  pl.BlockSpec((1, 2 * H), lambda i: (0, 0)),
                  pl.BlockSpec((2 * H, 1), lambda i: (0, 0))],
        out_specs=[pl.BlockSpec((tm, 2 * H), lambda i: (i, 0)),
                   pl.BlockSpec((2 * H, tm), lambda i: (0, i))],
        out_shape=[jax.ShapeDtypeStruct((N, 2 * H), F32), jax.ShapeDtypeStruct((2 * H, N), F32)],
        compiler_params=_cparams("parallel"),
        name="mlstm_gates",
    )(x, w_gates, w_gates.T, bias[None, :], bias[:, None])


def _mlstm_head(qh, kh, vh, ig_col, lf_col, ig_row, lf_row, C, n_row, m_prev, mxu_dtype):
    L = qh.shape[0]
    t_idx = lax.broadcasted_iota(jnp.int32, (L, L), 0)
    s_idx = lax.broadcasted_iota(jnp.int32, (L, L), 1)
    causal = s_idx <= t_idx

    def mxu(x):
        return x.astype(BF16).astype(mxu_dtype)

    b_col = jnp.sum(jnp.where(causal, lf_row, 0.0), axis=1, keepdims=True)
    b_row = jnp.sum(jnp.where(t_idx <= s_idx, lf_col, 0.0), axis=0, keepdims=True)
    dmat = jnp.where(causal, b_col - b_row + ig_row, -jnp.inf)
    a_col = b_col + m_prev
    m_t = jnp.maximum(a_col, jnp.max(dmat, axis=1, keepdims=True))
    wmat = jnp.exp(dmat - m_t) * _dg(qh, kh, _NT)
    inter = jnp.exp(a_col - m_t)
    qf = qh.astype(F32)
    kf = kh.astype(F32)
    num = inter * _dot(qh, mxu(C)) + _dot(mxu(wmat), vh)
    den = inter * jnp.sum(qf * n_row, axis=1, keepdims=True) + jnp.sum(wmat, axis=1, keepdims=True)
    hh = num * (1.0 / jnp.maximum(jnp.abs(den), jnp.exp(-m_t)))
    m_new = m_t[L - 1:L, :]
    b_last = b_col[L - 1:L, :]
    decay = jnp.exp(b_last - b_col + ig_col - m_new)
    carry = jnp.exp(b_last + m_prev - m_new)
    kd = kf * decay
    C_new = carry * C + _dg(mxu(kd), vh, _TN)
    n_new = carry * n_row + jnp.sum(kd, axis=0, keepdims=True)
    return hh, C_new, n_new, m_new


def _head_out(hh, nw, og):
    hn = hh * lax.rsqrt(jnp.mean(hh * hh, axis=1, keepdims=True) + LN_EPS)
    return (hn * nw * jax.nn.sigmoid(og)).astype(BF16)


def _mlstm_prompt_body(q_ref, k_ref, v_ref, og_ref, g_ref, gt_ref, nw_ref,
                       h_ref, c_ref, n_ref, m_ref, *, H, DK, DV):
    @pl.when(pl.program_id(1) == 0)
    def _():
        c_ref[...] = jnp.zeros_like(c_ref)
        n_ref[...] = jnp.zeros_like(n_ref)
        m_ref[...] = jnp.zeros_like(m_ref)

    g = g_ref[...]
    gt = gt_ref[...]
    for h in range(H):
        kq = slice(h * DK, (h + 1) * DK)
        vv = slice(h * DV, (h + 1) * DV)
        hh, C_new, n_new, m_new = _mlstm_head(
            q_ref[:, kq], k_ref[:, kq], v_ref[:, vv],
            g[:, h:h + 1], g[:, H + h:H + h + 1], gt[h:h + 1, :], gt[H + h:H + h + 1, :],
            c_ref[0, h], n_ref[0, h:h + 1, :], m_ref[0, :, h:h + 1], BF16)
        c_ref[0, h] = C_new
        n_ref[0, h:h + 1, :] = n_new
        m_ref[0, :, h:h + 1] = m_new
        h_ref[:, vv] = _head_out(hh, nw_ref[:, vv], og_ref[:, vv])


def _mlstm_prompt(q, k, v, og, g, gt, norm_w, B, T, H):
    DK = q.shape[1] // H
    DV = v.shape[1] // H
    L = MLSTM_CHUNK
    assert T % L == 0
    nc = T // L
    row = lambda b, c: (b * nc + c, 0)
    return pl.pallas_call(
        functools.partial(_mlstm_prompt_body, H=H, DK=DK, DV=DV),
        grid=(B, nc),
        in_specs=[pl.BlockSpec((L, H * DK), row), pl.BlockSpec((L, H * DK), row),
                  pl.BlockSpec((L, H * DV), row), pl.BlockSpec((L, H * DV), row),
                  pl.BlockSpec((L, 2 * H), row),
                  pl.BlockSpec((2 * H, L), lambda b, c: (0, b * nc + c)),
                  pl.BlockSpec((1, H * DV), lambda b, c: (0, 0))],
        out_specs=[pl.BlockSpec((L, H * DV), row),
                   pl.BlockSpec((1, H, DK, DV), lambda b, c: (b, 0, 0, 0)),
                   pl.BlockSpec((1, H, DK), lambda b, c: (b, 0, 0)),
                   pl.BlockSpec((1, 1, H), lambda b, c: (b, 0, 0))],
        out_shape=[jax.ShapeDtypeStruct((B * T, H * DV), BF16),
                   jax.ShapeDtypeStruct((B, H, DK, DV), F32),
                   jax.ShapeDtypeStruct((B, H, DK), F32),
                   jax.ShapeDtypeStruct((B, 1, H), F32)],
        compiler_params=_cparams("parallel", "arbitrary"),
        name="mlstm_prompt",
    )(q, k, v, og, g, gt, norm_w)


def _mlstm_sample_body(q_ref, k_ref, v_ref, og_ref, g_ref, gt_ref, nw_ref, c0_ref, n0_ref, m0_ref,
                       h_ref, c_ref, n_ref, m_ref, *, H, DK, DV, TS):
    q = q_ref[...].astype(F32)
    k = k_ref[...].astype(F32)
    v = v_ref[...].astype(F32)
    g = g_ref[...]
    gt = gt_ref[0]
    for bb in range(SAMPLE_BS):
        rows = slice(bb * TS, (bb + 1) * TS)
        for h in range(H):
            kq = slice(h * DK, (h + 1) * DK)
            vv = slice(h * DV, (h + 1) * DV)
            hh, C_new, n_new, m_new = _mlstm_head(
                q[rows, kq], k[rows, kq], v[rows, vv],
                g[rows, h:h + 1], g[rows, H + h:H + h + 1], gt[h:h + 1, rows], gt[H + h:H + h + 1, rows],
                c0_ref[bb, h], n0_ref[bb, h:h + 1, :], m0_ref[0, bb:bb + 1, h:h + 1], F32)
            c_ref[bb, h] = C_new
            n_ref[bb, h:h + 1, :] = n_new
            m_ref[0, bb:bb + 1, h:h + 1] = m_new
            h_ref[rows, vv] = _head_out(hh, nw_ref[:, vv], og_ref[rows, vv])


def _mlstm_sample(q, k, v, og, g, gt, norm_w, C0, n0, m0, NP, TS):
    Bs, H, DK, DV = C0.shape
    BS = SAMPLE_BS
    assert Bs % BS == 0 and NP % (BS * TS) == 0
    R = BS * TS
    off = NP // R
    row = lambda i: (off + i, 0)
    gt_s = gt[:, NP:].reshape(2 * H, Bs // BS, R).transpose(1, 0, 2)
    m0_s = m0.reshape(Bs // BS, BS, H)
    outs = pl.pallas_call(
        functools.partial(_mlstm_sample_body, H=H, DK=DK, DV=DV, TS=TS),
        grid=(Bs // BS,),
        in_specs=[pl.BlockSpec((R, H * DK), row), pl.BlockSpec((R, H * DK), row),
                  pl.BlockSpec((R, H * DV), row), pl.BlockSpec((R, H * DV), row),
                  pl.BlockSpec((R, 2 * H), row),
                  pl.BlockSpec((1, 2 * H, R), lambda i: (i, 0, 0)),
                  pl.BlockSpec((1, H * DV), lambda i: (0, 0)),
                  pl.BlockSpec((BS, H, DK, DV), lambda i: (i, 0, 0, 0)),
                  pl.BlockSpec((BS, H, DK), lambda i: (i, 0, 0)),
                  pl.BlockSpec((1, BS, H), lambda i: (i, 0, 0))],
        out_specs=[pl.BlockSpec((R, H * DV), lambda i: (i, 0)),
                   pl.BlockSpec((BS, H, DK, DV), lambda i: (i, 0, 0, 0)),
                   pl.BlockSpec((BS, H, DK), lambda i: (i, 0, 0)),
                   pl.BlockSpec((1, BS, H), lambda i: (i, 0, 0))],
        out_shape=[jax.ShapeDtypeStruct((Bs * TS, H * DV), BF16),
                   jax.ShapeDtypeStruct((Bs, H, DK, DV), F32),
                   jax.ShapeDtypeStruct((Bs, H, DK), F32),
                   jax.ShapeDtypeStruct((Bs // BS, BS, H), F32)],
        compiler_params=_cparams("parallel"),
        name="mlstm_sample",
    )(q, k, v, og, g, gt_s, norm_w, C0, n0, m0_s)
    return outs[0], outs[1], outs[2], outs[3].reshape(Bs, H)


def _layer_norm(z, gain, bias):
    mu = jnp.mean(z, axis=1, keepdims=True)
    zc = z - mu
    var = jnp.mean(zc * zc, axis=1, keepdims=True)
    return zc * lax.rsqrt(var + LN_EPS) * gain + bias


def _proj_norm_body(ap_ref, as_ref, w_ref, x_ref, gain_ref, bias_ref, o_ref, ob_ref, *, alpha, npb):
    a = jnp.where(pl.program_id(0) < npb, ap_ref[...], as_ref[...])
    y = _layer_norm(alpha * x_ref[...] + _dot(a, w_ref[...]), gain_ref[...], bias_ref[...])
    o_ref[...] = y
    ob_ref[...] = y.astype(BF16)


def _proj_norm(a_prompt, a_sample, w, x, gain, bias, alpha):
    NP, K = a_prompt.shape
    NS = a_sample.shape[0]
    N, D = NP + NS, w.shape[1]
    tm = _pick(NS, (256, 128))
    assert NP % tm == 0
    npb = NP // tm
    return pl.pallas_call(
        functools.partial(_proj_norm_body, alpha=alpha, npb=npb),
        grid=(N // tm,),
        in_specs=[pl.BlockSpec((tm, K), lambda i: (jnp.minimum(i, npb - 1), 0)),
                  pl.BlockSpec((tm, K), lambda i: (jnp.maximum(i - npb, 0), 0)),
                  pl.BlockSpec((K, D), lambda i: (0, 0)),
                  pl.BlockSpec((tm, D), lambda i: (i, 0)),
                  pl.BlockSpec((1, D), lambda i: (0, 0)),
                  pl.BlockSpec((1, D), lambda i: (0, 0))],
        out_specs=[pl.BlockSpec((tm, D), lambda i: (i, 0)), pl.BlockSpec((tm, D), lambda i: (i, 0))],
        out_shape=[jax.ShapeDtypeStruct((N, D), F32), jax.ShapeDtypeStruct((N, D), BF16)],
        compiler_params=_cparams("parallel"),
        name="proj_norm",
    )(a_prompt, a_sample, w, x, gain[None, :], bias[None, :])


def _static_loop(n, fn):
    for r in range(n):
        fn(r)


def _rows_loop(n, fn, unroll=8):
    def body(r, c):
        fn(r)
        return c
    lax.fori_loop(0, n, body, 0, unroll=unroll)


def _combine_norm_body(slot_ref, slot_next_ref, gate_ref, x_ref, gain_ref, bias_ref, ys_hbm, o_ref, ob_ref,
                       ybuf, sem, *, alpha, nblk):
    i = pl.program_id(0)
    cur = lax.rem(i, 2)
    tm = x_ref.shape[0]

    def row_copy(idx_ref, k, r, s):
        return pltpu.make_async_copy(ys_hbm.at[pl.ds(idx_ref[k, r], 1), :], ybuf.at[s, k, pl.ds(r, 1), :], sem.at[s])

    def start_block(idx_ref, s):
        for k in range(2):
            _rows_loop(tm, lambda r: row_copy(idx_ref, k, r, s).start())

    @pl.when(i == 0)
    def _():
        start_block(slot_ref, 0)

    @pl.when(i + 1 < nblk)
    def _():
        start_block(slot_next_ref, 1 - cur)

    for k in range(2):
        _static_loop(tm, lambda r: row_copy(slot_ref, k, r, cur).wait())
    g = gate_ref[...]
    moe = g[:, 0:1] * ybuf[cur, 0] + g[:, 1:2] * ybuf[cur, 1]
    y = _layer_norm(alpha * x_ref[...] + moe, gain_ref[...], bias_ref[...])
    o_ref[...] = y
    ob_ref[...] = y.astype(BF16)


def _combine_norm(x, ys, slot, gate_col, gain, bias, alpha):
    N, D = x.shape
    tm = _pick(N, (256, 128))
    nblk = N // tm
    idx_spec = lambda off: pl.BlockSpec((2, tm), lambda i: (0, jnp.minimum(i + off, nblk - 1)),
                                        memory_space=pltpu.SMEM)
    return pl.pallas_call(
        functools.partial(_combine_norm_body, alpha=alpha, nblk=nblk),
        grid=(nblk,),
        in_specs=[idx_spec(0), idx_spec(1),
                  pl.BlockSpec((tm, gate_col.shape[1]), lambda i: (i, 0)),
                  pl.BlockSpec((tm, D), lambda i: (i, 0)),
                  pl.BlockSpec((1, D), lambda i: (0, 0)),
                  pl.BlockSpec((1, D), lambda i: (0, 0)),
                  pl.BlockSpec(memory_space=pl.ANY)],
        out_specs=[pl.BlockSpec((tm, D), lambda i: (i, 0)), pl.BlockSpec((tm, D), lambda i: (i, 0))],
        out_shape=[jax.ShapeDtypeStruct((N, D), F32), jax.ShapeDtypeStruct((N, D), BF16)],
        scratch_shapes=[pltpu.VMEM((2, 2, tm, D), F32), pltpu.SemaphoreType.DMA((2,))],
        compiler_params=_cparams("arbitrary"),
        name="combine_norm",
    )(slot, slot, gate_col, x, gain[None, :], bias[None, :], ys)


def _router_body(x_ref, wt_ref, b_ref, eid_ref, gate_ref, *, E):
    logits = _dot3(wt_ref[...], x_ref[...], _NT)
    ex = jnp.exp(logits - jnp.max(logits, axis=0, keepdims=True))
    probs = ex / jnp.sum(ex, axis=0, keepdims=True)
    sel = probs + b_ref[...]
    per = E // N_GROUPS
    s = [sel[e:e + 1, :] for e in range(E)]
    p = [probs[e:e + 1, :] for e in range(E)]

    def top2_sum(vals):
        best = None
        for a in range(len(vals)):
            for b in range(a + 1, len(vals)):
                pair = vals[a] + vals[b]
                best = pair if best is None else jnp.maximum(best, pair)
        return best

    g_best = top2_sum(s[:per])
    g_idx = jnp.zeros_like(g_best, dtype=jnp.int32)
    for gi in range(1, N_GROUPS):
        score = top2_sum(s[gi * per:(gi + 1) * per])
        better = score > g_best
        g_idx = jnp.where(better, gi, g_idx)
        g_best = jnp.where(better, score, g_best)

    def in_group(vals, j):
        out = vals[j]
        for gi in range(1, N_GROUPS):
            out = jnp.where(g_idx == gi, vals[gi * per + j], out)
        return out

    sg = [in_group(s, j) for j in range(per)]
    pg = [in_group(p, j) for j in range(per)]
    i1 = jnp.zeros_like(g_idx)
    b1 = sg[0]
    for j in range(1, per):
        better = sg[j] > b1
        i1 = jnp.where(better, j, i1)
        b1 = jnp.where(better, sg[j], b1)
    i2 = jnp.zeros_like(g_idx)
    b2 = jnp.full_like(b1, -jnp.inf)
    for j in range(per):
        better = (i1 != j) & (sg[j] > b2)
        i2 = jnp.where(better, j, i2)
        b2 = jnp.where(better, sg[j], b2)
    p1 = pg[0]
    p2 = pg[0]
    for j in range(1, per):
        p1 = jnp.where(i1 == j, pg[j], p1)
        p2 = jnp.where(i2 == j, pg[j], p2)
    tot = p1 + p2
    eid_ref[...] = jnp.concatenate([g_idx * per + i1, g_idx * per + i2], axis=0)
    gates = jnp.concatenate([p1 / tot, p2 / tot, jnp.zeros((GATE_COLS - 2, p1.shape[1]), F32)], axis=0)
    gate_ref[...] = gates.T


def _router(x, w_router, b_router):
    N, D = x.shape
    E = w_router.shape[1]
    assert E % N_GROUPS == 0 and E // N_GROUPS >= 2
    tm = _pick(N, ROW_TILES)
    return pl.pallas_call(
        functools.partial(_router_body, E=E),
        grid=(N // tm,),
        in_specs=[pl.BlockSpec((tm, D), lambda i: (i, 0)),
                  pl.BlockSpec((E, D), lambda i: (0, 0)),
                  pl.BlockSpec((E, 1), lambda i: (0, 0))],
        out_specs=[pl.BlockSpec((2, tm), lambda i: (0, i)), pl.BlockSpec((tm, GATE_COLS), lambda i: (i, 0))],
        out_shape=[jax.ShapeDtypeStruct((2, N), jnp.int32), jax.ShapeDtypeStruct((N, GATE_COLS), F32)],
        compiler_params=_cparams("parallel"),
        name="router",
    )(x, w_router.T, b_router.astype(F32)[:, None])


def _plan_body(e_ref, slot_ref, cnt_ref, *, E, TM):
    e = e_ref[...]
    R = e.shape[0]
    upper = (lax.broadcasted_iota(jnp.int32, (LANES, LANES), 0)
             <= lax.broadcasted_iota(jnp.int32, (LANES, LANES), 1)).astype(BF16)
    below = (lax.broadcasted_iota(jnp.int32, (R, R), 1) < lax.broadcasted_iota(jnp.int32, (R, R), 0)).astype(BF16)
    lane = lax.broadcasted_iota(jnp.int32, (1, LANES), 1)
    slot = jnp.zeros(e.shape, F32)
    counts = jnp.zeros((1, LANES), F32)
    start = jnp.zeros((1, 1), F32)
    for ex in range(E):
        hit = e == ex
        within = _dot(hit.astype(BF16), upper)
        rows_before = _dot(below, within.astype(BF16))[:, LANES - 1:LANES]
        rank = within - 1.0 + rows_before
        count = rows_before[R - 1:R, :] + within[R - 1:R, LANES - 1:LANES]
        slot = jnp.where(hit, start + rank, slot)
        counts = jnp.where(lane == ex, count, counts)
        start = start + jnp.floor((count + (TM - 1)) * (1.0 / TM)) * TM
    slot_ref[...] = slot.astype(jnp.int32)
    cnt_ref[...] = jnp.broadcast_to(counts, cnt_ref.shape).astype(jnp.int32)


def _plan(eid, E, TM):
    N = eid.shape[1]
    A = 2 * N
    assert A % LANES == 0 and (TM & (TM - 1)) == 0
    R = A // LANES
    Rp = -(-R // LANES) * LANES
    e2d = jnp.pad(eid.reshape(R, LANES), ((0, Rp - R), (0, 0)), constant_values=-1)
    slot2d, cnt = pl.pallas_call(
        functools.partial(_plan_body, E=E, TM=TM),
        out_shape=[jax.ShapeDtypeStruct((Rp, LANES), jnp.int32), jax.ShapeDtypeStruct((8, LANES), jnp.int32)],
        compiler_params=pltpu.CompilerParams(vmem_limit_bytes=VMEM_LIMIT_BYTES),
        name="route_plan",
    )(e2d)
    return slot2d[:R].reshape(2, N), cnt[0, :E]


def _group_table(counts, TM, n_blocks):
    E = counts.shape[0]
    GB = MOE_GROUP_BLOCKS
    nblk = (counts + TM - 1) // TM
    blk_end = jnp.cumsum(nblk)
    blk_start = blk_end - nblk
    ngrp = (nblk + GB - 1) // GB
    grp_end = jnp.cumsum(ngrp)
    n_groups_max = E + n_blocks // GB
    g = jnp.arange(n_groups_max, dtype=jnp.int32)
    g_expert = jnp.minimum(jnp.sum((grp_end[None, :] <= g[:, None]).astype(jnp.int32), axis=1), E - 1)
    j = g - (grp_end - ngrp)[g_expert]
    g_start = blk_start[g_expert] + j * GB
    g_nblk = jnp.clip(nblk[g_expert] - j * GB, 0, GB)
    live = g < grp_end[-1]
    g_nblk = jnp.where(live, g_nblk, 0)
    meta = jnp.stack([grp_end[-1], blk_end[-1]]).astype(jnp.int32)
    pad_start = blk_start * TM + counts
    pad_count = nblk * TM - counts
    return (g_expert.astype(jnp.int32), g_start.astype(jnp.int32), g_nblk.astype(jnp.int32), meta,
            pad_start.astype(jnp.int32), pad_count.astype(jnp.int32))


def _pack_halves(x):
    half = x.shape[1] // 2
    bits = lax.bitcast_convert_type(x.astype(BF16).astype(F32), jnp.uint32)
    return (bits[:, :half] >> 16) | bits[:, half:]


def _unpack_halves(p):
    lo = lax.bitcast_convert_type(p << 16, F32).astype(BF16)
    hi = lax.bitcast_convert_type(p & jnp.uint32(0xFFFF0000), F32).astype(BF16)
    return lo, hi


def _dispatch_body(pad_start_ref, pad_count_ref, meta_ref, slot_ref, x_ref, xs_hbm, pbuf, zbuf, sem, zsem,
                   *, nblk, E, TM, n_blocks):
    i = pl.program_id(0)
    cur = lax.rem(i, 2)
    tm = x_ref.shape[0]

    def row_copy(k, r, s):
        return pltpu.make_async_copy(pbuf.at[s, pl.ds(r, 1), :], xs_hbm.at[pl.ds(slot_ref[k, r], 1), :], sem.at[s])

    def wait_block(s):
        for k in range(2):
            _static_loop(tm, lambda r: row_copy(k, r, s).wait())

    def zero_copy(row):
        return pltpu.make_async_copy(zbuf.at[pl.ds(0, 1), :], xs_hbm.at[pl.ds(row, 1), :], zsem.at[0])

    @pl.when(i == 0)
    def _():
        zbuf[...] = jnp.zeros_like(zbuf)
        for ex in range(E):
            _rows_loop(pad_count_ref[ex], lambda j: zero_copy(pad_start_ref[ex] + j).start(), unroll=1)
        tail0 = meta_ref[1] * TM
        ntail = n_blocks * TM - tail0
        _rows_loop(ntail, lambda j: zero_copy(tail0 + j).start(), unroll=1)
        for ex in range(E):
            _rows_loop(pad_count_ref[ex], lambda j: zero_copy(pad_start_ref[ex] + j).wait(), unroll=1)
        _rows_loop(ntail, lambda j: zero_copy(tail0 + j).wait(), unroll=1)

    @pl.when(i >= 2)
    def _():
        wait_block(cur)

    pbuf[cur] = _pack_halves(x_ref[...])
    for k in range(2):
        _rows_loop(tm, lambda r: row_copy(k, r, cur).start())

    @pl.when(i == nblk - 1)
    def _():
        wait_block(cur)
        if nblk >= 2:
            wait_block(1 - cur)


def _dispatch(x, slot, pad_start, pad_count, meta, TM, n_blocks):
    N, D = x.shape
    E = pad_start.shape[0]
    tm = _pick(N, (256, 128))
    nblk = N // tm
    grid_spec = pltpu.PrefetchScalarGridSpec(
        num_scalar_prefetch=3,
        grid=(nblk,),
        in_specs=[pl.BlockSpec((2, tm), lambda i, *_: (0, i), memory_space=pltpu.SMEM),
                  pl.BlockSpec((tm, D), lambda i, *_: (i, 0))],
        out_specs=pl.BlockSpec(memory_space=pl.ANY),
        scratch_shapes=[pltpu.VMEM((2, tm, D // 2), jnp.uint32), pltpu.VMEM((8, D // 2), jnp.uint32),
                        pltpu.SemaphoreType.DMA((2,)), pltpu.SemaphoreType.DMA((1,))],
    )
    return pl.pallas_call(
        functools.partial(_dispatch_body, nblk=nblk, E=E, TM=TM, n_blocks=n_blocks),
        grid_spec=grid_spec,
        out_shape=jax.ShapeDtypeStruct((n_blocks * TM, D // 2), jnp.uint32),
        compiler_params=_cparams("arbitrary"),
        name="moe_dispatch",
    )(pad_start, pad_count, meta, slot, x)


def _experts_body(ge_ref, gs_ref, gn_ref, meta_ref, xs_hbm, wg_hbm, wu_hbm, wd_hbm, ys_hbm,
                  xg, acc, stage_g, stage_u, stage_d, wgu, wdb, obuf, xsem, wsem, osem, *, chunks, TM, n_blocks):
    g = pl.program_id(0)
    n_groups = meta_ref[0]
    half = xg.shape[2]
    n_chunks = len(chunks)

    def x_copy(b0, rb):
        return pltpu.make_async_copy(xs_hbm.at[pl.ds((b0 + rb) * TM, TM), :], xg.at[rb], xsem.at[0])

    def w_copies(ex, c, s):
        f0, fw = chunks[c]
        return (pltpu.make_async_copy(wg_hbm.at[ex, :, pl.ds(f0, fw)], stage_g.at[s, :, pl.ds(0, fw)], wsem.at[s, 0]),
                pltpu.make_async_copy(wu_hbm.at[ex, :, pl.ds(f0, fw)], stage_u.at[s, :, pl.ds(0, fw)], wsem.at[s, 1]),
                pltpu.make_async_copy(wd_hbm.at[ex, pl.ds(f0, fw), :], stage_d.at[s, pl.ds(0, fw), :], wsem.at[s, 2]))

    def out_copy(b0, rb, s):
        return pltpu.make_async_copy(obuf.at[s], ys_hbm.at[pl.ds((b0 + rb) * TM, TM), :], osem.at[s])

    @pl.when(g < n_groups)
    def _():
        ex = ge_ref[g]
        b0 = gs_ref[g]
        nb = gn_ref[g]
        _rows_loop(nb, lambda rb: x_copy(b0, rb).start(), unroll=1)

        @pl.when(g == 0)
        def _():
            for cp in w_copies(ex, 0, 0):
                cp.start()

        _rows_loop(nb, lambda rb: x_copy(b0, rb).wait(), unroll=1)

        for c, (f0, fw) in enumerate(chunks):
            s = c % 2
            for cp in w_copies(ex, c, s):
                cp.wait()
            if c + 1 < n_chunks:
                for cp in w_copies(ex, c + 1, 1 - s):
                    cp.start()
            else:
                @pl.when(g + 1 < n_groups)
                def _():
                    for cp in w_copies(ge_ref[g + 1], 0, 1 - s):
                        cp.start()
            wgu[:, 0:fw] = stage_g[s, :, 0:fw].astype(BF16)
            wgu[:, fw:2 * fw] = stage_u[s, :, 0:fw].astype(BF16)
            wdb[0:fw, :] = stage_d[s, 0:fw, :].astype(BF16)

            def block(rb, carry, c=c, fw=fw):
                xl, xh = _unpack_halves(xg[rb])
                gu = _dot(xl, wgu[0:half, 0:2 * fw]) + _dot(xh, wgu[half:2 * half, 0:2 * fw])
                hidden = jax.nn.silu(gu[:, 0:fw]) * gu[:, fw:2 * fw]
                part = _dot(hidden.astype(BF16), wdb[0:fw, :])
                if c == 0:
                    acc[rb] = part
                elif c + 1 < n_chunks:
                    acc[rb] = acc[rb] + part
                else:
                    os_ = lax.rem(rb, 2)

                    @pl.when(rb >= 2)
                    def _():
                        out_copy(b0, rb, os_).wait()

                    obuf[os_] = acc[rb] + part
                    out_copy(b0, rb, os_).start()
                return carry

            lax.fori_loop(0, nb, block, 0)

        out_copy(b0, 0, lax.rem(nb - 1, 2)).wait()

        @pl.when(nb >= 2)
        def _():
            out_copy(b0, 0, lax.rem(nb, 2)).wait()

    @pl.when(g == pl.num_programs(0) - 1)
    def _():
        used = meta_ref[1]
        obuf[0] = jnp.zeros(obuf.shape[1:], F32)
        _rows_loop(n_blocks - used, lambda j: out_copy(used, j, 0).start(), unroll=1)
        _rows_loop(n_blocks - used, lambda j: out_copy(used, j, 0).wait(), unroll=1)


def _experts(xs, g_expert, g_start, g_nblk, meta, w_gate, w_up, w_down, TM, n_blocks):
    E, D, F = w_gate.shape
    GB = MOE_GROUP_BLOCKS
    FC = MOE_FC
    chunks = tuple((f0, min(FC, F - f0)) for f0 in range(0, F, FC))
    assert len(chunks) % 2 == 0 and all(fw % LANES == 0 for _, fw in chunks)
    grid_spec = pltpu.PrefetchScalarGridSpec(
        num_scalar_prefetch=4,
        grid=(g_expert.shape[0],),
        in_specs=[pl.BlockSpec(memory_space=pl.ANY)] * 4,
        out_specs=pl.BlockSpec(memory_space=pl.ANY),
        scratch_shapes=[pltpu.VMEM((GB, TM, D // 2), jnp.uint32), pltpu.VMEM((GB, TM, D), F32),
                        pltpu.VMEM((2, D, FC), F32), pltpu.VMEM((2, D, FC), F32), pltpu.VMEM((2, FC, D), F32),
                        pltpu.VMEM((D, 2 * FC), BF16), pltpu.VMEM((FC, D), BF16), pltpu.VMEM((2, TM, D), F32),
                        pltpu.SemaphoreType.DMA((1,)), pltpu.SemaphoreType.DMA((2, 3)),
                        pltpu.SemaphoreType.DMA((2,))],
    )
    return pl.pallas_call(
        functools.partial(_experts_body, chunks=chunks, TM=TM, n_blocks=n_blocks),
        grid_spec=grid_spec,
        out_shape=jax.ShapeDtypeStruct((n_blocks * TM, D), F32),
        compiler_params=_cparams("arbitrary"),
        name="moe_experts",
    )(g_expert, g_start, g_nblk, meta, xs, w_gate, w_up, w_down)


def _moe_layer(x, w_router, b_router, w_gate, w_up, w_down, gain, bias, alpha):
    N = x.shape[0]
    E = w_router.shape[1]
    TM = MOE_TM
    n_blocks = -(-(2 * N + E * (TM - 1)) // TM)
    eid, gate_col = _router(x, w_router, b_router)
    slot, counts = _plan(eid, E, TM)
    g_expert, g_start, g_nblk, meta, pad_start, pad_count = _group_table(counts, TM, n_blocks)
    xs = _dispatch(x, slot, pad_start, pad_count, meta, TM, n_blocks)
    ys = _experts(xs, g_expert, g_start, g_nblk, meta, w_gate, w_up, w_down, TM, n_blocks)
    return _combine_norm(x, ys, slot, gate_col, gain, bias, alpha)


def _pair_attention(q_pairs, kc, vc, valid, sinks_lo, sinks_hi):
    P = len(q_pairs)
    R, lanes = q_pairs[0].shape
    lo = lax.broadcasted_iota(jnp.int32, (R, lanes), 1) < lanes // 2
    zero = jnp.zeros((R, lanes), BF16)
    qs = jnp.concatenate([jnp.where(lo, qp, zero) for qp in q_pairs]
                         + [jnp.where(lo, zero, qp) for qp in q_pairs], axis=0)
    sink_col = jnp.concatenate([jnp.full((R, 1), s, F32) for s in list(sinks_lo) + list(sinks_hi)], axis=0)
    s = jnp.where(valid, _dg(qs, kc, _NT), -jnp.inf)
    mx = jnp.maximum(jnp.max(s, axis=1, keepdims=True), sink_col)
    e = jnp.exp(s - mx)
    den = jnp.sum(e, axis=1, keepdims=True) + jnp.exp(sink_col - mx)
    o = _dot(e.astype(BF16), vc) * (1.0 / den)
    return [jnp.where(lo, o[p * R:(p + 1) * R], o[(P + p) * R:(P + p + 1) * R]) for p in range(P)]


def _swa_prompt_body(sink_ref, q_ref, kp_ref, kc_ref, vp_ref, vc_ref, o_ref, *, KVH, G):
    W = q_ref.shape[0]
    P = G // 2
    first = pl.program_id(1) == 0
    r = lax.broadcasted_iota(jnp.int32, (G * W, 2 * W), 0)
    kj = lax.broadcasted_iota(jnp.int32, (G * W, 2 * W), 1)
    diff = r % W + W - kj
    valid = (diff >= 0) & (diff < W) & (jnp.logical_not(first) | (kj >= W))
    for g in range(KVH):
        ks = slice(g * LANES, (g + 1) * LANES)
        kc = jnp.concatenate([kp_ref[:, ks], kc_ref[:, ks]], axis=0)
        vc = jnp.concatenate([vp_ref[:, ks], vc_ref[:, ks]], axis=0)
        cols = [slice((g * P + p) * LANES, (g * P + p + 1) * LANES) for p in range(P)]
        outs = _pair_attention([q_ref[:, c] for c in cols], kc, vc, valid,
                               [sink_ref[g * G + 2 * p] for p in range(P)],
                               [sink_ref[g * G + 2 * p + 1] for p in range(P)])
        for c, o in zip(cols, outs):
            o_ref[:, c] = o.astype(BF16)


def _swa_prompt(q, kdup, vdup, sinks, B, T, KVH):
    Hq = sinks.shape[0]
    W = WINDOW
    assert T % W == 0 and kdup.shape[1] == KVH * LANES and q.shape[1] == Hq // 2 * LANES
    nb = T // W
    cur = lambda b, n: (b * nb + n, 0)
    prev = lambda b, n: (b * nb + jnp.maximum(n - 1, 0), 0)
    return pl.pallas_call(
        functools.partial(_swa_prompt_body, KVH=KVH, G=Hq // KVH),
        grid=(B, nb),
        in_specs=[pl.BlockSpec(memory_space=pltpu.SMEM),
                  pl.BlockSpec((W, q.shape[1]), cur),
                  pl.BlockSpec((W, KVH * LANES), prev), pl.BlockSpec((W, KVH * LANES), cur),
                  pl.BlockSpec((W, KVH * LANES), prev), pl.BlockSpec((W, KVH * LANES), cur)],
        out_specs=pl.BlockSpec((W, q.shape[1]), cur),
        out_shape=jax.ShapeDtypeStruct((B * T, q.shape[1]), BF16),
        compiler_params=_cparams("parallel", "parallel"),
        name="swa_prompt",
    )(sinks, q, kdup, kdup, vdup, vdup)


def _swa_sample_body(sink_ref, q_ref, k_ref, v_ref, o_ref, *, KVH, G, TS, WB):
    K = k_ref.shape[1]
    P = G // 2
    q = q_ref[...].astype(F32)
    r = lax.broadcasted_iota(jnp.int32, (G * TS, K), 0)
    kj = lax.broadcasted_iota(jnp.int32, (G * TS, K), 1)
    diff = WB + r % TS - kj
    valid = (diff >= 0) & (diff < WINDOW)
    for bb in range(SAMPLE_BS):
        rows = slice(bb * TS, (bb + 1) * TS)
        for g in range(KVH):
            ks = slice(g * LANES, (g + 1) * LANES)
            cols = [slice((g * P + p) * LANES, (g * P + p + 1) * LANES) for p in range(P)]
            outs = _pair_attention([q[rows, c].astype(BF16) for c in cols], k_ref[bb, :, ks], v_ref[bb, :, ks], valid,
                                   [sink_ref[g * G + 2 * p] for p in range(P)],
                                   [sink_ref[g * G + 2 * p + 1] for p in range(P)])
            for c, o in zip(cols, outs):
                o_ref[rows, c] = o.astype(BF16)


def _swa_sample(q, k_all, v_all, sinks, NP, TS, WB, KVH):
    Bs, K, _ = k_all.shape
    Hq = sinks.shape[0]
    BS = SAMPLE_BS
    R = BS * TS
    off = NP // R
    row = lambda i: (off + i, 0)
    return pl.pallas_call(
        functools.partial(_swa_sample_body, KVH=KVH, G=Hq // KVH, TS=TS, WB=WB),
        grid=(Bs // BS,),
        in_specs=[pl.BlockSpec(memory_space=pltpu.SMEM),
                  pl.BlockSpec((R, q.shape[1]), row),
                  pl.BlockSpec((BS, K, KVH * LANES), lambda i: (i, 0, 0)),
                  pl.BlockSpec((BS, K, KVH * LANES), lambda i: (i, 0, 0))],
        out_specs=pl.BlockSpec((R, q.shape[1]), lambda i: (i, 0)),
        out_shape=jax.ShapeDtypeStruct((Bs * TS, q.shape[1]), BF16),
        compiler_params=_cparams("parallel"),
        name="swa_sample",
    )(sinks, q, k_all, v_all)


def _rope_tables(pos, head_dim, lanes):
    half = head_dim // 2
    inv = ROPE_THETA ** (-jnp.arange(half, dtype=F32) / half)
    ang = pos.astype(F32)[:, None] * inv[None, :]
    reps = lanes // half
    return jnp.tile(jnp.cos(ang), (1, reps)), jnp.tile(jnp.sin(ang), (1, reps))


def kernel(x_prompt, x_sample, state_C, state_n, state_m, cache_k, cache_v, w_in_a, b_i_a, b_f_a, norm_a, w_out_a,
           w_kv, w_q_b, sinks_b, w_out_b, w_router, b_router, w_gate_e, w_up_e, w_down_e, ln_g, ln_b):
    B, T, D = x_prompt.shape
    Bs, TS, _ = x_sample.shape
    depth = ln_g.shape[0]
    assert depth == 2 and w_in_a.shape[0] == 1 and w_q_b.shape[0] == 1
    alpha = (2 * depth) ** 0.25
    NP, NS = B * T, Bs * TS
    H = b_i_a.shape[1]
    DK, DV = state_C.shape[3], state_C.shape[4]
    WB, KVH, HD = cache_k.shape[1], cache_k.shape[2], cache_k.shape[3]
    E = w_router.shape[1]

    x0 = jnp.concatenate([x_prompt.reshape(NP, D), x_sample.reshape(NS, D)], axis=0).astype(F32)
    x0b = x0.astype(BF16)

    w_in = w_in_a[0]
    nqk, nv = H * DK, H * DV
    w_qkv = w_in[:, :2 * nqk + nv].astype(BF16)
    w_og = w_in[:, 2 * nqk + nv:2 * nqk + nv + D].astype(BF16)
    w_gates = w_in[:, 2 * nqk + nv + D:].astype(F32)
    qkv_scale = jnp.concatenate([jnp.ones((nqk,), F32), jnp.full((nqk,), DK ** -0.5, F32), jnp.ones((nv,), F32)])
    qkv = _matmul(x0b, w_qkv, qkv_scale[None, :], BF16)
    q, k, v = qkv[:, :nqk], qkv[:, nqk:2 * nqk], qkv[:, 2 * nqk:]
    og = _matmul(x0b, w_og, jnp.ones((1, D), F32), F32)
    g, gt = _mlstm_gates(x0, w_gates, b_i_a[0], b_f_a[0])
    norm_w = norm_a[0].astype(F32)[None, :]
    h_p, C_p, n_p, m_p = _mlstm_prompt(q, k, v, og, g, gt, norm_w, B, T, H)
    h_s, C_s, n_s, m_s = _mlstm_sample(q, k, v, og, g, gt, norm_w, state_C[0].astype(F32), state_n[0].astype(F32),
                                       state_m[0].astype(F32), NP, TS)
    x1, _ = _proj_norm(h_p, h_s, w_out_a[0].astype(BF16), x0, ln_g[0, 0], ln_b[0, 0], alpha)
    x2, x2b = _moe_layer(x1, w_router, b_router, w_gate_e[0], w_up_e[0], w_down_e[0], ln_g[0, 1], ln_b[0, 1], alpha)

    pos = jnp.concatenate([jnp.tile(jnp.arange(T, dtype=jnp.int32), B),
                           jnp.tile(PAST_LEN + jnp.arange(TS, dtype=jnp.int32), Bs)])
    cos, sin = _rope_tables(pos, HD, 128)
    nkv = KVH * HD
    k_new = _matmul_rope(x2b, w_kv[:, :nkv].astype(BF16), cos, sin, HD, F32)
    v_new = _matmul(x2b, w_kv[:, nkv:].astype(BF16), jnp.ones((1, nkv), F32), F32)
    assert 2 * HD == LANES
    qb = _matmul_rope(x2b, w_q_b[0].astype(BF16), cos, sin, HD, BF16, scale=HD ** -0.5)

    def dup_heads(a):
        a4 = a.reshape(a.shape[:-1] + (KVH, HD)).astype(BF16)
        return jnp.concatenate([a4, a4], axis=-1).reshape(a.shape[:-1] + (KVH * LANES,))

    o_p = _swa_prompt(qb, dup_heads(k_new), dup_heads(v_new), sinks_b[0].astype(F32), B, T, KVH)
    k_s_new = k_new[NP:].reshape(Bs, TS, nkv)
    v_s_new = v_new[NP:].reshape(Bs, TS, nkv)
    k_cat = jnp.concatenate([cache_k.reshape(Bs, WB, nkv).astype(F32), k_s_new], axis=1)
    v_cat = jnp.concatenate([cache_v.reshape(Bs, WB, nkv).astype(F32), v_s_new], axis=1)
    kpad = -(-(WB + TS) // LANES) * LANES - (WB + TS)
    k_all = dup_heads(jnp.pad(k_cat, ((0, 0), (0, kpad), (0, 0))))
    v_all = dup_heads(jnp.pad(v_cat, ((0, 0), (0, kpad), (0, 0))))
    o_s = _swa_sample(qb, k_all, v_all, sinks_b[0].astype(F32), NP, TS, WB, KVH)
    x3, _ = _proj_norm(o_p, o_s, w_out_b[0].astype(BF16), x2, ln_g[1, 0], ln_b[1, 0], alpha)
    x4, _ = _moe_layer(x3, w_router, b_router, w_gate_e[1], w_up_e[1], w_down_e[1], ln_g[1, 1], ln_b[1, 1], alpha)

    y_prompt = x4[:NP].reshape(B, T, D)
    y_sample = x4[NP:].reshape(Bs, TS, D)
    k_p = k_new[:NP].reshape(B, T, KVH, HD)[:, -WB:]
    v_p = v_new[:NP].reshape(B, T, KVH, HD)[:, -WB:]
    k_s = k_cat[:, -WB:].reshape(Bs, WB, KVH, HD)
    v_s = v_cat[:, -WB:].reshape(Bs, WB, KVH, HD)
    return (y_prompt, y_sample, C_p[None], n_p[None], m_p.reshape(1, B, H), k_p, v_p,
            C_s[None], n_s[None], m_s[None], k_s, v_s)
```

```python
import functools
import math

import jax
import jax.numpy as jnp
from jax import lax
from jax.experimental import pallas as pl
from jax.experimental.pallas import tpu as pltpu

F32 = jnp.float32
BF16 = jnp.bfloat16

GATE_CAP = 15.0
LN_EPS = 1e-5
ROPE_THETA = 10000.0
WINDOW = 128
PAST_LEN = 8192
N_GROUPS = 4

VMEM_LIMIT_BYTES = 56 * 1024 * 1024
MLSTM_CHUNK = 128
MOE_TM = 256
MOE_GROUP_BLOCKS = 6
MOE_FC = 256
GATE_COLS = 8
LANES = 128
SAMPLE_BS = 2
ROW_TILES = (512, 256, 128)
MATMUL_ROW_TILES = (1024,) + ROW_TILES

_NT = (((1,), (1,)), ((), ()))
_TN = (((0,), (0,)), ((), ()))


def _pick(n, cands):
    for c in cands:
        if n % c == 0:
            return c
    raise ValueError(f"no tile in {cands} divides {n}")


def _cparams(*sem):
    return pltpu.CompilerParams(dimension_semantics=sem, vmem_limit_bytes=VMEM_LIMIT_BYTES)


def _dot(a, b):
    return jnp.dot(a, b, preferred_element_type=F32)


def _dg(a, b, dims):
    return lax.dot_general(a, b, dims, preferred_element_type=F32)


def _mm_scale_body(a_ref, b_ref, s_ref, o_ref):
    o_ref[...] = (_dot(a_ref[...], b_ref[...]) * s_ref[...]).astype(o_ref.dtype)


def _matmul(a, b, col_scale, out_dtype):
    M, K = a.shape
    N = b.shape[1]
    tm = _pick(M, MATMUL_ROW_TILES)
    tn = _pick(N, (1024, 512, 256))
    return pl.pallas_call(
        _mm_scale_body,
        grid=(M // tm, N // tn),
        in_specs=[pl.BlockSpec((tm, K), lambda i, j: (i, 0)),
                  pl.BlockSpec((K, tn), lambda i, j: (0, j)),
                  pl.BlockSpec((1, tn), lambda i, j: (0, j))],
        out_specs=pl.BlockSpec((tm, tn), lambda i, j: (i, j)),
        out_shape=jax.ShapeDtypeStruct((M, N), out_dtype),
        compiler_params=_cparams("parallel", "parallel"),
        name="matmul",
    )(a, b, col_scale)


def _mm_rope_body(a_ref, b_ref, cos_ref, sin_ref, o_ref, *, half, scale):
    acc = _dot(a_ref[...], b_ref[...])
    tn = acc.shape[1]
    reps = tn // cos_ref.shape[1]
    cos = jnp.concatenate([cos_ref[...]] * reps, axis=1)
    sin = jnp.concatenate([sin_ref[...]] * reps, axis=1)
    lane = lax.broadcasted_iota(jnp.int32, acc.shape, 1)
    partner = jnp.where(lane % (2 * half) < half, -pltpu.roll(acc, tn - half, 1), pltpu.roll(acc, half, 1))
    o_ref[...] = ((acc * cos + partner * sin) * scale).astype(o_ref.dtype)


def _matmul_rope(a, b, cos, sin, head_dim, out_dtype, scale=1.0):
    M, K = a.shape
    N = b.shape[1]
    tm = _pick(M, MATMUL_ROW_TILES)
    tn = _pick(N, (1024, 512, 256))
    assert math.frexp(scale)[0] == 0.5
    return pl.pallas_call(
        functools.partial(_mm_rope_body, half=head_dim // 2, scale=scale),
        grid=(M // tm, N // tn),
        in_specs=[pl.BlockSpec((tm, K), lambda i, j: (i, 0)),
                  pl.BlockSpec((K, tn), lambda i, j: (0, j)),
                  pl.BlockSpec((tm, cos.shape[1]), lambda i, j: (i, 0)),
                  pl.BlockSpec((tm, sin.shape[1]), lambda i, j: (i, 0))],
        out_specs=pl.BlockSpec((tm, tn), lambda i, j: (i, j)),
        out_shape=jax.ShapeDtypeStruct((M, N), out_dtype),
        compiler_params=_cparams("parallel", "parallel"),
        name="matmul_rope",
    )(a, b, cos, sin)


def _split_bf16(x):
    hi = x.astype(BF16)
    lo = (x - hi.astype(F32)).astype(BF16)
    return hi, lo


def _dot3(a, b, dims):
    ah, al = _split_bf16(a)
    bh, bl = _split_bf16(b)
    return _dg(ah, bh, dims) + _dg(al, bh, dims) + _dg(ah, bl, dims)


def _softcap(z):
    return GATE_CAP * jnp.tanh(z / GATE_CAP)


def _log_sigmoid(z):
    return jnp.minimum(z, 0.0) - jnp.log1p(jnp.exp(-jnp.abs(z)))


def _gates_body(x_ref, w_ref, wt_ref, brow_ref, bcol_ref, g_ref, gt_ref, *, H):
    x = x_ref[...]
    pre = _dot3(x, w_ref[...], (((1,), (0,)), ((), ()))) + brow_ref[...]
    pre_t = _dot3(wt_ref[...], x, _NT) + bcol_ref[...]
    z = _softcap(pre)
    zt = _softcap(pre_t)
    is_in = lax.broadcasted_iota(jnp.int32, z.shape, 1) < H
    is_in_t = lax.broadcasted_iota(jnp.int32, zt.shape, 0) < H
    g_ref[...] = jnp.where(is_in, z, _log_sigmoid(z))
    gt_ref[...] = jnp.where(is_in_t, zt, _log_sigmoid(zt))


def _mlstm_gates(x, w_gates, b_i, b_f):
    N, D = x.shape
    H = b_i.shape[0]
    tm = _pick(N, ROW_TILES)
    bias = jnp.concatenate([b_i, b_f]).astype(F32)
    return pl.pallas_call(
        functools.partial(_gates_body, H=H),
        grid=(N // tm,),
        in_specs=[pl.BlockSpec((tm, D), lambda i: (i, 0)),
                  pl.BlockSpec((D, 2 * H), lambda i: (0, 0)),
                  pl.BlockSpec((2 * H, D), lambda i: (0, 0)),
                  pl.BlockSpec((1, 2 * H), lambda i: (0, 0)),
                  pl.BlockSpec((2 * H, 1), lambda i: (0, 0))],
        out_specs=[pl.BlockSpec((tm, 2 * H), lambda i: (i, 0)),
                   pl.BlockSpec((2 * H, tm), lambda i: (0, i))],
        out_shape=[jax.ShapeDtypeStruct((N, 2 * H), F32), jax.ShapeDtypeStruct((2 * H, N), F32)],
        compiler_params=_cparams("parallel"),
        name="mlstm_gates",
    )(x, w_gates, w_gates.T, bias[None, :], bias[:, None])


def _mlstm_head(qh, kh, vh, ig_col, lf_col, ig_row, lf_row, C, n_row, m_prev, mxu_dtype):
    L = qh.shape[0]
    t_idx = lax.broadcasted_iota(jnp.int32, (L, L), 0)
    s_idx = lax.broadcasted_iota(jnp.int32, (L, L), 1)
    causal = s_idx <= t_idx

    def mxu(x):
        return x.astype(BF16).astype(mxu_dtype)

    b_col = jnp.sum(jnp.where(causal, lf_row, 0.0), axis=1, keepdims=True)
    b_row = jnp.sum(jnp.where(t_idx <= s_idx, lf_col, 0.0), axis=0, keepdims=True)
    dmat = jnp.where(causal, b_col - b_row + ig_row, -jnp.inf)
    a_col = b_col + m_prev
    m_t = jnp.maximum(a_col, jnp.max(dmat, axis=1, keepdims=True))
    wmat = jnp.exp(dmat - m_t) * _dg(qh, kh, _NT)
    inter = jnp.exp(a_col - m_t)
    qf = qh.astype(F32)
    kf = kh.astype(F32)
    num = inter * _dot(qh, mxu(C)) + _dot(mxu(wmat), vh)
    den = inter * jnp.sum(qf * n_row, axis=1, keepdims=True) + jnp.sum(wmat, axis=1, keepdims=True)
    hh = num * (1.0 / jnp.maximum(jnp.abs(den), jnp.exp(-m_t)))
    m_new = m_t[L - 1:L, :]
    b_last = b_col[L - 1:L, :]
    decay = jnp.exp(b_last - b_col + ig_col - m_new)
    carry = jnp.exp(b_last + m_prev - m_new)
    kd = kf * decay
    C_new = carry * C + _dg(mxu(kd), vh, _TN)
    n_new = carry * n_row + jnp.sum(kd, axis=0, keepdims=True)
    return hh, C_new, n_new, m_new


def _head_out(hh, nw, og):
    hn = hh * lax.rsqrt(jnp.mean(hh * hh, axis=1, keepdims=True) + LN_EPS)
    return (hn * nw * jax.nn.sigmoid(og)).astype(BF16)


def _mlstm_prompt_body(q_ref, k_ref, v_ref, og_ref, g_ref, gt_ref, nw_ref,
                       h_ref, c_ref, n_ref, m_ref, *, H, DK, DV):
    @pl.when(pl.program_id(1) == 0)
    def _():
        c_ref[...] = jnp.zeros_like(c_ref)
        n_ref[...] = jnp.zeros_like(n_ref)
        m_ref[...] = jnp.zeros_like(m_ref)

    g = g_ref[...]
    gt = gt_ref[...]
    for h in range(H):
        kq = slice(h * DK, (h + 1) * DK)
        vv = slice(h * DV, (h + 1) * DV)
        hh, C_new, n_new, m_new = _mlstm_head(
            q_ref[:, kq], k_ref[:, kq], v_ref[:, vv],
            g[:, h:h + 1], g[:, H + h:H + h + 1], gt[h:h + 1, :], gt[H + h:H + h + 1, :],
            c_ref[0, h], n_ref[0, h:h + 1, :], m_ref[0, :, h:h + 1], BF16)
        c_ref[0, h] = C_new
        n_ref[0, h:h + 1, :] = n_new
        m_ref[0, :, h:h + 1] = m_new
        h_ref[:, vv] = _head_out(hh, nw_ref[:, vv], og_ref[:, vv])


def _mlstm_prompt(qkv, og, g, gt, norm_w, B, T, H, DK, DV):
    L = MLSTM_CHUNK
    assert T % L == 0 and (2 * H * DK) % (H * DV) == 0
    nc = T // L
    v_blk = 2 * DK // DV
    row = lambda b, c: (b * nc + c, 0)
    return pl.pallas_call(
        functools.partial(_mlstm_prompt_body, H=H, DK=DK, DV=DV),
        grid=(B, nc),
        in_specs=[pl.BlockSpec((L, H * DK), row), pl.BlockSpec((L, H * DK), lambda b, c: (b * nc + c, 1)),
                  pl.BlockSpec((L, H * DV), lambda b, c: (b * nc + c, v_blk)), pl.BlockSpec((L, H * DV), row),
                  pl.BlockSpec((L, 2 * H), row),
                  pl.BlockSpec((2 * H, L), lambda b, c: (0, b * nc + c)),
                  pl.BlockSpec((1, H * DV), lambda b, c: (0, 0))],
        out_specs=[pl.BlockSpec((L, H * DV), row),
                   pl.BlockSpec((1, H, DK, DV), lambda b, c: (b, 0, 0, 0)),
                   pl.BlockSpec((1, H, DK), lambda b, c: (b, 0, 0)),
                   pl.BlockSpec((1, 1, H), lambda b, c: (b, 0, 0))],
        out_shape=[jax.ShapeDtypeStruct((B * T, H * DV), BF16),
                   jax.ShapeDtypeStruct((B, H, DK, DV), F32),
                   jax.ShapeDtypeStruct((B, H, DK), F32),
                   jax.ShapeDtypeStruct((B, 1, H), F32)],
        compiler_params=_cparams("parallel", "arbitrary"),
        name="mlstm_prompt",
    )(qkv, qkv, qkv, og, g, gt, norm_w)


def _mlstm_sample_body(q_ref, k_ref, v_ref, og_ref, g_ref, gt_ref, nw_ref, c0_ref, n0_ref, m0_ref,
                       h_ref, c_ref, n_ref, m_ref, *, H, DK, DV, TS):
    q = q_ref[...].astype(F32)
    k = k_ref[...].astype(F32)
    v = v_ref[...].astype(F32)
    g = g_ref[...]
    gt = gt_ref[0]
    for bb in range(SAMPLE_BS):
        rows = slice(bb * TS, (bb + 1) * TS)
        for h in range(H):
            kq = slice(h * DK, (h + 1) * DK)
            vv = slice(h * DV, (h + 1) * DV)
            hh, C_new, n_new, m_new = _mlstm_head(
                q[rows, kq], k[rows, kq], v[rows, vv],
                g[rows, h:h + 1], g[rows, H + h:H + h + 1], gt[h:h + 1, rows], gt[H + h:H + h + 1, rows],
                c0_ref[bb, h], n0_ref[bb, h:h + 1, :], m0_ref[0, bb:bb + 1, h:h + 1], F32)
            c_ref[bb, h] = C_new
            n_ref[bb, h:h + 1, :] = n_new
            m_ref[0, bb:bb + 1, h:h + 1] = m_new
            h_ref[rows, vv] = _head_out(hh, nw_ref[:, vv], og_ref[rows, vv])


def _mlstm_sample(qkv, og, g, gt, norm_w, C0, n0, m0, NP, TS):
    Bs, H, DK, DV = C0.shape
    BS = SAMPLE_BS
    assert Bs % BS == 0 and NP % (BS * TS) == 0
    R = BS * TS
    off = NP // R
    row = lambda i: (off + i, 0)
    gt_s = gt[:, NP:].reshape(2 * H, Bs // BS, R).transpose(1, 0, 2)
    m0_s = m0.reshape(Bs // BS, BS, H)
    outs = pl.pallas_call(
        functools.partial(_mlstm_sample_body, H=H, DK=DK, DV=DV, TS=TS),
        grid=(Bs // BS,),
        in_specs=[pl.BlockSpec((R, H * DK), row), pl.BlockSpec((R, H * DK), lambda i: (off + i, 1)),
                  pl.BlockSpec((R, H * DV), lambda i: (off + i, 2 * DK // DV)), pl.BlockSpec((R, H * DV), row),
                  pl.BlockSpec((R, 2 * H), row),
                  pl.BlockSpec((1, 2 * H, R), lambda i: (i, 0, 0)),
                  pl.BlockSpec((1, H * DV), lambda i: (0, 0)),
                  pl.BlockSpec((BS, H, DK, DV), lambda i: (i, 0, 0, 0)),
                  pl.BlockSpec((BS, H, DK), lambda i: (i, 0, 0)),
                  pl.BlockSpec((1, BS, H), lambda i: (i, 0, 0))],
        out_specs=[pl.BlockSpec((R, H * DV), lambda i: (i, 0)),
                   pl.BlockSpec((BS, H, DK, DV), lambda i: (i, 0, 0, 0)),
                   pl.BlockSpec((BS, H, DK), lambda i: (i, 0, 0)),
                   pl.BlockSpec((1, BS, H), lambda i: (i, 0, 0))],
        out_shape=[jax.ShapeDtypeStruct((Bs * TS, H * DV), BF16),
                   jax.ShapeDtypeStruct((Bs, H, DK, DV), F32),
                   jax.ShapeDtypeStruct((Bs, H, DK), F32),
                   jax.ShapeDtypeStruct((Bs // BS, BS, H), F32)],
        compiler_params=_cparams("parallel"),
        name="mlstm_sample",
    )(qkv, qkv, qkv, og, g, gt_s, norm_w, C0, n0, m0_s)
    return outs[0], outs[1], outs[2], outs[3].reshape(Bs, H)


def _layer_norm(z, gain, bias):
    mu = jnp.mean(z, axis=1, keepdims=True)
    zc = z - mu
    var = jnp.mean(zc * zc, axis=1, keepdims=True)
    return zc * lax.rsqrt(var + LN_EPS) * gain + bias


def _proj_norm_body(ap_ref, as_ref, w_ref, x_ref, gain_ref, bias_ref, o_ref, ob_ref, *, alpha, npb):
    a = jnp.where(pl.program_id(0) < npb, ap_ref[...], as_ref[...])
    y = _layer_norm(alpha * x_ref[...] + _dot(a, w_ref[...]), gain_ref[...], bias_ref[...])
    o_ref[...] = y
    ob_ref[...] = y.astype(BF16)


def _proj_norm(a_prompt, a_sample, w, x, gain, bias, alpha):
    NP, K = a_prompt.shape
    NS = a_sample.shape[0]
    N, D = NP + NS, w.shape[1]
    tm = _pick(NS, ROW_TILES)
    assert NP % tm == 0
    npb = NP // tm
    return pl.pallas_call(
        functools.partial(_proj_norm_body, alpha=alpha, npb=npb),
        grid=(N // tm,),
        in_specs=[pl.BlockSpec((tm, K), lambda i: (jnp.minimum(i, npb - 1), 0)),
                  pl.BlockSpec((tm, K), lambda i: (jnp.maximum(i - npb, 0), 0)),
                  pl.BlockSpec((K, D), lambda i: (0, 0)),
                  pl.BlockSpec((tm, D), lambda i: (i, 0)),
                  pl.BlockSpec((1, D), lambda i: (0, 0)),
                  pl.BlockSpec((1, D), lambda i: (0, 0))],
        out_specs=[pl.BlockSpec((tm, D), lambda i: (i, 0)), pl.BlockSpec((tm, D), lambda i: (i, 0))],
        out_shape=[jax.ShapeDtypeStruct((N, D), F32), jax.ShapeDtypeStruct((N, D), BF16)],
        compiler_params=_cparams("parallel"),
        name="proj_norm",
    )(a_prompt, a_sample, w, x, gain[None, :], bias[None, :])


def _static_loop(n, fn):
    for r in range(n):
        fn(r)


def _rows_loop(n, fn, unroll=8):
    def body(r, c):
        fn(r)
        return c
    lax.fori_loop(0, n, body, 0, unroll=unroll)


def _combine_norm_body(slot_ref, slot_next_ref, gate_ref, x_ref, gain_ref, bias_ref, ys_hbm, oa_ref, ob_ref,
                       ybuf, sem, *, alpha, nblk, split_blk):
    i = pl.program_id(0)
    cur = lax.rem(i, 2)
    tm, D = x_ref.shape

    def row_copy(idx_ref, k, r, s):
        return pltpu.make_async_copy(ys_hbm.at[pl.ds(idx_ref[k, r], 1)], ybuf.at[s, k, pl.ds(r, 1)], sem.at[s])

    def start_block(idx_ref, s):
        for k in range(2):
            _rows_loop(tm, lambda r: row_copy(idx_ref, k, r, s).start())

    @pl.when(i == 0)
    def _():
        start_block(slot_ref, 0)

    @pl.when(i + 1 < nblk)
    def _():
        start_block(slot_next_ref, 1 - cur)

    for k in range(2):
        _static_loop(tm, lambda r: row_copy(slot_ref, k, r, cur).wait())
    g = gate_ref[...]
    moe = g[:, 0:1] * ybuf[cur, 0].reshape(tm, D) + g[:, 1:2] * ybuf[cur, 1].reshape(tm, D)
    y = _layer_norm(alpha * x_ref[...] + moe, gain_ref[...], bias_ref[...])
    if split_blk is None:
        oa_ref[...] = y
        ob_ref[...] = y.astype(BF16)
    else:
        @pl.when(i < split_blk)
        def _():
            oa_ref[...] = y

        @pl.when(i >= split_blk)
        def _():
            ob_ref[...] = y


def _combine_norm(x, ys, slot, gate_col, gain, bias, alpha, split_rows=None):
    N, D = x.shape
    tm = _pick(N if split_rows is None else math.gcd(split_rows, N - split_rows), (256, 128))
    nblk = N // tm
    idx_spec = lambda off: pl.BlockSpec((2, tm), lambda i: (0, jnp.minimum(i + off, nblk - 1)),
                                        memory_space=pltpu.SMEM)
    if split_rows is None:
        split_blk = None
        out_specs = [pl.BlockSpec((tm, D), lambda i: (i, 0)), pl.BlockSpec((tm, D), lambda i: (i, 0))]
        out_shape = [jax.ShapeDtypeStruct((N, D), F32), jax.ShapeDtypeStruct((N, D), BF16)]
    else:
        split_blk = split_rows // tm
        out_specs = [pl.BlockSpec((tm, D), lambda i: (jnp.minimum(i, split_blk - 1), 0)),
                     pl.BlockSpec((tm, D), lambda i: (jnp.maximum(i - split_blk, 0), 0))]
        out_shape = [jax.ShapeDtypeStruct((split_rows, D), F32), jax.ShapeDtypeStruct((N - split_rows, D), F32)]
    return pl.pallas_call(
        functools.partial(_combine_norm_body, alpha=alpha, nblk=nblk, split_blk=split_blk),
        grid=(nblk,),
        in_specs=[idx_spec(0), idx_spec(1),
                  pl.BlockSpec((tm, gate_col.shape[1]), lambda i: (i, 0)),
                  pl.BlockSpec((tm, D), lambda i: (i, 0)),
                  pl.BlockSpec((1, D), lambda i: (0, 0)),
                  pl.BlockSpec((1, D), lambda i: (0, 0)),
                  pl.BlockSpec(memory_space=pl.ANY)],
        out_specs=out_specs,
        out_shape=out_shape,
        scratch_shapes=[pltpu.VMEM((2, 2, tm) + ys.shape[1:], F32), pltpu.SemaphoreType.DMA((2,))],
        compiler_params=_cparams("arbitrary"),
        name="combine_norm",
    )(slot, slot, gate_col, x, gain[None, :], bias[None, :], ys)


def _router_body(x_ref, wt_ref, b_ref, eid_ref, gate_ref, *, E):
    logits = _dot3(wt_ref[...], x_ref[...], _NT)
    ex = jnp.exp(logits - jnp.max(logits, axis=0, keepdims=True))
    probs = ex / jnp.sum(ex, axis=0, keepdims=True)
    sel = probs + b_ref[...]
    per = E // N_GROUPS
    s = [sel[e:e + 1, :] for e in range(E)]
    p = [probs[e:e + 1, :] for e in range(E)]

    def top2_sum(vals):
        best = None
        for a in range(len(vals)):
            for b in range(a + 1, len(vals)):
                pair = vals[a] + vals[b]
                best = pair if best is None else jnp.maximum(best, pair)
        return best

    g_best = top2_sum(s[:per])
    g_idx = jnp.zeros_like(g_best, dtype=jnp.int32)
    for gi in range(1, N_GROUPS):
        score = top2_sum(s[gi * per:(gi + 1) * per])
        better = score > g_best
        g_idx = jnp.where(better, gi, g_idx)
        g_best = jnp.where(better, score, g_best)

    def in_group(vals, j):
        out = vals[j]
        for gi in range(1, N_GROUPS):
            out = jnp.where(g_idx == gi, vals[gi * per + j], out)
        return out

    sg = [in_group(s, j) for j in range(per)]
    pg = [in_group(p, j) for j in range(per)]
    i1 = jnp.zeros_like(g_idx)
    b1 = sg[0]
    for j in range(1, per):
        better = sg[j] > b1
        i1 = jnp.where(better, j, i1)
        b1 = jnp.where(better, sg[j], b1)
    i2 = jnp.zeros_like(g_idx)
    b2 = jnp.full_like(b1, -jnp.inf)
    for j in range(per):
        better = (i1 != j) & (sg[j] > b2)
        i2 = jnp.where(better, j, i2)
        b2 = jnp.where(better, sg[j], b2)
    p1 = pg[0]
    p2 = pg[0]
    for j in range(1, per):
        p1 = jnp.where(i1 == j, pg[j], p1)
        p2 = jnp.where(i2 == j, pg[j], p2)
    tot = p1 + p2
    eid_ref[...] = jnp.concatenate([g_idx * per + i1, g_idx * per + i2], axis=0)
    gates = jnp.concatenate([p1 / tot, p2 / tot, jnp.zeros((GATE_COLS - 2, p1.shape[1]), F32)], axis=0)
    gate_ref[...] = gates.T


def _router(x, w_router, b_router):
    N, D = x.shape
    E = w_router.shape[1]
    assert E % N_GROUPS == 0 and E // N_GROUPS >= 2
    tm = _pick(N, ROW_TILES)
    return pl.pallas_call(
        functools.partial(_router_body, E=E),
        grid=(N // tm,),
        in_specs=[pl.BlockSpec((tm, D), lambda i: (i, 0)),
                  pl.BlockSpec((E, D), lambda i: (0, 0)),
                  pl.BlockSpec((E, 1), lambda i: (0, 0))],
        out_specs=[pl.BlockSpec((2, tm), lambda i: (0, i)), pl.BlockSpec((tm, GATE_COLS), lambda i: (i, 0))],
        out_shape=[jax.ShapeDtypeStruct((2, N), jnp.int32), jax.ShapeDtypeStruct((N, GATE_COLS), F32)],
        compiler_params=_cparams("parallel"),
        name="router",
    )(x, w_router.T, b_router.astype(F32)[:, None])


def _plan_body(e_ref, slot_ref, cnt_ref, *, E, TM):
    e = e_ref[...]
    R = e.shape[0]
    upper = (lax.broadcasted_iota(jnp.int32, (LANES, LANES), 0)
             <= lax.broadcasted_iota(jnp.int32, (LANES, LANES), 1)).astype(BF16)
    below = (lax.broadcasted_iota(jnp.int32, (R, R), 1) < lax.broadcasted_iota(jnp.int32, (R, R), 0)).astype(BF16)
    lane = lax.broadcasted_iota(jnp.int32, (1, LANES), 1)
    slot = jnp.zeros(e.shape, F32)
    counts = jnp.zeros((1, LANES), F32)
    start = jnp.zeros((1, 1), F32)
    for ex in range(E):
        hit = e == ex
        within = _dot(hit.astype(BF16), upper)
        rows_before = _dot(below, within.astype(BF16))[:, LANES - 1:LANES]
        rank = within - 1.0 + rows_before
        count = rows_before[R - 1:R, :] + within[R - 1:R, LANES - 1:LANES]
        slot = jnp.where(hit, start + rank, slot)
        counts = jnp.where(lane == ex, count, counts)
        start = start + jnp.floor((count + (TM - 1)) * (1.0 / TM)) * TM
    slot_ref[...] = slot.astype(jnp.int32)
    cnt_ref[...] = jnp.broadcast_to(counts, cnt_ref.shape).astype(jnp.int32)


def _plan(eid, E, TM):
    N = eid.shape[1]
    A = 2 * N
    assert A % LANES == 0 and (TM & (TM - 1)) == 0
    R = A // LANES
    Rp = -(-R // LANES) * LANES
    e2d = jnp.pad(eid.reshape(R, LANES), ((0, Rp - R), (0, 0)), constant_values=-1)
    slot2d, cnt = pl.pallas_call(
        functools.partial(_plan_body, E=E, TM=TM),
        out_shape=[jax.ShapeDtypeStruct((Rp, LANES), jnp.int32), jax.ShapeDtypeStruct((8, LANES), jnp.int32)],
        compiler_params=pltpu.CompilerParams(vmem_limit_bytes=VMEM_LIMIT_BYTES),
        name="route_plan",
    )(e2d)
    return slot2d[:R].reshape(2, N), cnt[0, :E]


def _group_table(counts, TM, n_blocks):
    E = counts.shape[0]
    GB = MOE_GROUP_BLOCKS
    nblk = (counts + TM - 1) // TM
    blk_end = jnp.cumsum(nblk)
    blk_start = blk_end - nblk
    ngrp = (nblk + GB - 1) // GB
    grp_end = jnp.cumsum(ngrp)
    n_groups_max = E + n_blocks // GB
    g = jnp.arange(n_groups_max, dtype=jnp.int32)
    g_expert = jnp.minimum(jnp.sum((grp_end[None, :] <= g[:, None]).astype(jnp.int32), axis=1), E - 1)
    j = g - (grp_end - ngrp)[g_expert]
    g_start = blk_start[g_expert] + j * GB
    g_nblk = jnp.clip(nblk[g_expert] - j * GB, 0, GB)
    live = g < grp_end[-1]
    g_nblk = jnp.where(live, g_nblk, 0)
    meta = jnp.stack([grp_end[-1], blk_end[-1]]).astype(jnp.int32)
    pad_start = blk_start * TM + counts
    pad_count = nblk * TM - counts
    return (g_expert.astype(jnp.int32), g_start.astype(jnp.int32), g_nblk.astype(jnp.int32), meta,
            pad_start.astype(jnp.int32), pad_count.astype(jnp.int32))


def _pack_halves(x):
    half = x.shape[1] // 2
    bits = lax.bitcast_convert_type(x.astype(BF16).astype(F32), jnp.uint32)
    return (bits[:, :half] >> 16) | bits[:, half:]


def _unpack_halves(p):
    lo = lax.bitcast_convert_type(p << 16, F32).astype(BF16)
    hi = lax.bitcast_convert_type(p & jnp.uint32(0xFFFF0000), F32).astype(BF16)
    return lo, hi


def _dispatch_body(pad_start_ref, pad_count_ref, meta_ref, slot_ref, x_ref, xs_hbm, pbuf, zbuf, sem, zsem,
                   *, nblk, E, TM, n_blocks):
    i = pl.program_id(0)
    cur = lax.rem(i, 2)
    tm = x_ref.shape[0]

    def row_copy(k, r, s):
        return pltpu.make_async_copy(pbuf.at[s, pl.ds(r, 1)], xs_hbm.at[pl.ds(slot_ref[k, r], 1)], sem.at[s])

    def wait_block(s):
        for k in range(2):
            _static_loop(tm, lambda r: row_copy(k, r, s).wait())

    def zero_copy(row):
        return pltpu.make_async_copy(zbuf.at[pl.ds(0, 1)], xs_hbm.at[pl.ds(row, 1)], zsem.at[0])

    @pl.when(i == 0)
    def _():
        zbuf[...] = jnp.zeros_like(zbuf)
        for ex in range(E):
            _rows_loop(pad_count_ref[ex], lambda j: zero_copy(pad_start_ref[ex] + j).start(), unroll=1)
        tail0 = meta_ref[1] * TM
        ntail = n_blocks * TM - tail0
        _rows_loop(ntail, lambda j: zero_copy(tail0 + j).start(), unroll=1)
        for ex in range(E):
            _rows_loop(pad_count_ref[ex], lambda j: zero_copy(pad_start_ref[ex] + j).wait(), unroll=1)
        _rows_loop(ntail, lambda j: zero_copy(tail0 + j).wait(), unroll=1)

    @pl.when(i >= 2)
    def _():
        wait_block(cur)

    pbuf[cur] = _pack_halves(x_ref[...]).reshape(pbuf.shape[1:])
    for k in range(2):
        _rows_loop(tm, lambda r: row_copy(k, r, cur).start())

    @pl.when(i == nblk - 1)
    def _():
        wait_block(cur)
        if nblk >= 2:
            wait_block(1 - cur)


def _dispatch(x, slot, pad_start, pad_count, meta, TM, n_blocks):
    N, D = x.shape
    E = pad_start.shape[0]
    tm = _pick(N, (256, 128))
    nblk = N // tm
    row_tile = (D // 2 // LANES, LANES)
    grid_spec = pltpu.PrefetchScalarGridSpec(
        num_scalar_prefetch=3,
        grid=(nblk,),
        in_specs=[pl.BlockSpec((2, tm), lambda i, *_: (0, i), memory_space=pltpu.SMEM),
                  pl.BlockSpec((tm, D), lambda i, *_: (i, 0))],
        out_specs=pl.BlockSpec(memory_space=pl.ANY),
        scratch_shapes=[pltpu.VMEM((2, tm) + row_tile, jnp.uint32), pltpu.VMEM((1,) + row_tile, jnp.uint32),
                        pltpu.SemaphoreType.DMA((2,)), pltpu.SemaphoreType.DMA((1,))],
    )
    return pl.pallas_call(
        functools.partial(_dispatch_body, nblk=nblk, E=E, TM=TM, n_blocks=n_blocks),
        grid_spec=grid_spec,
        out_shape=jax.ShapeDtypeStruct((n_blocks * TM,) + row_tile, jnp.uint32),
        compiler_params=_cparams("arbitrary"),
        name="moe_dispatch",
    )(pad_start, pad_count, meta, slot, x)


def _experts_body(ge_ref, gs_ref, gn_ref, meta_ref, xs_hbm, wg_hbm, wu_hbm, wd_hbm, ys_hbm,
                  xraw, xg, acc, stage_g, stage_u, stage_d, wgu, wdb, obuf, xsem, wsem, osem,
                  *, layer, chunks, TM, n_blocks):
    g = pl.program_id(0)
    n_groups = meta_ref[0]
    half = xg.shape[2]
    n_chunks = len(chunks)

    def x_copy(b0, rb):
        return pltpu.make_async_copy(xs_hbm.at[pl.ds((b0 + rb) * TM, TM)], xraw.at[rb], xsem.at[0])

    def w_copies(ex, c, s):
        f0, fw = chunks[c]
        return (pltpu.make_async_copy(wg_hbm.at[layer, ex, :, pl.ds(f0, fw)], stage_g.at[s, :, pl.ds(0, fw)],
                                      wsem.at[s, 0]),
                pltpu.make_async_copy(wu_hbm.at[layer, ex, :, pl.ds(f0, fw)], stage_u.at[s, :, pl.ds(0, fw)],
                                      wsem.at[s, 1]),
                pltpu.make_async_copy(wd_hbm.at[layer, ex, pl.ds(f0, fw), :], stage_d.at[s, pl.ds(0, fw), :],
                                      wsem.at[s, 2]))

    def out_copy(b0, rb, s):
        return pltpu.make_async_copy(obuf.at[s], ys_hbm.at[pl.ds((b0 + rb) * TM, TM)], osem.at[s])

    @pl.when(g < n_groups)
    def _():
        ex = ge_ref[g]
        b0 = gs_ref[g]
        nb = gn_ref[g]
        _rows_loop(nb, lambda rb: x_copy(b0, rb).start(), unroll=1)

        @pl.when(g == 0)
        def _():
            for cp in w_copies(ex, 0, 0):
                cp.start()

        def untile(rb):
            x_copy(b0, rb).wait()
            xg[rb] = xraw[rb].reshape(TM, half)

        _rows_loop(nb, untile, unroll=1)

        for c, (f0, fw) in enumerate(chunks):
            s = c % 2
            for cp in w_copies(ex, c, s):
                cp.wait()
            if c + 1 < n_chunks:
                for cp in w_copies(ex, c + 1, 1 - s):
                    cp.start()
            else:
                @pl.when(g + 1 < n_groups)
                def _():
                    for cp in w_copies(ge_ref[g + 1], 0, 1 - s):
                        cp.start()
            wgu[:, 0:fw] = stage_g[s, :, 0:fw].astype(BF16)
            wgu[:, fw:2 * fw] = stage_u[s, :, 0:fw].astype(BF16)
            wdb[0:fw, :] = stage_d[s, 0:fw, :].astype(BF16)

            def block(rb, carry, c=c, fw=fw):
                last = c + 1 == n_chunks
                os_ = lax.rem(rb, 2)
                if last:
                    @pl.when(rb >= 2)
                    def _():
                        out_copy(b0, rb, os_).wait()
                xl, xh = _unpack_halves(xg[rb])
                gu = _dot(xl, wgu[0:half, 0:2 * fw]) + _dot(xh, wgu[half:2 * half, 0:2 * fw])
                hidden = jax.nn.silu(gu[:, 0:fw]) * gu[:, fw:2 * fw]
                part = _dot(hidden.astype(BF16), wdb[0:fw, :])
                if c == 0:
                    acc[rb] = part
                elif not last:
                    acc[rb] = acc[rb] + part
                else:
                    obuf[os_] = (acc[rb] + part).reshape(obuf.shape[1:])
                    out_copy(b0, rb, os_).start()
                return carry

            lax.fori_loop(0, nb, block, 0)

        out_copy(b0, 0, lax.rem(nb - 1, 2)).wait()

        @pl.when(nb >= 2)
        def _():
            out_copy(b0, 0, lax.rem(nb, 2)).wait()

    @pl.when(g == pl.num_programs(0) - 1)
    def _():
        used = meta_ref[1]
        obuf[0] = jnp.zeros(obuf.shape[1:], F32)
        _rows_loop(n_blocks - used, lambda j: out_copy(used, j, 0).start(), unroll=1)
        _rows_loop(n_blocks - used, lambda j: out_copy(used, j, 0).wait(), unroll=1)


def _experts(xs, g_expert, g_start, g_nblk, meta, w_gate, w_up, w_down, layer, TM, n_blocks):
    _, E, D, F = w_gate.shape
    GB = MOE_GROUP_BLOCKS
    FC = MOE_FC
    chunks = tuple((f0, min(FC, F - f0)) for f0 in range(0, F, FC))
    assert len(chunks) % 2 == 0 and all(fw % LANES == 0 for _, fw in chunks)
    out_tile = (D // LANES, LANES)
    grid_spec = pltpu.PrefetchScalarGridSpec(
        num_scalar_prefetch=4,
        grid=(g_expert.shape[0],),
        in_specs=[pl.BlockSpec(memory_space=pl.ANY)] * 4,
        out_specs=pl.BlockSpec(memory_space=pl.ANY),
        scratch_shapes=[pltpu.VMEM((GB, TM) + xs.shape[1:], jnp.uint32),
                        pltpu.VMEM((GB, TM, D // 2), jnp.uint32), pltpu.VMEM((GB, TM, D), F32),
                        pltpu.VMEM((2, D, FC), F32), pltpu.VMEM((2, D, FC), F32), pltpu.VMEM((2, FC, D), F32),
                        pltpu.VMEM((D, 2 * FC), BF16), pltpu.VMEM((FC, D), BF16),
                        pltpu.VMEM((2, TM) + out_tile, F32),
                        pltpu.SemaphoreType.DMA((1,)), pltpu.SemaphoreType.DMA((2, 3)),
                        pltpu.SemaphoreType.DMA((2,))],
    )
    return pl.pallas_call(
        functools.partial(_experts_body, layer=layer, chunks=chunks, TM=TM, n_blocks=n_blocks),
        grid_spec=grid_spec,
        out_shape=jax.ShapeDtypeStruct((n_blocks * TM,) + out_tile, F32),
        compiler_params=_cparams("arbitrary"),
        name="moe_experts",
    )(g_expert, g_start, g_nblk, meta, xs, w_gate, w_up, w_down)


def _moe_layer(x, w_router, b_router, w_gate, w_up, w_down, layer, gain, bias, alpha, split_rows=None):
    N = x.shape[0]
    E = w_router.shape[1]
    TM = MOE_TM
    n_blocks = -(-(2 * N + E * (TM - 1)) // TM)
    eid, gate_col = _router(x, w_router, b_router)
    slot, counts = _plan(eid, E, TM)
    g_expert, g_start, g_nblk, meta, pad_start, pad_count = _group_table(counts, TM, n_blocks)
    xs = _dispatch(x, slot, pad_start, pad_count, meta, TM, n_blocks)
    ys = _experts(xs, g_expert, g_start, g_nblk, meta, w_gate, w_up, w_down, layer, TM, n_blocks)
    return _combine_norm(x, ys, slot, gate_col, gain, bias, alpha, split_rows)


def _pair_attention(q_pairs, kc, vc, valid, sinks_lo, sinks_hi):
    P = len(q_pairs)
    R, lanes = q_pairs[0].shape
    lo = lax.broadcasted_iota(jnp.int32, (R, lanes), 1) < lanes // 2
    zero = jnp.zeros((R, lanes), BF16)
    qs = jnp.concatenate([jnp.where(lo, qp, zero) for qp in q_pairs]
                         + [jnp.where(lo, zero, qp) for qp in q_pairs], axis=0)
    sink_col = jnp.concatenate([jnp.full((R, 1), s, F32) for s in list(sinks_lo) + list(sinks_hi)], axis=0)
    s = jnp.where(valid, _dg(qs, kc, _NT), -jnp.inf)
    mx = jnp.maximum(jnp.max(s, axis=1, keepdims=True), sink_col)
    e = jnp.exp(s - mx)
    den = jnp.sum(e, axis=1, keepdims=True) + jnp.exp(sink_col - mx)
    o = _dot(e.astype(BF16), vc) * (1.0 / den)
    return [jnp.where(lo, o[p * R:(p + 1) * R], o[(P + p) * R:(P + p + 1) * R]) for p in range(P)]


def _swa_prompt_body(sink_ref, q_ref, kp_ref, kc_ref, vp_ref, vc_ref, o_ref, *, KVH, G):
    W = q_ref.shape[0]
    P = G // 2
    first = pl.program_id(1) == 0
    r = lax.broadcasted_iota(jnp.int32, (G * W, 2 * W), 0)
    kj = lax.broadcasted_iota(jnp.int32, (G * W, 2 * W), 1)
    diff = r % W + W - kj
    valid = (diff >= 0) & (diff < W) & (jnp.logical_not(first) | (kj >= W))
    for g in range(KVH):
        ks = slice(g * LANES, (g + 1) * LANES)
        kc = jnp.concatenate([kp_ref[:, ks], kc_ref[:, ks]], axis=0)
        vc = jnp.concatenate([vp_ref[:, ks], vc_ref[:, ks]], axis=0)
        cols = [slice((g * P + p) * LANES, (g * P + p + 1) * LANES) for p in range(P)]
        outs = _pair_attention([q_ref[:, c] for c in cols], kc, vc, valid,
                               [sink_ref[g * G + 2 * p] for p in range(P)],
                               [sink_ref[g * G + 2 * p + 1] for p in range(P)])
        for c, o in zip(cols, outs):
            o_ref[:, c] = o.astype(BF16)


def _swa_prompt(q, kdup, vdup, sinks, B, T, KVH):
    Hq = sinks.shape[0]
    W = WINDOW
    assert T % W == 0 and kdup.shape[1] == KVH * LANES and q.shape[1] == Hq // 2 * LANES
    nb = T // W
    cur = lambda b, n: (b * nb + n, 0)
    prev = lambda b, n: (b * nb + jnp.maximum(n - 1, 0), 0)
    return pl.pallas_call(
        functools.partial(_swa_prompt_body, KVH=KVH, G=Hq // KVH),
        grid=(B, nb),
        in_specs=[pl.BlockSpec(memory_space=pltpu.SMEM),
                  pl.BlockSpec((W, q.shape[1]), cur),
                  pl.BlockSpec((W, KVH * LANES), prev), pl.BlockSpec((W, KVH * LANES), cur),
                  pl.BlockSpec((W, KVH * LANES), prev), pl.BlockSpec((W, KVH * LANES), cur)],
        out_specs=pl.BlockSpec((W, q.shape[1]), cur),
        out_shape=jax.ShapeDtypeStruct((B * T, q.shape[1]), BF16),
        compiler_params=_cparams("parallel", "parallel"),
        name="swa_prompt",
    )(sinks, q, kdup, kdup, vdup, vdup)


def _swa_sample_body(sink_ref, q_ref, k_ref, v_ref, o_ref, *, KVH, G, TS, WB):
    K = k_ref.shape[1]
    P = G // 2
    q = q_ref[...].astype(F32)
    r = lax.broadcasted_iota(jnp.int32, (G * TS, K), 0)
    kj = lax.broadcasted_iota(jnp.int32, (G * TS, K), 1)
    diff = WB + r % TS - kj
    valid = (diff >= 0) & (diff < WINDOW)
    for bb in range(SAMPLE_BS):
        rows = slice(bb * TS, (bb + 1) * TS)
        for g in range(KVH):
            ks = slice(g * LANES, (g + 1) * LANES)
            cols = [slice((g * P + p) * LANES, (g * P + p + 1) * LANES) for p in range(P)]
            outs = _pair_attention([q[rows, c].astype(BF16) for c in cols], k_ref[bb, :, ks], v_ref[bb, :, ks], valid,
                                   [sink_ref[g * G + 2 * p] for p in range(P)],
                                   [sink_ref[g * G + 2 * p + 1] for p in range(P)])
            for c, o in zip(cols, outs):
                o_ref[rows, c] = o.astype(BF16)


def _swa_sample(q, k_all, v_all, sinks, NP, TS, WB, KVH):
    Bs, K, _ = k_all.shape
    Hq = sinks.shape[0]
    BS = SAMPLE_BS
    R = BS * TS
    off = NP // R
    row = lambda i: (off + i, 0)
    return pl.pallas_call(
        functools.partial(_swa_sample_body, KVH=KVH, G=Hq // KVH, TS=TS, WB=WB),
        grid=(Bs // BS,),
        in_specs=[pl.BlockSpec(memory_space=pltpu.SMEM),
                  pl.BlockSpec((R, q.shape[1]), row),
                  pl.BlockSpec((BS, K, KVH * LANES), lambda i: (i, 0, 0)),
                  pl.BlockSpec((BS, K, KVH * LANES), lambda i: (i, 0, 0))],
        out_specs=pl.BlockSpec((R, q.shape[1]), lambda i: (i, 0)),
        out_shape=jax.ShapeDtypeStruct((Bs * TS, q.shape[1]), BF16),
        compiler_params=_cparams("parallel"),
        name="swa_sample",
    )(sinks, q, k_all, v_all)


def _rope_tables(pos, head_dim, lanes):
    half = head_dim // 2
    inv = ROPE_THETA ** (-jnp.arange(half, dtype=F32) / half)
    ang = pos.astype(F32)[:, None] * inv[None, :]
    reps = lanes // half
    return jnp.tile(jnp.cos(ang), (1, reps)), jnp.tile(jnp.sin(ang), (1, reps))


def kernel(x_prompt, x_sample, state_C, state_n, state_m, cache_k, cache_v, w_in_a, b_i_a, b_f_a, norm_a, w_out_a,
           w_kv, w_q_b, sinks_b, w_out_b, w_router, b_router, w_gate_e, w_up_e, w_down_e, ln_g, ln_b):
    B, T, D = x_prompt.shape
    Bs, TS, _ = x_sample.shape
    depth = ln_g.shape[0]
    assert depth == 2 and w_in_a.shape[0] == 1 and w_q_b.shape[0] == 1
    alpha = (2 * depth) ** 0.25
    NP, NS = B * T, Bs * TS
    H = b_i_a.shape[1]
    DK, DV = state_C.shape[3], state_C.shape[4]
    WB, KVH, HD = cache_k.shape[1], cache_k.shape[2], cache_k.shape[3]
    E = w_router.shape[1]

    x0 = jnp.concatenate([x_prompt.reshape(NP, D), x_sample.reshape(NS, D)], axis=0).astype(F32)
    x0b = x0.astype(BF16)

    w_in = w_in_a[0]
    nqk, nv = H * DK, H * DV
    w_qkv = w_in[:, :2 * nqk + nv].astype(BF16)
    w_og = w_in[:, 2 * nqk + nv:2 * nqk + nv + D].astype(BF16)
    w_gates = w_in[:, 2 * nqk + nv + D:].astype(F32)
    qkv_scale = jnp.concatenate([jnp.ones((nqk,), F32), jnp.full((nqk,), DK ** -0.5, F32), jnp.ones((nv,), F32)])
    qkv = _matmul(x0b, w_qkv, qkv_scale[None, :], BF16)
    og = _matmul(x0b, w_og, jnp.ones((1, D), F32), F32)
    g, gt = _mlstm_gates(x0, w_gates, b_i_a[0], b_f_a[0])
    norm_w = norm_a[0].astype(F32)[None, :]
    h_p, C_p, n_p, m_p = _mlstm_prompt(qkv, og, g, gt, norm_w, B, T, H, DK, DV)
    h_s, C_s, n_s, m_s = _mlstm_sample(qkv, og, g, gt, norm_w, state_C[0].astype(F32), state_n[0].astype(F32),
                                       state_m[0].astype(F32), NP, TS)
    x1, _ = _proj_norm(h_p, h_s, w_out_a[0].astype(BF16), x0, ln_g[0, 0], ln_b[0, 0], alpha)
    x2, x2b = _moe_layer(x1, w_router, b_router, w_gate_e, w_up_e, w_down_e, 0, ln_g[0, 1], ln_b[0, 1], alpha)

    pos = jnp.concatenate([jnp.tile(jnp.arange(T, dtype=jnp.int32), B),
                           jnp.tile(PAST_LEN + jnp.arange(TS, dtype=jnp.int32), Bs)])
    cos, sin = _rope_tables(pos, HD, 128)
    nkv = KVH * HD
    k_new = _matmul_rope(x2b, w_kv[:, :nkv].astype(BF16), cos, sin, HD, F32)
    v_new = _matmul(x2b, w_kv[:, nkv:].astype(BF16), jnp.ones((1, nkv), F32), F32)
    assert 2 * HD == LANES
    qb = _matmul_rope(x2b, w_q_b[0].astype(BF16), cos, sin, HD, BF16, scale=HD ** -0.5)

    def dup_heads(a):
        a4 = a.reshape(a.shape[:-1] + (KVH, HD)).astype(BF16)
        return jnp.concatenate([a4, a4], axis=-1).reshape(a.shape[:-1] + (KVH * LANES,))

    o_p = _swa_prompt(qb, dup_heads(k_new), dup_heads(v_new), sinks_b[0].astype(F32), B, T, KVH)
    k_s_new = k_new[NP:].reshape(Bs, TS, nkv)
    v_s_new = v_new[NP:].reshape(Bs, TS, nkv)
    k_cat = jnp.concatenate([cache_k.reshape(Bs, WB, nkv).astype(F32), k_s_new], axis=1)
    v_cat = jnp.concatenate([cache_v.reshape(Bs, WB, nkv).astype(F32), v_s_new], axis=1)
    kpad = -(-(WB + TS) // LANES) * LANES - (WB + TS)
    k_all = dup_heads(jnp.pad(k_cat, ((0, 0), (0, kpad), (0, 0))))
    v_all = dup_heads(jnp.pad(v_cat, ((0, 0), (0, kpad), (0, 0))))
    o_s = _swa_sample(qb, k_all, v_all, sinks_b[0].astype(F32), NP, TS, WB, KVH)
    x3, _ = _proj_norm(o_p, o_s, w_out_b[0].astype(BF16), x2, ln_g[1, 0], ln_b[1, 0], alpha)
    y_p, y_s = _moe_layer(x3, w_router, b_router, w_gate_e, w_up_e, w_down_e, 1, ln_g[1, 1], ln_b[1, 1], alpha,
                          split_rows=NP)
    y_prompt = y_p.reshape(B, T, D)
    y_sample = y_s.reshape(Bs, TS, D)
    k_p = k_new[:NP].reshape(B, T, KVH, HD)[:, -WB:]
    v_p = v_new[:NP].reshape(B, T, KVH, HD)[:, -WB:]
    k_s = k_cat[:, -WB:].reshape(Bs, WB, KVH, HD)
    v_s = v_cat[:, -WB:].reshape(Bs, WB, KVH, HD)
    return (y_prompt, y_sample, C_p[None], n_p[None], m_p.reshape(1, B, H), k_p, v_p,
            C_s[None], n_s[None], m_s[None], k_s, v_s)
```

```python
import functools
import math

import jax
import jax.numpy as jnp
from jax import lax
from jax.experimental import pallas as pl
from jax.experimental.pallas import tpu as pltpu

F32 = jnp.float32
BF16 = jnp.bfloat16

GATE_CAP = 15.0
LN_EPS = 1e-5
ROPE_THETA = 10000.0
WINDOW = 128
PAST_LEN = 8192
N_GROUPS = 4

VMEM_LIMIT_BYTES = 56 * 1024 * 1024
MLSTM_CHUNK = 256
MOE_TM = 256
MOE_GROUP_BLOCKS = 6
MOE_FC = 256
GATE_COLS = 8
LANES = 128
SAMPLE_BS = 2
ROW_TILES = (512, 256, 128)
MATMUL_ROW_TILES = (1024,) + ROW_TILES

_NT = (((1,), (1,)), ((), ()))
_TN = (((0,), (0,)), ((), ()))


def _pick(n, cands):
    for c in cands:
        if n % c == 0:
            return c
    raise ValueError(f"no tile in {cands} divides {n}")


def _cparams(*sem):
    return pltpu.CompilerParams(dimension_semantics=sem, vmem_limit_bytes=VMEM_LIMIT_BYTES)


def _dot(a, b):
    return jnp.dot(a, b, preferred_element_type=F32)


def _dg(a, b, dims):
    return lax.dot_general(a, b, dims, preferred_element_type=F32)


def _mm_scale_body(a_ref, b_ref, s_ref, o_ref):
    o_ref[...] = (_dot(a_ref[...], b_ref[...]) * s_ref[...]).astype(o_ref.dtype)


def _matmul(a, b, col_scale, out_dtype):
    M, K = a.shape
    N = b.shape[1]
    tm = _pick(M, MATMUL_ROW_TILES)
    tn = _pick(N, (1024, 512, 256))
    return pl.pallas_call(
        _mm_scale_body,
        grid=(M // tm, N // tn),
        in_specs=[pl.BlockSpec((tm, K), lambda i, j: (i, 0)),
                  pl.BlockSpec((K, tn), lambda i, j: (0, j)),
                  pl.BlockSpec((1, tn), lambda i, j: (0, j))],
        out_specs=pl.BlockSpec((tm, tn), lambda i, j: (i, j)),
        out_shape=jax.ShapeDtypeStruct((M, N), out_dtype),
        compiler_params=_cparams("parallel", "parallel"),
        name="matmul",
    )(a, b, col_scale)


def _mm_rope_body(a_ref, b_ref, cos_ref, sin_ref, o_ref, *, half, scale):
    acc = _dot(a_ref[...], b_ref[...])
    tn = acc.shape[1]
    reps = tn // cos_ref.shape[1]
    cos = jnp.concatenate([cos_ref[...]] * reps, axis=1)
    sin = jnp.concatenate([sin_ref[...]] * reps, axis=1)
    lane = lax.broadcasted_iota(jnp.int32, acc.shape, 1)
    partner = jnp.where(lane % (2 * half) < half, -pltpu.roll(acc, tn - half, 1), pltpu.roll(acc, half, 1))
    o_ref[...] = ((acc * cos + partner * sin) * scale).astype(o_ref.dtype)


def _matmul_rope(a, b, cos, sin, head_dim, out_dtype, scale=1.0):
    M, K = a.shape
    N = b.shape[1]
    tm = _pick(M, MATMUL_ROW_TILES)
    tn = _pick(N, (1024, 512, 256))
    assert math.frexp(scale)[0] == 0.5
    return pl.pallas_call(
        functools.partial(_mm_rope_body, half=head_dim // 2, scale=scale),
        grid=(M // tm, N // tn),
        in_specs=[pl.BlockSpec((tm, K), lambda i, j: (i, 0)),
                  pl.BlockSpec((K, tn), lambda i, j: (0, j)),
                  pl.BlockSpec((tm, cos.shape[1]), lambda i, j: (i, 0)),
                  pl.BlockSpec((tm, sin.shape[1]), lambda i, j: (i, 0))],
        out_specs=pl.BlockSpec((tm, tn), lambda i, j: (i, j)),
        out_shape=jax.ShapeDtypeStruct((M, N), out_dtype),
        compiler_params=_cparams("parallel", "parallel"),
        name="matmul_rope",
    )(a, b, cos, sin)


def _split_bf16(x):
    hi = x.astype(BF16)
    lo = (x - hi.astype(F32)).astype(BF16)
    return hi, lo


def _dot3(a, b, dims):
    ah, al = _split_bf16(a)
    bh, bl = _split_bf16(b)
    return _dg(ah, bh, dims) + _dg(al, bh, dims) + _dg(ah, bl, dims)


def _softcap(z):
    return GATE_CAP * jnp.tanh(z / GATE_CAP)


def _log_sigmoid(z):
    return jnp.minimum(z, 0.0) - jnp.log1p(jnp.exp(-jnp.abs(z)))


def _gates_body(x_ref, w_ref, wt_ref, brow_ref, bcol_ref, g_ref, gt_ref, *, H):
    x = x_ref[...]
    pre = _dot3(x, w_ref[...], (((1,), (0,)), ((), ()))) + brow_ref[...]
    pre_t = _dot3(wt_ref[...], x, _NT) + bcol_ref[...]
    z = _softcap(pre)
    zt = _softcap(pre_t)
    is_in = lax.broadcasted_iota(jnp.int32, z.shape, 1) < H
    is_in_t = lax.broadcasted_iota(jnp.int32, zt.shape, 0) < H
    g_ref[...] = jnp.where(is_in, z, _log_sigmoid(z))
    gt_ref[...] = jnp.where(is_in_t, zt, _log_sigmoid(zt))


def _mlstm_gates(x, w_gates, b_i, b_f):
    N, D = x.shape
    H = b_i.shape[0]
    tm = _pick(N, ROW_TILES)
    bias = jnp.concatenate([b_i, b_f]).astype(F32)
    return pl.pallas_call(
        functools.partial(_gates_body, H=H),
        grid=(N // tm,),
        in_specs=[pl.BlockSpec((tm, D), lambda i: (i, 0)),
                  pl.BlockSpec((D, 2 * H), lambda i: (0, 0)),
                  pl.BlockSpec((2 * H, D), lambda i: (0, 0)),
                  pl.BlockSpec((1, 2 * H), lambda i: (0, 0)),
                  pl.BlockSpec((2 * H, 1), lambda i: (0, 0))],
        out_specs=[pl.BlockSpec((tm, 2 * H), lambda i: (i, 0)),
                   pl.BlockSpec((2 * H, tm), lambda i: (0, i))],
        out_shape=[jax.ShapeDtypeStruct((N, 2 * H), F32), jax.ShapeDtypeStruct((2 * H, N), F32)],
        compiler_params=_cparams("parallel"),
        name="mlstm_gates",
    )(x, w_gates, w_gates.T, bias[None, :], bias[:, None])


def _mlstm_head(qh, kh, vh, ig_col, lf_col, ig_row, lf_row, C, n_row, m_prev):
    L = qh.shape[0]
    t_idx = lax.broadcasted_iota(jnp.int32, (L, L), 0)
    s_idx = lax.broadcasted_iota(jnp.int32, (L, L), 1)
    causal = s_idx <= t_idx

    def mxu(x):
        return x.astype(BF16)

    b_col = jnp.sum(jnp.where(causal, lf_row, 0.0), axis=1, keepdims=True)
    b_row = jnp.sum(jnp.where(t_idx <= s_idx, lf_col, 0.0), axis=0, keepdims=True)
    dmat = jnp.where(causal, b_col - b_row + ig_row, -jnp.inf)
    a_col = b_col + m_prev
    m_t = jnp.maximum(a_col, jnp.max(dmat, axis=1, keepdims=True))
    wmat = jnp.exp(dmat - m_t) * _dg(qh, kh, _NT)
    inter = jnp.exp(a_col - m_t)
    qf = qh.astype(F32)
    kf = kh.astype(F32)
    num = inter * _dot(qh, mxu(C)) + _dot(mxu(wmat), vh)
    den = inter * jnp.sum(qf * n_row, axis=1, keepdims=True) + jnp.sum(wmat, axis=1, keepdims=True)
    hh = num * (1.0 / jnp.maximum(jnp.abs(den), jnp.exp(-m_t)))
    m_new = m_t[L - 1:L, :]
    b_last = b_col[L - 1:L, :]
    decay = jnp.exp(b_last - b_col + ig_col - m_new)
    carry = jnp.exp(b_last + m_prev - m_new)
    kd = kf * decay
    C_new = carry * C + _dg(mxu(kd), vh, _TN)
    n_new = carry * n_row + jnp.sum(kd, axis=0, keepdims=True)
    return hh, C_new, n_new, m_new


def _head_out(hh, nw, og):
    hn = hh * lax.rsqrt(jnp.mean(hh * hh, axis=1, keepdims=True) + LN_EPS)
    return hn * nw * jax.nn.sigmoid(og)


def _mlstm_prompt_body(q_ref, k_ref, v_ref, og_ref, g_ref, gt_ref, nw_ref,
                       h_ref, c_ref, n_ref, m_ref, *, H, DK, DV):
    @pl.when(pl.program_id(1) == 0)
    def _():
        c_ref[...] = jnp.zeros_like(c_ref)
        n_ref[...] = jnp.zeros_like(n_ref)
        m_ref[...] = jnp.zeros_like(m_ref)

    g = g_ref[...]
    gt = gt_ref[...]
    for h in range(H):
        kq = slice(h * DK, (h + 1) * DK)
        vv = slice(h * DV, (h + 1) * DV)
        hh, C_new, n_new, m_new = _mlstm_head(
            q_ref[:, kq], k_ref[:, kq], v_ref[:, vv],
            g[:, h:h + 1], g[:, H + h:H + h + 1], gt[h:h + 1, :], gt[H + h:H + h + 1, :],
            c_ref[0, h], n_ref[0, h:h + 1, :], m_ref[0, :, h:h + 1])
        c_ref[0, h] = C_new
        n_ref[0, h:h + 1, :] = n_new
        m_ref[0, :, h:h + 1] = m_new
        h_ref[:, vv] = _head_out(hh, nw_ref[:, vv], og_ref[:, vv]).astype(BF16)


def _mlstm_prompt(qkv, og, g, gt, norm_w, B, T, H, DK, DV):
    L = MLSTM_CHUNK
    assert T % L == 0 and (2 * H * DK) % (H * DV) == 0
    nc = T // L
    v_blk = 2 * DK // DV
    row = lambda b, c: (b * nc + c, 0)
    return pl.pallas_call(
        functools.partial(_mlstm_prompt_body, H=H, DK=DK, DV=DV),
        grid=(B, nc),
        in_specs=[pl.BlockSpec((L, H * DK), row), pl.BlockSpec((L, H * DK), lambda b, c: (b * nc + c, 1)),
                  pl.BlockSpec((L, H * DV), lambda b, c: (b * nc + c, v_blk)), pl.BlockSpec((L, H * DV), row),
                  pl.BlockSpec((L, 2 * H), row),
                  pl.BlockSpec((2 * H, L), lambda b, c: (0, b * nc + c)),
                  pl.BlockSpec((1, H * DV), lambda b, c: (0, 0))],
        out_specs=[pl.BlockSpec((L, H * DV), row),
                   pl.BlockSpec((1, H, DK, DV), lambda b, c: (b, 0, 0, 0)),
                   pl.BlockSpec((1, H, DK), lambda b, c: (b, 0, 0)),
                   pl.BlockSpec((1, 1, H), lambda b, c: (b, 0, 0))],
        out_shape=[jax.ShapeDtypeStruct((B * T, H * DV), BF16),
                   jax.ShapeDtypeStruct((B, H, DK, DV), F32),
                   jax.ShapeDtypeStruct((B, H, DK), F32),
                   jax.ShapeDtypeStruct((B, 1, H), F32)],
        compiler_params=_cparams("parallel", "arbitrary"),
        name="mlstm_prompt",
    )(qkv, qkv, qkv, og, g, gt, norm_w)


def _mlstm_sample_body(q_ref, k_ref, v_ref, og_ref, g_ref, gt_ref, nw_ref, c0_ref, n0_ref, m0_ref,
                       h_ref, c_ref, n_ref, m_ref, *, H, DK, DV, TS):
    q = q_ref[...].astype(F32)
    k = k_ref[...].astype(F32)
    v = v_ref[...].astype(F32)
    og = og_ref[...]
    g = g_ref[...]
    gt = gt_ref[0]
    HT = H * TS
    r_idx = lax.broadcasted_iota(jnp.int32, (HT, HT), 0)
    c_idx = lax.broadcasted_iota(jnp.int32, (HT, HT), 1)
    same_head = (r_idx // TS) == (c_idx // TS)
    causal = same_head & (c_idx <= r_idx)
    upto = same_head & (r_idx <= c_idx)

    def stack(parts):
        return jnp.concatenate(parts, axis=0)

    def mxu(x):
        return x.astype(BF16)

    for bb in range(SAMPLE_BS):
        rows = slice(bb * TS, (bb + 1) * TS)
        hrows = [slice(h * TS, (h + 1) * TS) for h in range(H)]
        kq = [slice(h * DK, (h + 1) * DK) for h in range(H)]
        vv = [slice(h * DV, (h + 1) * DV) for h in range(H)]
        qf = stack([q[rows, kq[h]] for h in range(H)])
        kf = stack([k[rows, kq[h]] for h in range(H)])
        v_h = [v[rows, vv[h]].astype(BF16) for h in range(H)]
        ig_col = stack([g[rows, h:h + 1] for h in range(H)])
        lf_col = stack([g[rows, H + h:H + h + 1] for h in range(H)])
        ig_row = jnp.concatenate([gt[h:h + 1, rows] for h in range(H)], axis=1)
        lf_row = jnp.concatenate([gt[H + h:H + h + 1, rows] for h in range(H)], axis=1)
        m_prev = [m0_ref[0, bb:bb + 1, h:h + 1] for h in range(H)]
        m_prev_col = stack([jnp.broadcast_to(m, (TS, 1)) for m in m_prev])

        b_col = jnp.sum(jnp.where(causal, lf_row, 0.0), axis=1, keepdims=True)
        b_row = jnp.sum(jnp.where(upto, lf_col, 0.0), axis=0, keepdims=True)
        dmat = jnp.where(causal, b_col - b_row + ig_row, -jnp.inf)
        a_col = b_col + m_prev_col
        m_t = jnp.maximum(a_col, jnp.max(dmat, axis=1, keepdims=True))
        wmat = jnp.exp(dmat - m_t) * _dg(mxu(qf), mxu(kf), _NT)
        inter = jnp.exp(a_col - m_t)
        C0 = [c0_ref[bb, h] for h in range(H)]
        n0 = [n0_ref[bb, h:h + 1, :] for h in range(H)]
        q_c = stack([_dot(mxu(qf[hrows[h]]), mxu(C0[h])) for h in range(H)])
        n_rows = stack([jnp.broadcast_to(n, (TS, DK)) for n in n0])
        num = inter * q_c + _dot(mxu(wmat), stack(v_h))
        den = inter * jnp.sum(qf * n_rows, axis=1, keepdims=True) + jnp.sum(wmat, axis=1, keepdims=True)
        hh = num * (1.0 / jnp.maximum(jnp.abs(den), jnp.exp(-m_t)))

        m_new = [m_t[(h + 1) * TS - 1:(h + 1) * TS, :] for h in range(H)]
        b_last = [b_col[(h + 1) * TS - 1:(h + 1) * TS, :] for h in range(H)]
        m_new_col = stack([jnp.broadcast_to(m, (TS, 1)) for m in m_new])
        b_last_col = stack([jnp.broadcast_to(b, (TS, 1)) for b in b_last])
        kd = kf * jnp.exp(b_last_col - b_col + ig_col - m_new_col)

        og_s = stack([og[rows, vv[h]] for h in range(H)])
        nw_s = stack([jnp.broadcast_to(nw_ref[:, vv[h]], (TS, DV)) for h in range(H)])
        out = _head_out(hh, nw_s, og_s)
        for h in range(H):
            carry = jnp.exp(b_last[h] + m_prev[h] - m_new[h])
            c_ref[bb, h] = carry * C0[h] + _dg(mxu(kd[hrows[h]]), v_h[h], _TN)
            n_ref[bb, h:h + 1, :] = carry * n0[h] + jnp.sum(kd[hrows[h]], axis=0, keepdims=True)
            m_ref[0, bb:bb + 1, h:h + 1] = m_new[h]
            h_ref[rows, vv[h]] = out[hrows[h]].astype(BF16)


def _mlstm_sample(qkv, og, g, gt, norm_w, C0, n0, m0, NP, TS):
    Bs, H, DK, DV = C0.shape
    BS = SAMPLE_BS
    assert Bs % BS == 0 and NP % (BS * TS) == 0
    R = BS * TS
    off = NP // R
    row = lambda i: (off + i, 0)
    gt_s = gt[:, NP:].reshape(2 * H, Bs // BS, R).transpose(1, 0, 2)
    m0_s = m0.reshape(Bs // BS, BS, H)
    outs = pl.pallas_call(
        functools.partial(_mlstm_sample_body, H=H, DK=DK, DV=DV, TS=TS),
        grid=(Bs // BS,),
        in_specs=[pl.BlockSpec((R, H * DK), row), pl.BlockSpec((R, H * DK), lambda i: (off + i, 1)),
                  pl.BlockSpec((R, H * DV), lambda i: (off + i, 2 * DK // DV)), pl.BlockSpec((R, H * DV), row),
                  pl.BlockSpec((R, 2 * H), row),
                  pl.BlockSpec((1, 2 * H, R), lambda i: (i, 0, 0)),
                  pl.BlockSpec((1, H * DV), lambda i: (0, 0)),
                  pl.BlockSpec((BS, H, DK, DV), lambda i: (i, 0, 0, 0)),
                  pl.BlockSpec((BS, H, DK), lambda i: (i, 0, 0)),
                  pl.BlockSpec((1, BS, H), lambda i: (i, 0, 0))],
        out_specs=[pl.BlockSpec((R, H * DV), lambda i: (i, 0)),
                   pl.BlockSpec((BS, H, DK, DV), lambda i: (i, 0, 0, 0)),
                   pl.BlockSpec((BS, H, DK), lambda i: (i, 0, 0)),
                   pl.BlockSpec((1, BS, H), lambda i: (i, 0, 0))],
        out_shape=[jax.ShapeDtypeStruct((Bs * TS, H * DV), BF16),
                   jax.ShapeDtypeStruct((Bs, H, DK, DV), F32),
                   jax.ShapeDtypeStruct((Bs, H, DK), F32),
                   jax.ShapeDtypeStruct((Bs // BS, BS, H), F32)],
        compiler_params=_cparams("parallel"),
        name="mlstm_sample",
    )(qkv, qkv, qkv, og, g, gt_s, norm_w, C0, n0, m0_s)
    return outs[0], outs[1], outs[2], outs[3].reshape(Bs, H)


def _layer_norm(z, gain, bias):
    mu = jnp.mean(z, axis=1, keepdims=True)
    zc = z - mu
    var = jnp.mean(zc * zc, axis=1, keepdims=True)
    return zc * lax.rsqrt(var + LN_EPS) * gain + bias


def _proj_norm_body(ap_ref, as_ref, w_ref, x_ref, gain_ref, bias_ref, o_ref, ob_ref, *, alpha, npb):
    a = jnp.where(pl.program_id(0) < npb, ap_ref[...], as_ref[...])
    y = _layer_norm(alpha * x_ref[...] + _dot(a, w_ref[...]), gain_ref[...], bias_ref[...])
    o_ref[...] = y
    ob_ref[...] = y.astype(BF16)


def _proj_norm(a_prompt, a_sample, w, x, gain, bias, alpha):
    NP, K = a_prompt.shape
    NS = a_sample.shape[0]
    N, D = NP + NS, w.shape[1]
    tm = _pick(NS, ROW_TILES)
    assert NP % tm == 0
    npb = NP // tm
    return pl.pallas_call(
        functools.partial(_proj_norm_body, alpha=alpha, npb=npb),
        grid=(N // tm,),
        in_specs=[pl.BlockSpec((tm, K), lambda i: (jnp.minimum(i, npb - 1), 0)),
                  pl.BlockSpec((tm, K), lambda i: (jnp.maximum(i - npb, 0), 0)),
                  pl.BlockSpec((K, D), lambda i: (0, 0)),
                  pl.BlockSpec((tm, D), lambda i: (i, 0)),
                  pl.BlockSpec((1, D), lambda i: (0, 0)),
                  pl.BlockSpec((1, D), lambda i: (0, 0))],
        out_specs=[pl.BlockSpec((tm, D), lambda i: (i, 0)), pl.BlockSpec((tm, D), lambda i: (i, 0))],
        out_shape=[jax.ShapeDtypeStruct((N, D), F32), jax.ShapeDtypeStruct((N, D), BF16)],
        compiler_params=_cparams("parallel"),
        name="proj_norm",
    )(a_prompt, a_sample, w, x, gain[None, :], bias[None, :])


def _static_loop(n, fn):
    for r in range(n):
        fn(r)


def _rows_loop(n, fn, unroll=8):
    def body(r, c):
        fn(r)
        return c
    lax.fori_loop(0, n, body, 0, unroll=unroll)


def _combine_norm_body(slot_ref, slot_next_ref, gate_ref, x_ref, gain_ref, bias_ref, ys_hbm, oa_ref, ob_ref,
                       ybuf, sem, *, alpha, nblk, split_blk):
    i = pl.program_id(0)
    cur = lax.rem(i, 2)
    tm, D = x_ref.shape

    def row_copy(idx_ref, k, r, s):
        return pltpu.make_async_copy(ys_hbm.at[pl.ds(idx_ref[k, r], 1)], ybuf.at[s, k, pl.ds(r, 1)], sem.at[s])

    def start_block(idx_ref, s):
        for k in range(2):
            _rows_loop(tm, lambda r: row_copy(idx_ref, k, r, s).start())

    @pl.when(i == 0)
    def _():
        start_block(slot_ref, 0)

    @pl.when(i + 1 < nblk)
    def _():
        start_block(slot_next_ref, 1 - cur)

    for k in range(2):
        _static_loop(tm, lambda r: row_copy(slot_ref, k, r, cur).wait())
    g = gate_ref[...]
    moe = g[:, 0:1] * ybuf[cur, 0].reshape(tm, D) + g[:, 1:2] * ybuf[cur, 1].reshape(tm, D)
    y = _layer_norm(alpha * x_ref[...] + moe, gain_ref[...], bias_ref[...])
    if split_blk is None:
        oa_ref[...] = y
        ob_ref[...] = y.astype(BF16)
    else:
        @pl.when(i < split_blk)
        def _():
            oa_ref[...] = y

        @pl.when(i >= split_blk)
        def _():
            ob_ref[...] = y


def _combine_norm(x, ys, slot, gate_col, gain, bias, alpha, split_rows=None):
    N, D = x.shape
    tm = _pick(N if split_rows is None else math.gcd(split_rows, N - split_rows), (256, 128))
    nblk = N // tm
    idx_spec = lambda off: pl.BlockSpec((2, tm), lambda i: (0, jnp.minimum(i + off, nblk - 1)),
                                        memory_space=pltpu.SMEM)
    if split_rows is None:
        split_blk = None
        out_specs = [pl.BlockSpec((tm, D), lambda i: (i, 0)), pl.BlockSpec((tm, D), lambda i: (i, 0))]
        out_shape = [jax.ShapeDtypeStruct((N, D), F32), jax.ShapeDtypeStruct((N, D), BF16)]
    else:
        split_blk = split_rows // tm
        out_specs = [pl.BlockSpec((tm, D), lambda i: (jnp.minimum(i, split_blk - 1), 0)),
                     pl.BlockSpec((tm, D), lambda i: (jnp.maximum(i - split_blk, 0), 0))]
        out_shape = [jax.ShapeDtypeStruct((split_rows, D), F32), jax.ShapeDtypeStruct((N - split_rows, D), F32)]
    return pl.pallas_call(
        functools.partial(_combine_norm_body, alpha=alpha, nblk=nblk, split_blk=split_blk),
        grid=(nblk,),
        in_specs=[idx_spec(0), idx_spec(1),
                  pl.BlockSpec((tm, gate_col.shape[1]), lambda i: (i, 0)),
                  pl.BlockSpec((tm, D), lambda i: (i, 0)),
                  pl.BlockSpec((1, D), lambda i: (0, 0)),
                  pl.BlockSpec((1, D), lambda i: (0, 0)),
                  pl.BlockSpec(memory_space=pl.ANY)],
        out_specs=out_specs,
        out_shape=out_shape,
        scratch_shapes=[pltpu.VMEM((2, 2, tm) + ys.shape[1:], F32), pltpu.SemaphoreType.DMA((2,))],
        compiler_params=_cparams("arbitrary"),
        name="combine_norm",
    )(slot, slot, gate_col, x, gain[None, :], bias[None, :], ys)


def _router_body(x_ref, wt_ref, b_ref, eid_ref, gate_ref, *, E):
    logits = _dot3(wt_ref[...], x_ref[...], _NT)
    ex = jnp.exp(logits - jnp.max(logits, axis=0, keepdims=True))
    probs = ex / jnp.sum(ex, axis=0, keepdims=True)
    sel = probs + b_ref[...]
    per = E // N_GROUPS
    s = [sel[e:e + 1, :] for e in range(E)]
    p = [probs[e:e + 1, :] for e in range(E)]

    def top2_sum(vals):
        best = None
        for a in range(len(vals)):
            for b in range(a + 1, len(vals)):
                pair = vals[a] + vals[b]
                best = pair if best is None else jnp.maximum(best, pair)
        return best

    g_best = top2_sum(s[:per])
    g_idx = jnp.zeros_like(g_best, dtype=jnp.int32)
    for gi in range(1, N_GROUPS):
        score = top2_sum(s[gi * per:(gi + 1) * per])
        better = score > g_best
        g_idx = jnp.where(better, gi, g_idx)
        g_best = jnp.where(better, score, g_best)

    def in_group(vals, j):
        out = vals[j]
        for gi in range(1, N_GROUPS):
            out = jnp.where(g_idx == gi, vals[gi * per + j], out)
        return out

    sg = [in_group(s, j) for j in range(per)]
    pg = [in_group(p, j) for j in range(per)]
    i1 = jnp.zeros_like(g_idx)
    b1 = sg[0]
    for j in range(1, per):
        better = sg[j] > b1
        i1 = jnp.where(better, j, i1)
        b1 = jnp.where(better, sg[j], b1)
    i2 = jnp.zeros_like(g_idx)
    b2 = jnp.full_like(b1, -jnp.inf)
    for j in range(per):
        better = (i1 != j) & (sg[j] > b2)
        i2 = jnp.where(better, j, i2)
        b2 = jnp.where(better, sg[j], b2)
    p1 = pg[0]
    p2 = pg[0]
    for j in range(1, per):
        p1 = jnp.where(i1 == j, pg[j], p1)
        p2 = jnp.where(i2 == j, pg[j], p2)
    tot = p1 + p2
    eid_ref[...] = jnp.concatenate([g_idx * per + i1, g_idx * per + i2], axis=0)
    gates = jnp.concatenate([p1 / tot, p2 / tot, jnp.zeros((GATE_COLS - 2, p1.shape[1]), F32)], axis=0)
    gate_ref[...] = gates.T


def _router(x, w_router, b_router):
    N, D = x.shape
    E = w_router.shape[1]
    assert E % N_GROUPS == 0 and E // N_GROUPS >= 2
    tm = _pick(N, ROW_TILES)
    return pl.pallas_call(
        functools.partial(_router_body, E=E),
        grid=(N // tm,),
        in_specs=[pl.BlockSpec((tm, D), lambda i: (i, 0)),
                  pl.BlockSpec((E, D), lambda i: (0, 0)),
                  pl.BlockSpec((E, 1), lambda i: (0, 0))],
        out_specs=[pl.BlockSpec((2, tm), lambda i: (0, i)), pl.BlockSpec((tm, GATE_COLS), lambda i: (i, 0))],
        out_shape=[jax.ShapeDtypeStruct((2, N), jnp.int32), jax.ShapeDtypeStruct((N, GATE_COLS), F32)],
        compiler_params=_cparams("parallel"),
        name="router",
    )(x, w_router.T, b_router.astype(F32)[:, None])


def _plan_body(e_ref, slot_ref, cnt_ref, *, E, TM):
    e = e_ref[...]
    R = e.shape[0]
    upper = (lax.broadcasted_iota(jnp.int32, (LANES, LANES), 0)
             <= lax.broadcasted_iota(jnp.int32, (LANES, LANES), 1)).astype(BF16)
    below = (lax.broadcasted_iota(jnp.int32, (R, R), 1) < lax.broadcasted_iota(jnp.int32, (R, R), 0)).astype(BF16)
    lane = lax.broadcasted_iota(jnp.int32, (1, LANES), 1)
    slot = jnp.zeros(e.shape, F32)
    counts = jnp.zeros((1, LANES), F32)
    start = jnp.zeros((1, 1), F32)
    for ex in range(E):
        hit = e == ex
        within = _dot(hit.astype(BF16), upper)
        rows_before = _dot(below, within.astype(BF16))[:, LANES - 1:LANES]
        rank = within - 1.0 + rows_before
        count = rows_before[R - 1:R, :] + within[R - 1:R, LANES - 1:LANES]
        slot = jnp.where(hit, start + rank, slot)
        counts = jnp.where(lane == ex, count, counts)
        start = start + jnp.floor((count + (TM - 1)) * (1.0 / TM)) * TM
    slot_ref[...] = slot.astype(jnp.int32)
    cnt_ref[...] = jnp.broadcast_to(counts, cnt_ref.shape).astype(jnp.int32)


def _plan(eid, E, TM):
    N = eid.shape[1]
    A = 2 * N
    assert A % LANES == 0 and (TM & (TM - 1)) == 0
    R = A // LANES
    Rp = -(-R // LANES) * LANES
    e2d = jnp.pad(eid.reshape(R, LANES), ((0, Rp - R), (0, 0)), constant_values=-1)
    slot2d, cnt = pl.pallas_call(
        functools.partial(_plan_body, E=E, TM=TM),
        out_shape=[jax.ShapeDtypeStruct((Rp, LANES), jnp.int32), jax.ShapeDtypeStruct((8, LANES), jnp.int32)],
        compiler_params=pltpu.CompilerParams(vmem_limit_bytes=VMEM_LIMIT_BYTES),
        name="route_plan",
    )(e2d)
    return slot2d[:R].reshape(2, N), cnt[0, :E]


def _group_table(counts, TM, n_blocks):
    E = counts.shape[0]
    GB = MOE_GROUP_BLOCKS
    nblk = (counts + TM - 1) // TM
    blk_end = jnp.cumsum(nblk)
    blk_start = blk_end - nblk
    ngrp = (nblk + GB - 1) // GB
    grp_end = jnp.cumsum(ngrp)
    n_groups_max = E + n_blocks // GB
    g = jnp.arange(n_groups_max, dtype=jnp.int32)
    g_expert = jnp.minimum(jnp.sum((grp_end[None, :] <= g[:, None]).astype(jnp.int32), axis=1), E - 1)
    j = g - (grp_end - ngrp)[g_expert]
    g_start = blk_start[g_expert] + j * GB
    g_nblk = jnp.clip(nblk[g_expert] - j * GB, 0, GB)
    live = g < grp_end[-1]
    g_nblk = jnp.where(live, g_nblk, 0)
    meta = jnp.stack([grp_end[-1], blk_end[-1]]).astype(jnp.int32)
    pad_start = blk_start * TM + counts
    pad_count = nblk * TM - counts
    return (g_expert.astype(jnp.int32), g_start.astype(jnp.int32), g_nblk.astype(jnp.int32), meta,
            pad_start.astype(jnp.int32), pad_count.astype(jnp.int32))


def _pack_halves(x):
    half = x.shape[1] // 2
    bits = lax.bitcast_convert_type(x.astype(BF16).astype(F32), jnp.uint32)
    return (bits[:, :half] >> 16) | bits[:, half:]


def _unpack_halves(p):
    lo = lax.bitcast_convert_type(p << 16, F32).astype(BF16)
    hi = lax.bitcast_convert_type(p & jnp.uint32(0xFFFF0000), F32).astype(BF16)
    return lo, hi


def _dispatch_body(pad_start_ref, pad_count_ref, meta_ref, slot_ref, x_ref, xs_hbm, pbuf, zbuf, sem, zsem,
                   *, nblk, E, TM, n_blocks):
    i = pl.program_id(0)
    cur = lax.rem(i, 2)
    tm = x_ref.shape[0]

    def row_copy(k, r, s):
        return pltpu.make_async_copy(pbuf.at[s, pl.ds(r, 1)], xs_hbm.at[pl.ds(slot_ref[k, r], 1)], sem.at[s])

    def wait_block(s):
        for k in range(2):
            _static_loop(tm, lambda r: row_copy(k, r, s).wait())

    def zero_copy(row):
        return pltpu.make_async_copy(zbuf.at[pl.ds(0, 1)], xs_hbm.at[pl.ds(row, 1)], zsem.at[0])

    @pl.when(i == 0)
    def _():
        zbuf[...] = jnp.zeros_like(zbuf)
        for ex in range(E):
            _rows_loop(pad_count_ref[ex], lambda j: zero_copy(pad_start_ref[ex] + j).start(), unroll=1)
        tail0 = meta_ref[1] * TM
        ntail = n_blocks * TM - tail0
        _rows_loop(ntail, lambda j: zero_copy(tail0 + j).start(), unroll=1)
        for ex in range(E):
            _rows_loop(pad_count_ref[ex], lambda j: zero_copy(pad_start_ref[ex] + j).wait(), unroll=1)
        _rows_loop(ntail, lambda j: zero_copy(tail0 + j).wait(), unroll=1)

    @pl.when(i >= 2)
    def _():
        wait_block(cur)

    pbuf[cur] = _pack_halves(x_ref[...]).reshape(pbuf.shape[1:])
    for k in range(2):
        _rows_loop(tm, lambda r: row_copy(k, r, cur).start())

    @pl.when(i == nblk - 1)
    def _():
        wait_block(cur)
        if nblk >= 2:
            wait_block(1 - cur)


def _dispatch(x, slot, pad_start, pad_count, meta, TM, n_blocks):
    N, D = x.shape
    E = pad_start.shape[0]
    tm = _pick(N, (256, 128))
    nblk = N // tm
    row_tile = (D // 2 // LANES, LANES)
    grid_spec = pltpu.PrefetchScalarGridSpec(
        num_scalar_prefetch=3,
        grid=(nblk,),
        in_specs=[pl.BlockSpec((2, tm), lambda i, *_: (0, i), memory_space=pltpu.SMEM),
                  pl.BlockSpec((tm, D), lambda i, *_: (i, 0))],
        out_specs=pl.BlockSpec(memory_space=pl.ANY),
        scratch_shapes=[pltpu.VMEM((2, tm) + row_tile, jnp.uint32), pltpu.VMEM((1,) + row_tile, jnp.uint32),
                        pltpu.SemaphoreType.DMA((2,)), pltpu.SemaphoreType.DMA((1,))],
    )
    return pl.pallas_call(
        functools.partial(_dispatch_body, nblk=nblk, E=E, TM=TM, n_blocks=n_blocks),
        grid_spec=grid_spec,
        out_shape=jax.ShapeDtypeStruct((n_blocks * TM,) + row_tile, jnp.uint32),
        compiler_params=_cparams("arbitrary"),
        name="moe_dispatch",
    )(pad_start, pad_count, meta, slot, x)


def _experts_body(ge_ref, gs_ref, gn_ref, meta_ref, xs_hbm, wg_hbm, wu_hbm, wd_hbm, ys_hbm,
                  xraw, xg, acc, stage_g, stage_u, stage_d, wgu, wdb, obuf, xsem, wsem, osem,
                  *, layer, chunks, TM, n_blocks):
    g = pl.program_id(0)
    n_groups = meta_ref[0]
    half = xg.shape[2]
    n_chunks = len(chunks)

    def x_copy(b0, rb):
        return pltpu.make_async_copy(xs_hbm.at[pl.ds((b0 + rb) * TM, TM)], xraw.at[rb], xsem.at[0])

    def w_copies(ex, c, s):
        f0, fw = chunks[c]
        return (pltpu.make_async_copy(wg_hbm.at[layer, ex, :, pl.ds(f0, fw)], stage_g.at[s, :, pl.ds(0, fw)],
                                      wsem.at[s, 0]),
                pltpu.make_async_copy(wu_hbm.at[layer, ex, :, pl.ds(f0, fw)], stage_u.at[s, :, pl.ds(0, fw)],
                                      wsem.at[s, 1]),
                pltpu.make_async_copy(wd_hbm.at[layer, ex, pl.ds(f0, fw), :], stage_d.at[s, pl.ds(0, fw), :],
                                      wsem.at[s, 2]))

    def out_copy(b0, rb, s):
        return pltpu.make_async_copy(obuf.at[s], ys_hbm.at[pl.ds((b0 + rb) * TM, TM)], osem.at[s])

    @pl.when(g < n_groups)
    def _():
        ex = ge_ref[g]
        b0 = gs_ref[g]
        nb = gn_ref[g]
        _rows_loop(nb, lambda rb: x_copy(b0, rb).start(), unroll=1)

        def start_weights(expert, c, s):
            for idx, cp in enumerate(w_copies(expert, c, s)):
                cp.start(priority=idx % 2)

        @pl.when(g == 0)
        def _():
            start_weights(ex, 0, 0)

        _rows_loop(nb, lambda rb: x_copy(b0, rb).wait(), unroll=1)

        def untile(rb):
            xg[rb] = xraw[rb].reshape(TM, half)

        _rows_loop(nb, untile, unroll=1)

        for c, (f0, fw) in enumerate(chunks):
            s = c % 2
            for cp in w_copies(ex, c, s):
                cp.wait()
            if c + 1 < n_chunks:
                start_weights(ex, c + 1, 1 - s)
            else:
                @pl.when(g + 1 < n_groups)
                def _():
                    start_weights(ge_ref[g + 1], 0, 1 - s)
            wgu[:, 0:fw] = stage_g[s, :, 0:fw].astype(BF16)
            wgu[:, fw:2 * fw] = stage_u[s, :, 0:fw].astype(BF16)
            wdb[0:fw, :] = stage_d[s, 0:fw, :].astype(BF16)

            def block(rb, carry, c=c, fw=fw):
                last = c + 1 == n_chunks
                buf = lax.rem(rb, 2)
                if last:
                    @pl.when(rb >= 2)
                    def _():
                        out_copy(b0, rb, buf).wait()
                xl, xh = _unpack_halves(xg[rb])
                gu = _dot(xl, wgu[0:half, 0:2 * fw]) + _dot(xh, wgu[half:2 * half, 0:2 * fw])
                hidden = jax.nn.silu(gu[:, 0:fw]) * gu[:, fw:2 * fw]
                part = _dot(hidden.astype(BF16), wdb[0:fw, :])
                if c == 0:
                    acc[rb] = part
                elif not last:
                    acc[rb] = acc[rb] + part
                else:
                    obuf[buf] = acc[rb] + part
                    out_copy(b0, rb, buf).start()
                return carry

            lax.fori_loop(0, nb, block, 0)

        out_copy(b0, 0, lax.rem(nb - 1, 2)).wait()

        @pl.when(nb >= 2)
        def _():
            out_copy(b0, 0, lax.rem(nb, 2)).wait()

    @pl.when(g == pl.num_programs(0) - 1)
    def _():
        used = meta_ref[1]
        obuf[0] = jnp.zeros(obuf.shape[1:], F32)
        _rows_loop(n_blocks - used, lambda j: out_copy(used, j, 0).start(), unroll=1)
        _rows_loop(n_blocks - used, lambda j: out_copy(used, j, 0).wait(), unroll=1)


def _experts(xs, g_expert, g_start, g_nblk, meta, w_gate, w_up, w_down, layer, TM, n_blocks):
    _, E, D, F = w_gate.shape
    GB = MOE_GROUP_BLOCKS
    FC = MOE_FC
    chunks = tuple((f0, min(FC, F - f0)) for f0 in range(0, F, FC))
    assert len(chunks) % 2 == 0 and all(fw % LANES == 0 for _, fw in chunks)
    out_tile = (D,)
    grid_spec = pltpu.PrefetchScalarGridSpec(
        num_scalar_prefetch=4,
        grid=(g_expert.shape[0],),
        in_specs=[pl.BlockSpec(memory_space=pl.ANY)] * 4,
        out_specs=pl.BlockSpec(memory_space=pl.ANY),
        scratch_shapes=[pltpu.VMEM((GB, TM) + xs.shape[1:], jnp.uint32),
                        pltpu.VMEM((GB, TM, D // 2), jnp.uint32), pltpu.VMEM((GB, TM, D), F32),
                        pltpu.VMEM((2, D, FC), F32), pltpu.VMEM((2, D, FC), F32), pltpu.VMEM((2, FC, D), F32),
                        pltpu.VMEM((D, 2 * FC), BF16), pltpu.VMEM((FC, D), BF16),
                        pltpu.VMEM((2, TM) + out_tile, F32),
                        pltpu.SemaphoreType.DMA((1,)), pltpu.SemaphoreType.DMA((2, 3)),
                        pltpu.SemaphoreType.DMA((2,))],
    )
    return pl.pallas_call(
        functools.partial(_experts_body, layer=layer, chunks=chunks, TM=TM, n_blocks=n_blocks),
        grid_spec=grid_spec,
        out_shape=jax.ShapeDtypeStruct((n_blocks * TM,) + out_tile, F32),
        compiler_params=_cparams("arbitrary"),
        name="moe_experts",
    )(g_expert, g_start, g_nblk, meta, xs, w_gate, w_up, w_down)


def _moe_layer(x, w_router, b_router, w_gate, w_up, w_down, layer, gain, bias, alpha, split_rows=None):
    N = x.shape[0]
    E = w_router.shape[1]
    TM = MOE_TM
    n_blocks = -(-(2 * N + E * (TM - 1)) // TM)
    eid, gate_col = _router(x, w_router, b_router)
    slot, counts = _plan(eid, E, TM)
    g_expert, g_start, g_nblk, meta, pad_start, pad_count = _group_table(counts, TM, n_blocks)
    xs = _dispatch(x, slot, pad_start, pad_count, meta, TM, n_blocks)
    ys = _experts(xs, g_expert, g_start, g_nblk, meta, w_gate, w_up, w_down, layer, TM, n_blocks)
    return _combine_norm(x, ys, slot, gate_col, gain, bias, alpha, split_rows)


def _pair_attention(q_pairs, kc, vc, bias, sinks_lo, sinks_hi):
    P = len(q_pairs)
    R, lanes = q_pairs[0].shape
    lo = lax.broadcasted_iota(jnp.int32, (R, lanes), 1) < lanes // 2
    zero = jnp.zeros((R, lanes), BF16)
    qs = jnp.concatenate([jnp.where(lo, qp, zero) for qp in q_pairs]
                         + [jnp.where(lo, zero, qp) for qp in q_pairs], axis=0)
    sink_col = jnp.concatenate([jnp.full((R, 1), s, F32) for s in list(sinks_lo) + list(sinks_hi)], axis=0)
    s = _dg(qs, kc, _NT) + jnp.concatenate([bias] * (2 * P), axis=0)
    mx = jnp.maximum(jnp.max(s, axis=1, keepdims=True), sink_col)
    e = jnp.exp(s - mx)
    den = jnp.sum(e, axis=1, keepdims=True) + jnp.exp(sink_col - mx)
    o = _dot(e.astype(BF16), vc) * (1.0 / den)
    return [jnp.where(lo, o[p * R:(p + 1) * R], o[(P + p) * R:(P + p + 1) * R]) for p in range(P)]


def _swa_prompt_body(sink_ref, q_ref, kp_ref, kc_ref, vp_ref, vc_ref, o_ref, *, KVH, G):
    W = q_ref.shape[0]
    P = G // 2
    first = pl.program_id(1) == 0
    r = lax.broadcasted_iota(jnp.int32, (W, 2 * W), 0)
    kj = lax.broadcasted_iota(jnp.int32, (W, 2 * W), 1)
    diff = r + W - kj
    valid = (diff >= 0) & (diff < W) & (jnp.logical_not(first) | (kj >= W))
    bias = jnp.where(valid, 0.0, -jnp.inf)
    for g in range(KVH):
        ks = slice(g * LANES, (g + 1) * LANES)
        kc = jnp.concatenate([kp_ref[:, ks], kc_ref[:, ks]], axis=0)
        vc = jnp.concatenate([vp_ref[:, ks], vc_ref[:, ks]], axis=0)
        cols = [slice((g * P + p) * LANES, (g * P + p + 1) * LANES) for p in range(P)]
        outs = _pair_attention([q_ref[:, c] for c in cols], kc, vc, bias,
                               [sink_ref[g * G + 2 * p] for p in range(P)],
                               [sink_ref[g * G + 2 * p + 1] for p in range(P)])
        for c, o in zip(cols, outs):
            o_ref[:, c] = o.astype(BF16)


def _swa_prompt(q, kdup, vdup, sinks, B, T, KVH):
    Hq = sinks.shape[0]
    W = WINDOW
    assert T % W == 0 and kdup.shape[1] == KVH * LANES and q.shape[1] == Hq // 2 * LANES
    nb = T // W
    cur = lambda b, n: (b * nb + n, 0)
    prev = lambda b, n: (b * nb + jnp.maximum(n - 1, 0), 0)
    return pl.pallas_call(
        functools.partial(_swa_prompt_body, KVH=KVH, G=Hq // KVH),
        grid=(B, nb),
        in_specs=[pl.BlockSpec(memory_space=pltpu.SMEM),
                  pl.BlockSpec((W, q.shape[1]), cur),
                  pl.BlockSpec((W, KVH * LANES), prev), pl.BlockSpec((W, KVH * LANES), cur),
                  pl.BlockSpec((W, KVH * LANES), prev), pl.BlockSpec((W, KVH * LANES), cur)],
        out_specs=pl.BlockSpec((W, q.shape[1]), cur),
        out_shape=jax.ShapeDtypeStruct((B * T, q.shape[1]), BF16),
        compiler_params=_cparams("parallel", "parallel"),
        name="swa_prompt",
    )(sinks, q, kdup, kdup, vdup, vdup)


def _swa_sample_body(sink_ref, q_ref, k_ref, v_ref, o_ref, *, KVH, G, TS, WB):
    K = k_ref.shape[1]
    P = G // 2
    q = q_ref[...].astype(F32)
    r = lax.broadcasted_iota(jnp.int32, (TS, K), 0)
    kj = lax.broadcasted_iota(jnp.int32, (TS, K), 1)
    diff = WB + r - kj
    bias = jnp.where((diff >= 0) & (diff < WINDOW), 0.0, -jnp.inf)
    for bb in range(SAMPLE_BS):
        rows = slice(bb * TS, (bb + 1) * TS)
        for g in range(KVH):
            ks = slice(g * LANES, (g + 1) * LANES)
            cols = [slice((g * P + p) * LANES, (g * P + p + 1) * LANES) for p in range(P)]
            outs = _pair_attention([q[rows, c].astype(BF16) for c in cols], k_ref[bb, :, ks], v_ref[bb, :, ks], bias,
                                   [sink_ref[g * G + 2 * p] for p in range(P)],
                                   [sink_ref[g * G + 2 * p + 1] for p in range(P)])
            for c, o in zip(cols, outs):
                o_ref[rows, c] = o.astype(BF16)


def _swa_sample(q, k_all, v_all, sinks, NP, TS, WB, KVH):
    Bs, K, _ = k_all.shape
    Hq = sinks.shape[0]
    BS = SAMPLE_BS
    R = BS * TS
    off = NP // R
    row = lambda i: (off + i, 0)
    return pl.pallas_call(
        functools.partial(_swa_sample_body, KVH=KVH, G=Hq // KVH, TS=TS, WB=WB),
        grid=(Bs // BS,),
        in_specs=[pl.BlockSpec(memory_space=pltpu.SMEM),
                  pl.BlockSpec((R, q.shape[1]), row),
                  pl.BlockSpec((BS, K, KVH * LANES), lambda i: (i, 0, 0)),
                  pl.BlockSpec((BS, K, KVH * LANES), lambda i: (i, 0, 0))],
        out_specs=pl.BlockSpec((R, q.shape[1]), lambda i: (i, 0)),
        out_shape=jax.ShapeDtypeStruct((Bs * TS, q.shape[1]), BF16),
        compiler_params=_cparams("parallel"),
        name="swa_sample",
    )(sinks, q, k_all, v_all)


def _rope_tables(pos, head_dim, lanes):
    half = head_dim // 2
    inv = ROPE_THETA ** (-jnp.arange(half, dtype=F32) / half)
    ang = pos.astype(F32)[:, None] * inv[None, :]
    reps = lanes // half
    return jnp.tile(jnp.cos(ang), (1, reps)), jnp.tile(jnp.sin(ang), (1, reps))


def kernel(x_prompt, x_sample, state_C, state_n, state_m, cache_k, cache_v, w_in_a, b_i_a, b_f_a, norm_a, w_out_a,
           w_kv, w_q_b, sinks_b, w_out_b, w_router, b_router, w_gate_e, w_up_e, w_down_e, ln_g, ln_b):
    B, T, D = x_prompt.shape
    Bs, TS, _ = x_sample.shape
    depth = ln_g.shape[0]
    assert depth == 2 and w_in_a.shape[0] == 1 and w_q_b.shape[0] == 1
    alpha = (2 * depth) ** 0.25
    NP, NS = B * T, Bs * TS
    H = b_i_a.shape[1]
    DK, DV = state_C.shape[3], state_C.shape[4]
    WB, KVH, HD = cache_k.shape[1], cache_k.shape[2], cache_k.shape[3]
    E = w_router.shape[1]

    x0 = jnp.concatenate([x_prompt.reshape(NP, D), x_sample.reshape(NS, D)], axis=0).astype(F32)
    x0b = x0.astype(BF16)

    w_in = w_in_a[0]
    nqk, nv = H * DK, H * DV
    w_qkv = w_in[:, :2 * nqk + nv].astype(BF16)
    w_og = w_in[:, 2 * nqk + nv:2 * nqk + nv + D].astype(BF16)
    w_gates = w_in[:, 2 * nqk + nv + D:].astype(F32)
    qkv_scale = jnp.concatenate([jnp.ones((nqk,), F32), jnp.full((nqk,), DK ** -0.5, F32), jnp.ones((nv,), F32)])
    qkv = _matmul(x0b, w_qkv, qkv_scale[None, :], BF16)
    og = _matmul(x0b, w_og, jnp.ones((1, D), F32), F32)
    g, gt = _mlstm_gates(x0, w_gates, b_i_a[0], b_f_a[0])
    norm_w = norm_a[0].astype(F32)[None, :]
    h_p, C_p, n_p, m_p = _mlstm_prompt(qkv, og, g, gt, norm_w, B, T, H, DK, DV)
    h_s, C_s, n_s, m_s = _mlstm_sample(qkv, og, g, gt, norm_w, state_C[0].astype(F32), state_n[0].astype(F32),
                                       state_m[0].astype(F32), NP, TS)
    x1, _ = _proj_norm(h_p, h_s, w_out_a[0].astype(BF16), x0, ln_g[0, 0], ln_b[0, 0], alpha)
    x2, x2b = _moe_layer(x1, w_router, b_router, w_gate_e, w_up_e, w_down_e, 0, ln_g[0, 1], ln_b[0, 1], alpha)

    pos = jnp.concatenate([jnp.tile(jnp.arange(T, dtype=jnp.int32), B),
                           jnp.tile(PAST_LEN + jnp.arange(TS, dtype=jnp.int32), Bs)])
    cos, sin = _rope_tables(pos, HD, 128)
    nkv = KVH * HD
    k_new = _matmul_rope(x2b, w_kv[:, :nkv].astype(BF16), cos, sin, HD, F32)
    v_new = _matmul(x2b, w_kv[:, nkv:].astype(BF16), jnp.ones((1, nkv), F32), F32)
    assert 2 * HD == LANES
    qb = _matmul_rope(x2b, w_q_b[0].astype(BF16), cos, sin, HD, BF16, scale=HD ** -0.5)

    def dup_heads(a):
        a4 = a.reshape(a.shape[:-1] + (KVH, HD)).astype(BF16)
        return jnp.concatenate([a4, a4], axis=-1).reshape(a.shape[:-1] + (KVH * LANES,))

    o_p = _swa_prompt(qb, dup_heads(k_new), dup_heads(v_new), sinks_b[0].astype(F32), B, T, KVH)
    k_s_new = k_new[NP:].reshape(Bs, TS, nkv)
    v_s_new = v_new[NP:].reshape(Bs, TS, nkv)
    k_cat = jnp.concatenate([cache_k.reshape(Bs, WB, nkv).astype(F32), k_s_new], axis=1)
    v_cat = jnp.concatenate([cache_v.reshape(Bs, WB, nkv).astype(F32), v_s_new], axis=1)
    kpad = -(-(WB + TS) // LANES) * LANES - (WB + TS)
    k_all = dup_heads(jnp.pad(k_cat, ((0, 0), (0, kpad), (0, 0))))
    v_all = dup_heads(jnp.pad(v_cat, ((0, 0), (0, kpad), (0, 0))))
    o_s = _swa_sample(qb, k_all, v_all, sinks_b[0].astype(F32), NP, TS, WB, KVH)
    x3, _ = _proj_norm(o_p, o_s, w_out_b[0].astype(BF16), x2, ln_g[1, 0], ln_b[1, 0], alpha)
    y_p, y_s = _moe_layer(x3, w_router, b_router, w_gate_e, w_up_e, w_down_e, 1, ln_g[1, 1], ln_b[1, 1], alpha,
                          split_rows=NP)
    y_prompt = y_p.reshape(B, T, D)
    y_sample = y_s.reshape(Bs, TS, D)
    k_p = k_new[:NP].reshape(B, T, KVH, HD)[:, -WB:]
    v_p = v_new[:NP].reshape(B, T, KVH, HD)[:, -WB:]
    k_s = k_cat[:, -WB:].reshape(Bs, WB, KVH, HD)
    v_s = v_cat[:, -WB:].reshape(Bs, WB, KVH, HD)
    return (y_prompt, y_sample, C_p[None], n_p[None], m_p.reshape(1, B, H), k_p, v_p,
            C_s[None], n_s[None], m_s[None], k_s, v_s)
```

```python
import functools
import math

import jax
import jax.numpy as jnp
from jax import lax
from jax.experimental import pallas as pl
from jax.experimental.pallas import tpu as pltpu

F32 = jnp.float32
BF16 = jnp.bfloat16

GATE_CAP = 15.0
LN_EPS = 1e-5
ROPE_THETA = 10000.0
WINDOW = 128
PAST_LEN = 8192
N_GROUPS = 4

VMEM_LIMIT_BYTES = 56 * 1024 * 1024
MLSTM_CHUNK = 256
MOE_TM = 256
MOE_GROUP_BLOCKS = 6
MOE_FC = 256
GATE_COLS = 8
LANES = 128
SAMPLE_BS = 2
ROW_TILES = (512, 256, 128)
MATMUL_ROW_TILES = (1024,) + ROW_TILES

_NT = (((1,), (1,)), ((), ()))
_TN = (((0,), (0,)), ((), ()))


def _pick(n, cands):
    for c in cands:
        if n % c == 0:
            return c
    raise ValueError(f"no tile in {cands} divides {n}")


def _cparams(*sem):
    return pltpu.CompilerParams(dimension_semantics=sem, vmem_limit_bytes=VMEM_LIMIT_BYTES)


def _dot(a, b):
    return jnp.dot(a, b, preferred_element_type=F32)


def _dg(a, b, dims):
    return lax.dot_general(a, b, dims, preferred_element_type=F32)


def _mm_scale_body(a_ref, b_ref, s_ref, o_ref):
    o_ref[...] = (_dot(a_ref[...], b_ref[...].astype(BF16)) * s_ref[...]).astype(o_ref.dtype)


def _matmul(a, b, col_scale, out_dtype, col0=0, ncols=None):
    M, K = a.shape
    N = b.shape[1] if ncols is None else ncols
    tm = _pick(M, MATMUL_ROW_TILES)
    tn = _pick(math.gcd(N, col0) if col0 else N, (1024, 512, 256))
    j0 = col0 // tn
    return pl.pallas_call(
        _mm_scale_body,
        grid=(M // tm, N // tn),
        in_specs=[pl.BlockSpec((tm, K), lambda i, j: (i, 0)),
                  pl.BlockSpec((K, tn), lambda i, j: (0, j0 + j)),
                  pl.BlockSpec((1, tn), lambda i, j: (0, j))],
        out_specs=pl.BlockSpec((tm, tn), lambda i, j: (i, j)),
        out_shape=jax.ShapeDtypeStruct((M, N), out_dtype),
        compiler_params=_cparams("parallel", "parallel"),
        name="matmul",
    )(a, b, col_scale)


def _mm_rope_body(a_ref, b_ref, cos_ref, sin_ref, o_ref, *, half, scale):
    acc = _dot(a_ref[...], b_ref[...])
    tn = acc.shape[1]
    reps = tn // cos_ref.shape[1]
    cos = jnp.concatenate([cos_ref[...]] * reps, axis=1)
    sin = jnp.concatenate([sin_ref[...]] * reps, axis=1)
    lane = lax.broadcasted_iota(jnp.int32, acc.shape, 1)
    partner = jnp.where(lane % (2 * half) < half, -pltpu.roll(acc, tn - half, 1), pltpu.roll(acc, half, 1))
    o_ref[...] = ((acc * cos + partner * sin) * scale).astype(o_ref.dtype)


def _matmul_rope(a, b, cos, sin, head_dim, out_dtype, scale=1.0):
    M, K = a.shape
    N = b.shape[1]
    tm = _pick(M, MATMUL_ROW_TILES)
    tn = _pick(N, (1024, 512, 256))
    assert math.frexp(scale)[0] == 0.5
    return pl.pallas_call(
        functools.partial(_mm_rope_body, half=head_dim // 2, scale=scale),
        grid=(M // tm, N // tn),
        in_specs=[pl.BlockSpec((tm, K), lambda i, j: (i, 0)),
                  pl.BlockSpec((K, tn), lambda i, j: (0, j)),
                  pl.BlockSpec((tm, cos.shape[1]), lambda i, j: (i, 0)),
                  pl.BlockSpec((tm, sin.shape[1]), lambda i, j: (i, 0))],
        out_specs=pl.BlockSpec((tm, tn), lambda i, j: (i, j)),
        out_shape=jax.ShapeDtypeStruct((M, N), out_dtype),
        compiler_params=_cparams("parallel", "parallel"),
        name="matmul_rope",
    )(a, b, cos, sin)


def _split_bf16(x):
    hi = x.astype(BF16)
    lo = (x - hi.astype(F32)).astype(BF16)
    return hi, lo


def _dot3(a, b, dims):
    ah, al = _split_bf16(a)
    bh, bl = _split_bf16(b)
    return _dg(ah, bh, dims) + _dg(al, bh, dims) + _dg(ah, bl, dims)


def _softcap(z):
    return GATE_CAP * jnp.tanh(z / GATE_CAP)


def _log_sigmoid(z):
    return jnp.minimum(z, 0.0) - jnp.log1p(jnp.exp(-jnp.abs(z)))


def _gates_body(xp_ref, xs_ref, w_ref, wt_ref, brow_ref, bcol_ref, g_ref, gt_ref, xb_ref, *, H, npb):
    x = jnp.where(pl.program_id(0) < npb, xp_ref[...], xs_ref[...])
    xb_ref[...] = x.astype(BF16)
    pre = _dot3(x, w_ref[...], (((1,), (0,)), ((), ()))) + brow_ref[...]
    pre_t = _dot3(wt_ref[...], x, _NT) + bcol_ref[...]
    z = _softcap(pre)
    zt = _softcap(pre_t)
    is_in = lax.broadcasted_iota(jnp.int32, z.shape, 1) < H
    is_in_t = lax.broadcasted_iota(jnp.int32, zt.shape, 0) < H
    g_ref[...] = jnp.where(is_in, z, _log_sigmoid(z))
    gt_ref[...] = jnp.where(is_in_t, zt, _log_sigmoid(zt))


def _mlstm_gates(x_prompt, x_sample, w_gates, b_i, b_f):
    NP, D = x_prompt.shape
    NS = x_sample.shape[0]
    N = NP + NS
    H = b_i.shape[0]
    tm = _pick(NS, ROW_TILES)
    assert NP % tm == 0
    npb = NP // tm
    bias = jnp.concatenate([b_i, b_f]).astype(F32)
    return pl.pallas_call(
        functools.partial(_gates_body, H=H, npb=npb),
        grid=(N // tm,),
        in_specs=[pl.BlockSpec((tm, D), lambda i: (jnp.minimum(i, npb - 1), 0)),
                  pl.BlockSpec((tm, D), lambda i: (jnp.maximum(i - npb, 0), 0)),
                  pl.BlockSpec((D, 2 * H), lambda i: (0, 0)),
                  pl.BlockSpec((2 * H, D), lambda i: (0, 0)),
                  pl.BlockSpec((1, 2 * H), lambda i: (0, 0)),
                  pl.BlockSpec((2 * H, 1), lambda i: (0, 0))],
        out_specs=[pl.BlockSpec((tm, 2 * H), lambda i: (i, 0)),
                   pl.BlockSpec((2 * H, tm), lambda i: (0, i)),
                   pl.BlockSpec((tm, D), lambda i: (i, 0))],
        out_shape=[jax.ShapeDtypeStruct((N, 2 * H), F32), jax.ShapeDtypeStruct((2 * H, N), F32),
                   jax.ShapeDtypeStruct((N, D), BF16)],
        compiler_params=_cparams("parallel"),
        name="mlstm_gates",
    )(x_prompt, x_sample, w_gates, w_gates.T, bias[None, :], bias[:, None])


def _mlstm_head(qh, kh, vh, ig_col, lf_col, ig_row, lf_row, C, n_row, m_prev):
    L = qh.shape[0]
    t_idx = lax.broadcasted_iota(jnp.int32, (L, L), 0)
    s_idx = lax.broadcasted_iota(jnp.int32, (L, L), 1)
    causal = s_idx <= t_idx

    def mxu(x):
        return x.astype(BF16)

    b_col = jnp.sum(jnp.where(causal, lf_row, 0.0), axis=1, keepdims=True)
    b_row = jnp.sum(jnp.where(t_idx <= s_idx, lf_col, 0.0), axis=0, keepdims=True)
    dmat = jnp.where(causal, b_col - b_row + ig_row, -jnp.inf)
    a_col = b_col + m_prev
    m_t = jnp.maximum(a_col, jnp.max(dmat, axis=1, keepdims=True))
    wmat = jnp.exp(dmat - m_t) * _dg(qh, kh, _NT)
    inter = jnp.exp(a_col - m_t)
    qf = qh.astype(F32)
    kf = kh.astype(F32)
    num = inter * _dot(qh, mxu(C)) + _dot(mxu(wmat), vh)
    den = inter * jnp.sum(qf * n_row, axis=1, keepdims=True) + jnp.sum(wmat, axis=1, keepdims=True)
    hh = num * (1.0 / jnp.maximum(jnp.abs(den), jnp.exp(-m_t)))
    m_new = m_t[L - 1:L, :]
    b_last = b_col[L - 1:L, :]
    decay = jnp.exp(b_last - b_col + ig_col - m_new)
    carry = jnp.exp(b_last + m_prev - m_new)
    kd = kf * decay
    C_new = carry * C + _dg(mxu(kd), vh, _TN)
    n_new = carry * n_row + jnp.sum(kd, axis=0, keepdims=True)
    return hh, C_new, n_new, m_new


def _head_out(hh, nw, og):
    hn = hh * lax.rsqrt(jnp.mean(hh * hh, axis=1, keepdims=True) + LN_EPS)
    return hn * nw * jax.nn.sigmoid(og)


def _mlstm_prompt_body(q_ref, k_ref, v_ref, og_ref, g_ref, gt_ref, nw_ref,
                       h_ref, c_ref, n_ref, m_ref, *, H, DK, DV):
    @pl.when(pl.program_id(1) == 0)
    def _():
        c_ref[...] = jnp.zeros_like(c_ref)
        n_ref[...] = jnp.zeros_like(n_ref)
        m_ref[...] = jnp.zeros_like(m_ref)

    g = g_ref[...]
    gt = gt_ref[...]
    for h in range(H):
        kq = slice(h * DK, (h + 1) * DK)
        vv = slice(h * DV, (h + 1) * DV)
        hh, C_new, n_new, m_new = _mlstm_head(
            q_ref[:, kq], k_ref[:, kq], v_ref[:, vv],
            g[:, h:h + 1], g[:, H + h:H + h + 1], gt[h:h + 1, :], gt[H + h:H + h + 1, :],
            c_ref[0, h], n_ref[0, h:h + 1, :], m_ref[0, :, h:h + 1])
        c_ref[0, h] = C_new
        n_ref[0, h:h + 1, :] = n_new
        m_ref[0, :, h:h + 1] = m_new
        h_ref[:, vv] = _head_out(hh, nw_ref[:, vv], og_ref[:, vv]).astype(BF16)


def _mlstm_prompt(qkv, og, g, gt, norm_w, B, T, H, DK, DV):
    L = MLSTM_CHUNK
    assert T % L == 0 and (2 * H * DK) % (H * DV) == 0
    nc = T // L
    v_blk = 2 * DK // DV
    row = lambda b, c: (b * nc + c, 0)
    return pl.pallas_call(
        functools.partial(_mlstm_prompt_body, H=H, DK=DK, DV=DV),
        grid=(B, nc),
        in_specs=[pl.BlockSpec((L, H * DK), row), pl.BlockSpec((L, H * DK), lambda b, c: (b * nc + c, 1)),
                  pl.BlockSpec((L, H * DV), lambda b, c: (b * nc + c, v_blk)), pl.BlockSpec((L, H * DV), row),
                  pl.BlockSpec((L, 2 * H), row),
                  pl.BlockSpec((2 * H, L), lambda b, c: (0, b * nc + c)),
                  pl.BlockSpec((1, H * DV), lambda b, c: (0, 0))],
        out_specs=[pl.BlockSpec((L, H * DV), row),
                   pl.BlockSpec((1, H, DK, DV), lambda b, c: (b, 0, 0, 0)),
                   pl.BlockSpec((1, H, DK), lambda b, c: (b, 0, 0)),
                   pl.BlockSpec((1, 1, H), lambda b, c: (b, 0, 0))],
        out_shape=[jax.ShapeDtypeStruct((B * T, H * DV), BF16),
                   jax.ShapeDtypeStruct((B, H, DK, DV), F32),
                   jax.ShapeDtypeStruct((B, H, DK), F32),
                   jax.ShapeDtypeStruct((B, 1, H), F32)],
        compiler_params=_cparams("parallel", "arbitrary"),
        name="mlstm_prompt",
    )(qkv, qkv, qkv, og, g, gt, norm_w)


def _mlstm_sample_body(q_ref, k_ref, v_ref, og_ref, g_ref, gt_ref, nw_ref, c0_ref, n0_ref, m0_ref,
                       h_ref, c_ref, n_ref, m_ref, *, H, DK, DV, TS):
    q = q_ref[...].astype(F32)
    k = k_ref[...].astype(F32)
    v = v_ref[...].astype(F32)
    og = og_ref[...]
    g = g_ref[...]
    gt = gt_ref[0]
    HT = H * TS
    r_idx = lax.broadcasted_iota(jnp.int32, (HT, HT), 0)
    c_idx = lax.broadcasted_iota(jnp.int32, (HT, HT), 1)
    same_head = (r_idx // TS) == (c_idx // TS)
    causal = same_head & (c_idx <= r_idx)
    upto = same_head & (r_idx <= c_idx)

    def stack(parts):
        return jnp.concatenate(parts, axis=0)

    def mxu(x):
        return x.astype(BF16)

    for bb in range(SAMPLE_BS):
        rows = slice(bb * TS, (bb + 1) * TS)
        hrows = [slice(h * TS, (h + 1) * TS) for h in range(H)]
        kq = [slice(h * DK, (h + 1) * DK) for h in range(H)]
        vv = [slice(h * DV, (h + 1) * DV) for h in range(H)]
        qf = stack([q[rows, kq[h]] for h in range(H)])
        kf = stack([k[rows, kq[h]] for h in range(H)])
        v_h = [v[rows, vv[h]].astype(BF16) for h in range(H)]
        ig_col = stack([g[rows, h:h + 1] for h in range(H)])
        lf_col = stack([g[rows, H + h:H + h + 1] for h in range(H)])
        ig_row = jnp.concatenate([gt[h:h + 1, rows] for h in range(H)], axis=1)
        lf_row = jnp.concatenate([gt[H + h:H + h + 1, rows] for h in range(H)], axis=1)
        m_prev = [m0_ref[0, bb:bb + 1, h:h + 1] for h in range(H)]
        m_prev_col = stack([jnp.broadcast_to(m, (TS, 1)) for m in m_prev])

        b_col = jnp.sum(jnp.where(causal, lf_row, 0.0), axis=1, keepdims=True)
        b_row = jnp.sum(jnp.where(upto, lf_col, 0.0), axis=0, keepdims=True)
        dmat = jnp.where(causal, b_col - b_row + ig_row, -jnp.inf)
        a_col = b_col + m_prev_col
        m_t = jnp.maximum(a_col, jnp.max(dmat, axis=1, keepdims=True))
        wmat = jnp.exp(dmat - m_t) * _dg(mxu(qf), mxu(kf), _NT)
        inter = jnp.exp(a_col - m_t)
        C0 = [c0_ref[bb, h] for h in range(H)]
        n0 = [n0_ref[bb, h:h + 1, :] for h in range(H)]
        q_c = stack([_dot(mxu(qf[hrows[h]]), mxu(C0[h])) for h in range(H)])
        n_rows = stack([jnp.broadcast_to(n, (TS, DK)) for n in n0])
        num = inter * q_c + _dot(mxu(wmat), stack(v_h))
        den = inter * jnp.sum(qf * n_rows, axis=1, keepdims=True) + jnp.sum(wmat, axis=1, keepdims=True)
        hh = num * (1.0 / jnp.maximum(jnp.abs(den), jnp.exp(-m_t)))

        m_new = [m_t[(h + 1) * TS - 1:(h + 1) * TS, :] for h in range(H)]
        b_last = [b_col[(h + 1) * TS - 1:(h + 1) * TS, :] for h in range(H)]
        m_new_col = stack([jnp.broadcast_to(m, (TS, 1)) for m in m_new])
        b_last_col = stack([jnp.broadcast_to(b, (TS, 1)) for b in b_last])
        kd = kf * jnp.exp(b_last_col - b_col + ig_col - m_new_col)

        og_s = stack([og[rows, vv[h]] for h in range(H)])
        nw_s = stack([jnp.broadcast_to(nw_ref[:, vv[h]], (TS, DV)) for h in range(H)])
        out = _head_out(hh, nw_s, og_s)
        for h in range(H):
            carry = jnp.exp(b_last[h] + m_prev[h] - m_new[h])
            c_ref[bb, h] = carry * C0[h] + _dg(mxu(kd[hrows[h]]), v_h[h], _TN)
            n_ref[bb, h:h + 1, :] = carry * n0[h] + jnp.sum(kd[hrows[h]], axis=0, keepdims=True)
            m_ref[0, bb:bb + 1, h:h + 1] = m_new[h]
            h_ref[rows, vv[h]] = out[hrows[h]].astype(BF16)


def _mlstm_sample(qkv, og, g, gt, norm_w, C0, n0, m0, NP, TS):
    Bs, H, DK, DV = C0.shape
    BS = SAMPLE_BS
    assert Bs % BS == 0 and NP % (BS * TS) == 0
    R = BS * TS
    off = NP // R
    row = lambda i: (off + i, 0)
    gt_s = gt[:, NP:].reshape(2 * H, Bs // BS, R).transpose(1, 0, 2)
    m0_s = m0.reshape(Bs // BS, BS, H)
    outs = pl.pallas_call(
        functools.partial(_mlstm_sample_body, H=H, DK=DK, DV=DV, TS=TS),
        grid=(Bs // BS,),
        in_specs=[pl.BlockSpec((R, H * DK), row), pl.BlockSpec((R, H * DK), lambda i: (off + i, 1)),
                  pl.BlockSpec((R, H * DV), lambda i: (off + i, 2 * DK // DV)), pl.BlockSpec((R, H * DV), row),
                  pl.BlockSpec((R, 2 * H), row),
                  pl.BlockSpec((1, 2 * H, R), lambda i: (i, 0, 0)),
                  pl.BlockSpec((1, H * DV), lambda i: (0, 0)),
                  pl.BlockSpec((BS, H, DK, DV), lambda i: (i, 0, 0, 0)),
                  pl.BlockSpec((BS, H, DK), lambda i: (i, 0, 0)),
                  pl.BlockSpec((1, BS, H), lambda i: (i, 0, 0))],
        out_specs=[pl.BlockSpec((R, H * DV), lambda i: (i, 0)),
                   pl.BlockSpec((BS, H, DK, DV), lambda i: (i, 0, 0, 0)),
                   pl.BlockSpec((BS, H, DK), lambda i: (i, 0, 0)),
                   pl.BlockSpec((1, BS, H), lambda i: (i, 0, 0))],
        out_shape=[jax.ShapeDtypeStruct((Bs * TS, H * DV), BF16),
                   jax.ShapeDtypeStruct((Bs, H, DK, DV), F32),
                   jax.ShapeDtypeStruct((Bs, H, DK), F32),
                   jax.ShapeDtypeStruct((Bs // BS, BS, H), F32)],
        compiler_params=_cparams("parallel"),
        name="mlstm_sample",
    )(qkv, qkv, qkv, og, g, gt_s, norm_w, C0, n0, m0_s)
    return outs[0], outs[1], outs[2], outs[3].reshape(Bs, H)


def _layer_norm(z, gain, bias):
    mu = jnp.mean(z, axis=1, keepdims=True)
    zc = z - mu
    var = jnp.mean(zc * zc, axis=1, keepdims=True)
    return zc * lax.rsqrt(var + LN_EPS) * gain + bias


def _proj_norm_body(ap_ref, as_ref, w_ref, *refs, alpha, npb, x_split):
    prompt = pl.program_id(0) < npb
    a = jnp.where(prompt, ap_ref[...], as_ref[...])
    if x_split:
        xp_ref, xs_ref, gain_ref, bias_ref, o_ref, ob_ref = refs
        x = jnp.where(prompt, xp_ref[...], xs_ref[...])
    else:
        x_ref, gain_ref, bias_ref, o_ref, ob_ref = refs
        x = x_ref[...]
    y = _layer_norm(alpha * x + _dot(a, w_ref[...]), gain_ref[...], bias_ref[...])
    o_ref[...] = y
    ob_ref[...] = y.astype(BF16)


def _proj_norm(a_prompt, a_sample, w, x, gain, bias, alpha):
    NP, K = a_prompt.shape
    NS = a_sample.shape[0]
    N, D = NP + NS, w.shape[1]
    tm = _pick(NS, ROW_TILES)
    assert NP % tm == 0
    npb = NP // tm
    first = lambda i: (jnp.minimum(i, npb - 1), 0)
    second = lambda i: (jnp.maximum(i - npb, 0), 0)
    x_split = isinstance(x, tuple)
    x_ops = tuple(x) if x_split else (x,)
    x_specs = ([pl.BlockSpec((tm, D), first), pl.BlockSpec((tm, D), second)] if x_split
               else [pl.BlockSpec((tm, D), lambda i: (i, 0))])
    return pl.pallas_call(
        functools.partial(_proj_norm_body, alpha=alpha, npb=npb, x_split=x_split),
        grid=(N // tm,),
        in_specs=[pl.BlockSpec((tm, K), first), pl.BlockSpec((tm, K), second),
                  pl.BlockSpec((K, D), lambda i: (0, 0))] + x_specs +
                 [pl.BlockSpec((1, D), lambda i: (0, 0)),
                  pl.BlockSpec((1, D), lambda i: (0, 0))],
        out_specs=[pl.BlockSpec((tm, D), lambda i: (i, 0)), pl.BlockSpec((tm, D), lambda i: (i, 0))],
        out_shape=[jax.ShapeDtypeStruct((N, D), F32), jax.ShapeDtypeStruct((N, D), BF16)],
        compiler_params=_cparams("parallel"),
        name="proj_norm",
    )(a_prompt, a_sample, w, *x_ops, gain[None, :], bias[None, :])


def _static_loop(n, fn):
    for r in range(n):
        fn(r)


def _rows_loop(n, fn, unroll=8):
    def body(r, c):
        fn(r)
        return c
    lax.fori_loop(0, n, body, 0, unroll=unroll)


def _combine_norm_body(slot_ref, slot_next_ref, gate_ref, x_ref, gain_ref, bias_ref, ys_hbm, oa_ref, ob_ref,
                       ybuf, sem, *, alpha, nblk, split_blk):
    i = pl.program_id(0)
    cur = lax.rem(i, 2)
    tm, D = x_ref.shape

    def row_copy(idx_ref, k, r, s):
        return pltpu.make_async_copy(ys_hbm.at[pl.ds(idx_ref[k, r], 1)], ybuf.at[s, k, pl.ds(r, 1)], sem.at[s])

    def start_block(idx_ref, s):
        for k in range(2):
            _rows_loop(tm, lambda r: row_copy(idx_ref, k, r, s).start())

    @pl.when(i == 0)
    def _():
        start_block(slot_ref, 0)

    @pl.when(i + 1 < nblk)
    def _():
        start_block(slot_next_ref, 1 - cur)

    for k in range(2):
        _static_loop(tm, lambda r: row_copy(slot_ref, k, r, cur).wait())
    g = gate_ref[...]
    moe = g[:, 0:1] * ybuf[cur, 0].reshape(tm, D) + g[:, 1:2] * ybuf[cur, 1].reshape(tm, D)
    y = _layer_norm(alpha * x_ref[...] + moe, gain_ref[...], bias_ref[...])
    if split_blk is None:
        oa_ref[...] = y
        ob_ref[...] = y.astype(BF16)
    else:
        @pl.when(i < split_blk)
        def _():
            oa_ref[...] = y

        @pl.when(i >= split_blk)
        def _():
            ob_ref[...] = y


def _combine_norm(x, ys, slot, gate_col, gain, bias, alpha, split_rows=None):
    N, D = x.shape
    tm = _pick(N if split_rows is None else math.gcd(split_rows, N - split_rows), (256, 128))
    nblk = N // tm
    idx_spec = lambda off: pl.BlockSpec((2, tm), lambda i: (0, jnp.minimum(i + off, nblk - 1)),
                                        memory_space=pltpu.SMEM)
    if split_rows is None:
        split_blk = None
        out_specs = [pl.BlockSpec((tm, D), lambda i: (i, 0)), pl.BlockSpec((tm, D), lambda i: (i, 0))]
        out_shape = [jax.ShapeDtypeStruct((N, D), F32), jax.ShapeDtypeStruct((N, D), BF16)]
    else:
        split_blk = split_rows // tm
        out_specs = [pl.BlockSpec((tm, D), lambda i: (jnp.minimum(i, split_blk - 1), 0)),
                     pl.BlockSpec((tm, D), lambda i: (jnp.maximum(i - split_blk, 0), 0))]
        out_shape = [jax.ShapeDtypeStruct((split_rows, D), F32), jax.ShapeDtypeStruct((N - split_rows, D), F32)]
    return pl.pallas_call(
        functools.partial(_combine_norm_body, alpha=alpha, nblk=nblk, split_blk=split_blk),
        grid=(nblk,),
        in_specs=[idx_spec(0), idx_spec(1),
                  pl.BlockSpec((tm, gate_col.shape[1]), lambda i: (i, 0)),
                  pl.BlockSpec((tm, D), lambda i: (i, 0)),
                  pl.BlockSpec((1, D), lambda i: (0, 0)),
                  pl.BlockSpec((1, D), lambda i: (0, 0)),
                  pl.BlockSpec(memory_space=pl.ANY)],
        out_specs=out_specs,
        out_shape=out_shape,
        scratch_shapes=[pltpu.VMEM((2, 2, tm) + ys.shape[1:], F32), pltpu.SemaphoreType.DMA((2,))],
        compiler_params=_cparams("arbitrary"),
        name="combine_norm",
    )(slot, slot, gate_col, x, gain[None, :], bias[None, :], ys)


def _router_body(x_ref, wt_ref, b_ref, eid_ref, gate_ref, *, E):
    logits = _dot3(wt_ref[...], x_ref[...], _NT)
    ex = jnp.exp(logits - jnp.max(logits, axis=0, keepdims=True))
    probs = ex / jnp.sum(ex, axis=0, keepdims=True)
    sel = probs + b_ref[...]
    per = E // N_GROUPS
    s = [sel[e:e + 1, :] for e in range(E)]
    p = [probs[e:e + 1, :] for e in range(E)]

    def top2_sum(vals):
        best = None
        for a in range(len(vals)):
            for b in range(a + 1, len(vals)):
                pair = vals[a] + vals[b]
                best = pair if best is None else jnp.maximum(best, pair)
        return best

    g_best = top2_sum(s[:per])
    g_idx = jnp.zeros_like(g_best, dtype=jnp.int32)
    for gi in range(1, N_GROUPS):
        score = top2_sum(s[gi * per:(gi + 1) * per])
        better = score > g_best
        g_idx = jnp.where(better, gi, g_idx)
        g_best = jnp.where(better, score, g_best)

    def in_group(vals, j):
        out = vals[j]
        for gi in range(1, N_GROUPS):
            out = jnp.where(g_idx == gi, vals[gi * per + j], out)
        return out

    sg = [in_group(s, j) for j in range(per)]
    pg = [in_group(p, j) for j in range(per)]
    i1 = jnp.zeros_like(g_idx)
    b1 = sg[0]
    for j in range(1, per):
        better = sg[j] > b1
        i1 = jnp.where(better, j, i1)
        b1 = jnp.where(better, sg[j], b1)
    i2 = jnp.zeros_like(g_idx)
    b2 = jnp.full_like(b1, -jnp.inf)
    for j in range(per):
        better = (i1 != j) & (sg[j] > b2)
        i2 = jnp.where(better, j, i2)
        b2 = jnp.where(better, sg[j], b2)
    p1 = pg[0]
    p2 = pg[0]
    for j in range(1, per):
        p1 = jnp.where(i1 == j, pg[j], p1)
        p2 = jnp.where(i2 == j, pg[j], p2)
    tot = p1 + p2
    eid_ref[...] = jnp.concatenate([g_idx * per + i1, g_idx * per + i2], axis=0)
    gates = jnp.concatenate([p1 / tot, p2 / tot, jnp.zeros((GATE_COLS - 2, p1.shape[1]), F32)], axis=0)
    gate_ref[...] = gates.T


def _router(x, w_router, b_router):
    N, D = x.shape
    E = w_router.shape[1]
    assert E % N_GROUPS == 0 and E // N_GROUPS >= 2
    tm = _pick(N, ROW_TILES)
    return pl.pallas_call(
        functools.partial(_router_body, E=E),
        grid=(N // tm,),
        in_specs=[pl.BlockSpec((tm, D), lambda i: (i, 0)),
                  pl.BlockSpec((E, D), lambda i: (0, 0)),
                  pl.BlockSpec((E, 1), lambda i: (0, 0))],
        out_specs=[pl.BlockSpec((2, tm), lambda i: (0, i)), pl.BlockSpec((tm, GATE_COLS), lambda i: (i, 0))],
        out_shape=[jax.ShapeDtypeStruct((2, N), jnp.int32), jax.ShapeDtypeStruct((N, GATE_COLS), F32)],
        compiler_params=_cparams("parallel"),
        name="router",
    )(x, w_router.T, b_router.astype(F32)[:, None])


def _plan_body(e_ref, slot_ref, cnt_ref, *, E, TM):
    e = e_ref[...]
    R = e.shape[0]
    upper = (lax.broadcasted_iota(jnp.int32, (LANES, LANES), 0)
             <= lax.broadcasted_iota(jnp.int32, (LANES, LANES), 1)).astype(BF16)
    below = (lax.broadcasted_iota(jnp.int32, (R, R), 1) < lax.broadcasted_iota(jnp.int32, (R, R), 0)).astype(BF16)
    lane = lax.broadcasted_iota(jnp.int32, (1, LANES), 1)
    slot = jnp.zeros(e.shape, F32)
    counts = jnp.zeros((1, LANES), F32)
    start = jnp.zeros((1, 1), F32)
    for ex in range(E):
        hit = e == ex
        within = _dot(hit.astype(BF16), upper)
        rows_before = _dot(below, within.astype(BF16))[:, LANES - 1:LANES]
        rank = within - 1.0 + rows_before
        count = rows_before[R - 1:R, :] + within[R - 1:R, LANES - 1:LANES]
        slot = jnp.where(hit, start + rank, slot)
        counts = jnp.where(lane == ex, count, counts)
        start = start + jnp.floor((count + (TM - 1)) * (1.0 / TM)) * TM
    slot_ref[...] = slot.astype(jnp.int32)
    cnt_ref[...] = jnp.broadcast_to(counts, cnt_ref.shape).astype(jnp.int32)


def _plan(eid, E, TM):
    N = eid.shape[1]
    A = 2 * N
    assert A % LANES == 0 and (TM & (TM - 1)) == 0
    R = A // LANES
    Rp = -(-R // LANES) * LANES
    e2d = jnp.pad(eid.reshape(R, LANES), ((0, Rp - R), (0, 0)), constant_values=-1)
    slot2d, cnt = pl.pallas_call(
        functools.partial(_plan_body, E=E, TM=TM),
        out_shape=[jax.ShapeDtypeStruct((Rp, LANES), jnp.int32), jax.ShapeDtypeStruct((8, LANES), jnp.int32)],
        compiler_params=pltpu.CompilerParams(vmem_limit_bytes=VMEM_LIMIT_BYTES),
        name="route_plan",
    )(e2d)
    return slot2d[:R].reshape(2, N), cnt[0, :E]


def _group_table(counts, TM, n_blocks):
    E = counts.shape[0]
    GB = MOE_GROUP_BLOCKS
    nblk = (counts + TM - 1) // TM
    blk_end = jnp.cumsum(nblk)
    blk_start = blk_end - nblk
    ngrp = (nblk + GB - 1) // GB
    grp_end = jnp.cumsum(ngrp)
    n_groups_max = E + n_blocks // GB
    g = jnp.arange(n_groups_max, dtype=jnp.int32)
    g_expert = jnp.minimum(jnp.sum((grp_end[None, :] <= g[:, None]).astype(jnp.int32), axis=1), E - 1)
    j = g - (grp_end - ngrp)[g_expert]
    g_start = blk_start[g_expert] + j * GB
    g_nblk = jnp.clip(nblk[g_expert] - j * GB, 0, GB)
    live = g < grp_end[-1]
    g_nblk = jnp.where(live, g_nblk, 0)
    meta = jnp.stack([grp_end[-1], blk_end[-1]]).astype(jnp.int32)
    pad_start = blk_start * TM + counts
    pad_count = nblk * TM - counts
    return (g_expert.astype(jnp.int32), g_start.astype(jnp.int32), g_nblk.astype(jnp.int32), meta,
            pad_start.astype(jnp.int32), pad_count.astype(jnp.int32))


def _pack_halves(x):
    half = x.shape[1] // 2
    bits = lax.bitcast_convert_type(x.astype(BF16).astype(F32), jnp.uint32)
    return (bits[:, :half] >> 16) | bits[:, half:]


def _unpack_halves(p):
    lo = lax.bitcast_convert_type(p << 16, F32).astype(BF16)
    hi = lax.bitcast_convert_type(p & jnp.uint32(0xFFFF0000), F32).astype(BF16)
    return lo, hi


def _dispatch_body(pad_start_ref, pad_count_ref, meta_ref, slot_ref, x_ref, xs_hbm, pbuf, zbuf, sem, zsem,
                   *, nblk, E, TM, n_blocks):
    i = pl.program_id(0)
    cur = lax.rem(i, 2)
    tm = x_ref.shape[0]

    def row_copy(k, r, s):
        return pltpu.make_async_copy(pbuf.at[s, pl.ds(r, 1)], xs_hbm.at[pl.ds(slot_ref[k, r], 1)], sem.at[s])

    def wait_block(s):
        for k in range(2):
            _static_loop(tm, lambda r: row_copy(k, r, s).wait())

    def zero_copy(row):
        return pltpu.make_async_copy(zbuf.at[pl.ds(0, 1)], xs_hbm.at[pl.ds(row, 1)], zsem.at[0])

    @pl.when(i == 0)
    def _():
        zbuf[...] = jnp.zeros_like(zbuf)
        for ex in range(E):
            _rows_loop(pad_count_ref[ex], lambda j: zero_copy(pad_start_ref[ex] + j).start(), unroll=1)
        tail0 = meta_ref[1] * TM
        ntail = n_blocks * TM - tail0
        _rows_loop(ntail, lambda j: zero_copy(tail0 + j).start(), unroll=1)
        for ex in range(E):
            _rows_loop(pad_count_ref[ex], lambda j: zero_copy(pad_start_ref[ex] + j).wait(), unroll=1)
        _rows_loop(ntail, lambda j: zero_copy(tail0 + j).wait(), unroll=1)

    @pl.when(i >= 2)
    def _():
        wait_block(cur)

    pbuf[cur] = _pack_halves(x_ref[...]).reshape(pbuf.shape[1:])
    for k in range(2):
        _rows_loop(tm, lambda r: row_copy(k, r, cur).start())

    @pl.when(i == nblk - 1)
    def _():
        wait_block(cur)
        if nblk >= 2:
            wait_block(1 - cur)


def _dispatch(x, slot, pad_start, pad_count, meta, TM, n_blocks):
    N, D = x.shape
    E = pad_start.shape[0]
    tm = _pick(N, (256, 128))
    nblk = N // tm
    row_tile = (D // 2 // LANES, LANES)
    grid_spec = pltpu.PrefetchScalarGridSpec(
        num_scalar_prefetch=3,
        grid=(nblk,),
        in_specs=[pl.BlockSpec((2, tm), lambda i, *_: (0, i), memory_space=pltpu.SMEM),
                  pl.BlockSpec((tm, D), lambda i, *_: (i, 0))],
        out_specs=pl.BlockSpec(memory_space=pl.ANY),
        scratch_shapes=[pltpu.VMEM((2, tm) + row_tile, jnp.uint32), pltpu.VMEM((1,) + row_tile, jnp.uint32),
                        pltpu.SemaphoreType.DMA((2,)), pltpu.SemaphoreType.DMA((1,))],
    )
    return pl.pallas_call(
        functools.partial(_dispatch_body, nblk=nblk, E=E, TM=TM, n_blocks=n_blocks),
        grid_spec=grid_spec,
        out_shape=jax.ShapeDtypeStruct((n_blocks * TM,) + row_tile, jnp.uint32),
        compiler_params=_cparams("arbitrary"),
        name="moe_dispatch",
    )(pad_start, pad_count, meta, slot, x)


def _experts_body(ge_ref, gs_ref, gn_ref, meta_ref, xs_hbm, wg_hbm, wu_hbm, wd_hbm, ys_hbm,
                  xraw, xg, acc, stage_g, stage_u, stage_d, wgu, wdb, obuf, xsem, wsem, osem,
                  *, layer, chunks, TM, n_blocks):
    g = pl.program_id(0)
    n_groups = meta_ref[0]
    half = xg.shape[2]
    n_chunks = len(chunks)

    def x_copy(b0, rb):
        return pltpu.make_async_copy(xs_hbm.at[pl.ds((b0 + rb) * TM, TM)], xraw.at[rb], xsem.at[0])

    def w_copies(ex, c, s):
        f0, fw = chunks[c]
        return (pltpu.make_async_copy(wg_hbm.at[layer, ex, :, pl.ds(f0, fw)], stage_g.at[s, :, pl.ds(0, fw)],
                                      wsem.at[s, 0]),
                pltpu.make_async_copy(wu_hbm.at[layer, ex, :, pl.ds(f0, fw)], stage_u.at[s, :, pl.ds(0, fw)],
                                      wsem.at[s, 1]),
                pltpu.make_async_copy(wd_hbm.at[layer, ex, pl.ds(f0, fw), :], stage_d.at[s, pl.ds(0, fw), :],
                                      wsem.at[s, 2]))

    def out_copy(b0, rb, s):
        return pltpu.make_async_copy(obuf.at[s], ys_hbm.at[pl.ds((b0 + rb) * TM, TM)], osem.at[s])

    @pl.when(g < n_groups)
    def _():
        ex = ge_ref[g]
        b0 = gs_ref[g]
        nb = gn_ref[g]
        _rows_loop(nb, lambda rb: x_copy(b0, rb).start(), unroll=1)

        def start_weights(expert, c, s):
            for idx, cp in enumerate(w_copies(expert, c, s)):
                cp.start(priority=idx % 2)

        @pl.when(g == 0)
        def _():
            start_weights(ex, 0, 0)

        _rows_loop(nb, lambda rb: x_copy(b0, rb).wait(), unroll=1)

        def untile(rb):
            xg[rb] = xraw[rb].reshape(TM, half)

        _rows_loop(nb, untile, unroll=1)

        for c, (f0, fw) in enumerate(chunks):
            s = c % 2
            for cp in w_copies(ex, c, s):
                cp.wait()
            if c + 1 < n_chunks:
                start_weights(ex, c + 1, 1 - s)
            else:
                @pl.when(g + 1 < n_groups)
                def _():
                    start_weights(ge_ref[g + 1], 0, 1 - s)
            wgu[:, 0:fw] = stage_g[s, :, 0:fw].astype(BF16)
            wgu[:, fw:2 * fw] = stage_u[s, :, 0:fw].astype(BF16)
            wdb[0:fw, :] = stage_d[s, 0:fw, :].astype(BF16)

            def block(rb, carry, c=c, fw=fw):
                last = c + 1 == n_chunks
                buf = lax.rem(rb, 2)
                if last:
                    @pl.when(rb >= 2)
                    def _():
                        out_copy(b0, rb, buf).wait()
                xl, xh = _unpack_halves(xg[rb])
                gu = _dot(xl, wgu[0:half, 0:2 * fw]) + _dot(xh, wgu[half:2 * half, 0:2 * fw])
                hidden = jax.nn.silu(gu[:, 0:fw]) * gu[:, fw:2 * fw]
                part = _dot(hidden.astype(BF16), wdb[0:fw, :])
                if c == 0:
                    acc[rb] = part
                elif not last:
                    acc[rb] = acc[rb] + part
                else:
                    obuf[buf] = (acc[rb] + part).reshape(obuf.shape[1:])
                    out_copy(b0, rb, buf).start()
                return carry

            lax.fori_loop(0, nb, block, 0)

        out_copy(b0, 0, lax.rem(nb - 1, 2)).wait()

        @pl.when(nb >= 2)
        def _():
            out_copy(b0, 0, lax.rem(nb, 2)).wait()

    @pl.when(g == pl.num_programs(0) - 1)
    def _():
        used = meta_ref[1]
        obuf[0] = jnp.zeros(obuf.shape[1:], F32)
        _rows_loop(n_blocks - used, lambda j: out_copy(used, j, 0).start(), unroll=1)
        _rows_loop(n_blocks - used, lambda j: out_copy(used, j, 0).wait(), unroll=1)


def _experts(xs, g_expert, g_start, g_nblk, meta, w_gate, w_up, w_down, layer, TM, n_blocks):
    _, E, D, F = w_gate.shape
    GB = MOE_GROUP_BLOCKS
    FC = MOE_FC
    chunks = tuple((f0, min(FC, F - f0)) for f0 in range(0, F, FC))
    assert len(chunks) % 2 == 0 and all(fw % LANES == 0 for _, fw in chunks)
    out_tile = (D // LANES, LANES)
    grid_spec = pltpu.PrefetchScalarGridSpec(
        num_scalar_prefetch=4,
        grid=(g_expert.shape[0],),
        in_specs=[pl.BlockSpec(memory_space=pl.ANY)] * 4,
        out_specs=pl.BlockSpec(memory_space=pl.ANY),
        scratch_shapes=[pltpu.VMEM((GB, TM) + xs.shape[1:], jnp.uint32),
                        pltpu.VMEM((GB, TM, D // 2), jnp.uint32), pltpu.VMEM((GB, TM, D), F32),
                        pltpu.VMEM((2, D, FC), F32), pltpu.VMEM((2, D, FC), F32), pltpu.VMEM((2, FC, D), F32),
                        pltpu.VMEM((D, 2 * FC), BF16), pltpu.VMEM((FC, D), BF16),
                        pltpu.VMEM((2, TM) + out_tile, F32),
                        pltpu.SemaphoreType.DMA((1,)), pltpu.SemaphoreType.DMA((2, 3)),
                        pltpu.SemaphoreType.DMA((2,))],
    )
    return pl.pallas_call(
        functools.partial(_experts_body, layer=layer, chunks=chunks, TM=TM, n_blocks=n_blocks),
        grid_spec=grid_spec,
        out_shape=jax.ShapeDtypeStruct((n_blocks * TM,) + out_tile, F32),
        compiler_params=_cparams("arbitrary"),
        name="moe_experts",
    )(g_expert, g_start, g_nblk, meta, xs, w_gate, w_up, w_down)


def _moe_layer(x, w_router, b_router, w_gate, w_up, w_down, layer, gain, bias, alpha, split_rows=None):
    N = x.shape[0]
    E = w_router.shape[1]
    TM = MOE_TM
    n_blocks = -(-(2 * N + E * (TM - 1)) // TM)
    eid, gate_col = _router(x, w_router, b_router)
    slot, counts = _plan(eid, E, TM)
    g_expert, g_start, g_nblk, meta, pad_start, pad_count = _group_table(counts, TM, n_blocks)
    xs = _dispatch(x, slot, pad_start, pad_count, meta, TM, n_blocks)
    ys = _experts(xs, g_expert, g_start, g_nblk, meta, w_gate, w_up, w_down, layer, TM, n_blocks)
    return _combine_norm(x, ys, slot, gate_col, gain, bias, alpha, split_rows)


def _pair_attention(q_pairs, kc, vc, bias, sinks_lo, sinks_hi):
    P = len(q_pairs)
    R, lanes = q_pairs[0].shape
    lo = lax.broadcasted_iota(jnp.int32, (R, lanes), 1) < lanes // 2
    zero = jnp.zeros((R, lanes), BF16)
    qs = jnp.concatenate([jnp.where(lo, qp, zero) for qp in q_pairs]
                         + [jnp.where(lo, zero, qp) for qp in q_pairs], axis=0)
    sink_col = jnp.concatenate([jnp.full((R, 1), s, F32) for s in list(sinks_lo) + list(sinks_hi)], axis=0)
    s = _dg(qs, kc, _NT) + bias
    mx = jnp.maximum(jnp.max(s, axis=1, keepdims=True), sink_col)
    e = jnp.exp(s - mx)
    den = jnp.sum(e, axis=1, keepdims=True) + jnp.exp(sink_col - mx)
    o = _dot(e.astype(BF16), vc) * (1.0 / den)
    return [jnp.where(lo, o[p * R:(p + 1) * R], o[(P + p) * R:(P + p + 1) * R]) for p in range(P)]


def _swa_prompt_body(sink_ref, q_ref, kp_ref, kc_ref, vp_ref, vc_ref, o_ref, *, KVH, G):
    W = q_ref.shape[0]
    P = G // 2
    first = pl.program_id(1) == 0
    r = lax.broadcasted_iota(jnp.int32, (W, 2 * W), 0)
    kj = lax.broadcasted_iota(jnp.int32, (W, 2 * W), 1)
    diff = r + W - kj
    valid = (diff >= 0) & (diff < W) & (jnp.logical_not(first) | (kj >= W))
    bias = jnp.concatenate([jnp.where(valid, 0.0, -jnp.inf)] * G, axis=0)
    for g in range(KVH):
        ks = slice(g * LANES, (g + 1) * LANES)
        kc = jnp.concatenate([kp_ref[:, ks], kc_ref[:, ks]], axis=0)
        vc = jnp.concatenate([vp_ref[:, ks], vc_ref[:, ks]], axis=0)
        cols = [slice((g * P + p) * LANES, (g * P + p + 1) * LANES) for p in range(P)]
        outs = _pair_attention([q_ref[:, c] for c in cols], kc, vc, bias,
                               [sink_ref[g * G + 2 * p] for p in range(P)],
                               [sink_ref[g * G + 2 * p + 1] for p in range(P)])
        for c, o in zip(cols, outs):
            o_ref[:, c] = o.astype(BF16)


def _swa_prompt(q, kdup, vdup, sinks, B, T, KVH):
    Hq = sinks.shape[0]
    W = WINDOW
    assert T % W == 0 and kdup.shape[1] == KVH * LANES and q.shape[1] == Hq // 2 * LANES
    nb = T // W
    cur = lambda b, n: (b * nb + n, 0)
    prev = lambda b, n: (b * nb + jnp.maximum(n - 1, 0), 0)
    return pl.pallas_call(
        functools.partial(_swa_prompt_body, KVH=KVH, G=Hq // KVH),
        grid=(B, nb),
        in_specs=[pl.BlockSpec(memory_space=pltpu.SMEM),
                  pl.BlockSpec((W, q.shape[1]), cur),
                  pl.BlockSpec((W, KVH * LANES), prev), pl.BlockSpec((W, KVH * LANES), cur),
                  pl.BlockSpec((W, KVH * LANES), prev), pl.BlockSpec((W, KVH * LANES), cur)],
        out_specs=pl.BlockSpec((W, q.shape[1]), cur),
        out_shape=jax.ShapeDtypeStruct((B * T, q.shape[1]), BF16),
        compiler_params=_cparams("parallel", "parallel"),
        name="swa_prompt",
    )(sinks, q, kdup, kdup, vdup, vdup)


def _swa_sample_body(sink_ref, q_ref, k_ref, v_ref, o_ref, *, KVH, G, TS, WB):
    K = k_ref.shape[1]
    P = G // 2
    NPAIR = KVH * P
    q = q_ref[...].astype(F32)
    r = lax.broadcasted_iota(jnp.int32, (2 * NPAIR * TS, KVH * K), 0)
    c = lax.broadcasted_iota(jnp.int32, (2 * NPAIR * TS, KVH * K), 1)
    diff = WB + r % TS - c % K
    own = (r % (NPAIR * TS)) // (P * TS) == c // K
    bias = jnp.where(own & (diff >= 0) & (diff < WINDOW), 0.0, -jnp.inf)
    cols = [slice(p * LANES, (p + 1) * LANES) for p in range(NPAIR)]
    sinks_lo = [sink_ref[2 * p] for p in range(NPAIR)]
    sinks_hi = [sink_ref[2 * p + 1] for p in range(NPAIR)]
    for bb in range(SAMPLE_BS):
        rows = slice(bb * TS, (bb + 1) * TS)
        kc = jnp.concatenate([k_ref[bb, :, g * LANES:(g + 1) * LANES] for g in range(KVH)], axis=0)
        vc = jnp.concatenate([v_ref[bb, :, g * LANES:(g + 1) * LANES] for g in range(KVH)], axis=0)
        outs = _pair_attention([q[rows, cc].astype(BF16) for cc in cols], kc, vc, bias, sinks_lo, sinks_hi)
        for cc, o in zip(cols, outs):
            o_ref[rows, cc] = o.astype(BF16)


def _swa_sample(q, k_all, v_all, sinks, NP, TS, WB, KVH):
    Bs, K, _ = k_all.shape
    Hq = sinks.shape[0]
    BS = SAMPLE_BS
    R = BS * TS
    off = NP // R
    row = lambda i: (off + i, 0)
    return pl.pallas_call(
        functools.partial(_swa_sample_body, KVH=KVH, G=Hq // KVH, TS=TS, WB=WB),
        grid=(Bs // BS,),
        in_specs=[pl.BlockSpec(memory_space=pltpu.SMEM),
                  pl.BlockSpec((R, q.shape[1]), row),
                  pl.BlockSpec((BS, K, KVH * LANES), lambda i: (i, 0, 0)),
                  pl.BlockSpec((BS, K, KVH * LANES), lambda i: (i, 0, 0))],
        out_specs=pl.BlockSpec((R, q.shape[1]), lambda i: (i, 0)),
        out_shape=jax.ShapeDtypeStruct((Bs * TS, q.shape[1]), BF16),
        compiler_params=_cparams("parallel"),
        name="swa_sample",
    )(sinks, q, k_all, v_all)


def _rope_tables(pos, head_dim, lanes):
    half = head_dim // 2
    inv = ROPE_THETA ** (-jnp.arange(half, dtype=F32) / half)
    ang = pos.astype(F32)[:, None] * inv[None, :]
    reps = lanes // half
    return jnp.tile(jnp.cos(ang), (1, reps)), jnp.tile(jnp.sin(ang), (1, reps))


def kernel(x_prompt, x_sample, state_C, state_n, state_m, cache_k, cache_v, w_in_a, b_i_a, b_f_a, norm_a, w_out_a,
           w_kv, w_q_b, sinks_b, w_out_b, w_router, b_router, w_gate_e, w_up_e, w_down_e, ln_g, ln_b):
    B, T, D = x_prompt.shape
    Bs, TS, _ = x_sample.shape
    depth = ln_g.shape[0]
    assert depth == 2 and w_in_a.shape[0] == 1 and w_q_b.shape[0] == 1
    alpha = (2 * depth) ** 0.25
    NP, NS = B * T, Bs * TS
    H = b_i_a.shape[1]
    DK, DV = state_C.shape[3], state_C.shape[4]
    WB, KVH, HD = cache_k.shape[1], cache_k.shape[2], cache_k.shape[3]
    E = w_router.shape[1]

    x0_p = x_prompt.reshape(NP, D).astype(F32)
    x0_s = x_sample.reshape(NS, D).astype(F32)

    w_in = w_in_a[0].astype(F32)
    nqk, nv = H * DK, H * DV
    w_gates = w_in[:, 2 * nqk + nv + D:]
    g, gt, x0b = _mlstm_gates(x0_p, x0_s, w_gates, b_i_a[0], b_f_a[0])
    qkv_scale = jnp.concatenate([jnp.ones((nqk,), F32), jnp.full((nqk,), DK ** -0.5, F32), jnp.ones((nv,), F32)])
    qkv = _matmul(x0b, w_in, qkv_scale[None, :], BF16, col0=0, ncols=2 * nqk + nv)
    og = _matmul(x0b, w_in, jnp.ones((1, D), F32), F32, col0=2 * nqk + nv, ncols=D)
    norm_w = norm_a[0].astype(F32)[None, :]
    h_p, C_p, n_p, m_p = _mlstm_prompt(qkv, og, g, gt, norm_w, B, T, H, DK, DV)
    h_s, C_s, n_s, m_s = _mlstm_sample(qkv, og, g, gt, norm_w, state_C[0].astype(F32), state_n[0].astype(F32),
                                       state_m[0].astype(F32), NP, TS)
    x1, _ = _proj_norm(h_p, h_s, w_out_a[0].astype(BF16), (x0_p, x0_s), ln_g[0, 0], ln_b[0, 0], alpha)
    x2, x2b = _moe_layer(x1, w_router, b_router, w_gate_e, w_up_e, w_down_e, 0, ln_g[0, 1], ln_b[0, 1], alpha)

    pos = jnp.concatenate([jnp.tile(jnp.arange(T, dtype=jnp.int32), B),
                           jnp.tile(PAST_LEN + jnp.arange(TS, dtype=jnp.int32), Bs)])
    cos, sin = _rope_tables(pos, HD, 128)
    nkv = KVH * HD
    k_new = _matmul_rope(x2b, w_kv[:, :nkv].astype(BF16), cos, sin, HD, F32)
    v_new = _matmul(x2b, w_kv[:, nkv:].astype(BF16), jnp.ones((1, nkv), F32), F32)
    assert 2 * HD == LANES
    qb = _matmul_rope(x2b, w_q_b[0].astype(BF16), cos, sin, HD, BF16, scale=HD ** -0.5)

    def dup_heads(a):
        a4 = a.reshape(a.shape[:-1] + (KVH, HD)).astype(BF16)
        return jnp.concatenate([a4, a4], axis=-1).reshape(a.shape[:-1] + (KVH * LANES,))

    o_p = _swa_prompt(qb, dup_heads(k_new), dup_heads(v_new), sinks_b[0].astype(F32), B, T, KVH)
    k_s_new = k_new[NP:].reshape(Bs, TS, nkv)
    v_s_new = v_new[NP:].reshape(Bs, TS, nkv)
    k_cat = jnp.concatenate([cache_k.reshape(Bs, WB, nkv).astype(F32), k_s_new], axis=1)
    v_cat = jnp.concatenate([cache_v.reshape(Bs, WB, nkv).astype(F32), v_s_new], axis=1)
    kpad = -(-(WB + TS) // LANES) * LANES - (WB + TS)
    k_all = dup_heads(jnp.pad(k_cat, ((0, 0), (0, kpad), (0, 0))))
    v_all = dup_heads(jnp.pad(v_cat, ((0, 0), (0, kpad), (0, 0))))
    o_s = _swa_sample(qb, k_all, v_all, sinks_b[0].astype(F32), NP, TS, WB, KVH)
    x3, _ = _proj_norm(o_p, o_s, w_out_b[0].astype(BF16), x2, ln_g[1, 0], ln_b[1, 0], alpha)
    y_p, y_s = _moe_layer(x3, w_router, b_router, w_gate_e, w_up_e, w_down_e, 1, ln_g[1, 1], ln_b[1, 1], alpha,
                          split_rows=NP)
    y_prompt = y_p.reshape(B, T, D)
    y_sample = y_s.reshape(Bs, TS, D)
    k_p = k_new[:NP].reshape(B, T, KVH, HD)[:, -WB:]
    v_p = v_new[:NP].reshape(B, T, KVH, HD)[:, -WB:]
    k_s = k_cat[:, -WB:].reshape(Bs, WB, KVH, HD)
    v_s = v_cat[:, -WB:].reshape(Bs, WB, KVH, HD)
    return (y_prompt, y_sample, C_p[None], n_p[None], m_p.reshape(1, B, H), k_p, v_p,
            C_s[None], n_s[None], m_s[None], k_s, v_s)
```

```python
import functools
import math

import jax
import jax.numpy as jnp
from jax import lax
from jax.experimental import pallas as pl
from jax.experimental.pallas import tpu as pltpu

F32 = jnp.float32
BF16 = jnp.bfloat16

GATE_CAP = 15.0
LN_EPS = 1e-5
ROPE_THETA = 10000.0
WINDOW = 128
PAST_LEN = 8192
N_GROUPS = 4

VMEM_LIMIT_BYTES = 56 * 1024 * 1024
MLSTM_CHUNK = 256
MOE_TM = 256
MOE_GROUP_BLOCKS = 6
MOE_FC = 256
GATE_COLS = 8
LANES = 128
SAMPLE_BS = 2
ROW_TILES = (512, 256, 128)
MATMUL_ROW_TILES = (1024,) + ROW_TILES

_NT = (((1,), (1,)), ((), ()))
_TN = (((0,), (0,)), ((), ()))


def _pick(n, cands):
    for c in cands:
        if n % c == 0:
            return c
    raise ValueError(f"no tile in {cands} divides {n}")


def _cparams(*sem):
    return pltpu.CompilerParams(dimension_semantics=sem, vmem_limit_bytes=VMEM_LIMIT_BYTES)


def _dot(a, b):
    return jnp.dot(a, b, preferred_element_type=F32)


def _dg(a, b, dims):
    return lax.dot_general(a, b, dims, preferred_element_type=F32)


def _mm_scale_body(a_ref, b_ref, s_ref, o_ref):
    o_ref[...] = (_dot(a_ref[...], b_ref[...].astype(BF16)) * s_ref[...]).astype(o_ref.dtype)


def _matmul(a, b, col_scale, out_dtype, col0=0, ncols=None):
    M, K = a.shape
    N = b.shape[-1] if ncols is None else ncols
    tm = _pick(M, MATMUL_ROW_TILES)
    tn = _pick(math.gcd(N, col0) if col0 else N, (1024, 512, 256))
    j0 = col0 // tn
    return pl.pallas_call(
        _mm_scale_body,
        grid=(M // tm, N // tn),
        in_specs=[pl.BlockSpec((tm, K), lambda i, j: (i, 0)),
                  pl.BlockSpec((None, K, tn), lambda i, j: (0, 0, j0 + j)),
                  pl.BlockSpec((1, tn), lambda i, j: (0, j))],
        out_specs=pl.BlockSpec((tm, tn), lambda i, j: (i, j)),
        out_shape=jax.ShapeDtypeStruct((M, N), out_dtype),
        compiler_params=_cparams("parallel", "parallel"),
        name="matmul",
    )(a, b if b.ndim == 3 else b[None], col_scale)


def _mm_rope_body(a_ref, b_ref, cos_ref, sin_ref, o_ref, *, half, scale):
    acc = _dot(a_ref[...], b_ref[...])
    tn = acc.shape[1]
    reps = tn // cos_ref.shape[1]
    cos = jnp.concatenate([cos_ref[...]] * reps, axis=1)
    sin = jnp.concatenate([sin_ref[...]] * reps, axis=1)
    lane = lax.broadcasted_iota(jnp.int32, acc.shape, 1)
    partner = jnp.where(lane % (2 * half) < half, -pltpu.roll(acc, tn - half, 1), pltpu.roll(acc, half, 1))
    o_ref[...] = ((acc * cos + partner * sin) * scale).astype(o_ref.dtype)


def _matmul_rope(a, b, cos, sin, head_dim, out_dtype, scale=1.0):
    M, K = a.shape
    N = b.shape[1]
    tm = _pick(M, MATMUL_ROW_TILES)
    tn = _pick(N, (1024, 512, 256))
    assert math.frexp(scale)[0] == 0.5
    return pl.pallas_call(
        functools.partial(_mm_rope_body, half=head_dim // 2, scale=scale),
        grid=(M // tm, N // tn),
        in_specs=[pl.BlockSpec((tm, K), lambda i, j: (i, 0)),
                  pl.BlockSpec((K, tn), lambda i, j: (0, j)),
                  pl.BlockSpec((tm, cos.shape[1]), lambda i, j: (i, 0)),
                  pl.BlockSpec((tm, sin.shape[1]), lambda i, j: (i, 0))],
        out_specs=pl.BlockSpec((tm, tn), lambda i, j: (i, j)),
        out_shape=jax.ShapeDtypeStruct((M, N), out_dtype),
        compiler_params=_cparams("parallel", "parallel"),
        name="matmul_rope",
    )(a, b, cos, sin)


def _split_bf16(x):
    hi = x.astype(BF16)
    lo = (x - hi.astype(F32)).astype(BF16)
    return hi, lo


def _dot3(a, b, dims):
    ah, al = _split_bf16(a)
    bh, bl = _split_bf16(b)
    return _dg(ah, bh, dims) + _dg(al, bh, dims) + _dg(ah, bl, dims)


def _softcap(z):
    return GATE_CAP * jnp.tanh(z / GATE_CAP)


def _log_sigmoid(z):
    return jnp.minimum(z, 0.0) - jnp.log1p(jnp.exp(-jnp.abs(z)))


def _gates_body(xp_ref, xs_ref, w_ref, wt_ref, brow_ref, bcol_ref, g_ref, gt_ref, xb_ref, *, H, npb):
    x = jnp.where(pl.program_id(0) < npb, xp_ref[...], xs_ref[...])
    xb_ref[...] = x.astype(BF16)
    pre = _dot3(x, w_ref[...], (((1,), (0,)), ((), ()))) + brow_ref[...]
    pre_t = _dot3(wt_ref[...], x, _NT) + bcol_ref[...]
    z = _softcap(pre)
    zt = _softcap(pre_t)
    is_in = lax.broadcasted_iota(jnp.int32, z.shape, 1) < H
    is_in_t = lax.broadcasted_iota(jnp.int32, zt.shape, 0) < H
    g_ref[...] = jnp.where(is_in, z, _log_sigmoid(z))
    gt_ref[...] = jnp.where(is_in_t, zt, _log_sigmoid(zt))


def _mlstm_gates(x_prompt, x_sample, w_gates, b_i, b_f):
    NP, D = x_prompt.shape
    NS = x_sample.shape[0]
    N = NP + NS
    H = b_i.shape[0]
    tm = _pick(NS, ROW_TILES)
    assert NP % tm == 0
    npb = NP // tm
    bias = jnp.concatenate([b_i, b_f]).astype(F32)
    return pl.pallas_call(
        functools.partial(_gates_body, H=H, npb=npb),
        grid=(N // tm,),
        in_specs=[pl.BlockSpec((tm, D), lambda i: (jnp.minimum(i, npb - 1), 0)),
                  pl.BlockSpec((tm, D), lambda i: (jnp.maximum(i - npb, 0), 0)),
                  pl.BlockSpec((D, 2 * H), lambda i: (0, 0)),
                  pl.BlockSpec((2 * H, D), lambda i: (0, 0)),
                  pl.BlockSpec((1, 2 * H), lambda i: (0, 0)),
                  pl.BlockSpec((2 * H, 1), lambda i: (0, 0))],
        out_specs=[pl.BlockSpec((tm, 2 * H), lambda i: (i, 0)),
                   pl.BlockSpec((2 * H, tm), lambda i: (0, i)),
                   pl.BlockSpec((tm, D), lambda i: (i, 0))],
        out_shape=[jax.ShapeDtypeStruct((N, 2 * H), F32), jax.ShapeDtypeStruct((2 * H, N), F32),
                   jax.ShapeDtypeStruct((N, D), BF16)],
        compiler_params=_cparams("parallel"),
        name="mlstm_gates",
    )(x_prompt, x_sample, w_gates, w_gates.T, bias[None, :], bias[:, None])


def _mlstm_head(qh, kh, vh, ig_col, lf_col, ig_row, lf_row, C, n_row, m_prev):
    L = qh.shape[0]
    t_idx = lax.broadcasted_iota(jnp.int32, (L, L), 0)
    s_idx = lax.broadcasted_iota(jnp.int32, (L, L), 1)
    causal = s_idx <= t_idx

    def mxu(x):
        return x.astype(BF16)

    b_col = jnp.sum(jnp.where(causal, lf_row, 0.0), axis=1, keepdims=True)
    b_row = jnp.sum(jnp.where(t_idx <= s_idx, lf_col, 0.0), axis=0, keepdims=True)
    dmat = jnp.where(causal, b_col - b_row + ig_row, -jnp.inf)
    a_col = b_col + m_prev
    m_t = jnp.maximum(a_col, jnp.max(dmat, axis=1, keepdims=True))
    wmat = jnp.exp(dmat - m_t) * _dg(qh, kh, _NT)
    inter = jnp.exp(a_col - m_t)
    qf = qh.astype(F32)
    kf = kh.astype(F32)
    num = inter * _dot(qh, mxu(C)) + _dot(mxu(wmat), vh)
    den = inter * jnp.sum(qf * n_row, axis=1, keepdims=True) + jnp.sum(wmat, axis=1, keepdims=True)
    hh = num * (1.0 / jnp.maximum(jnp.abs(den), jnp.exp(-m_t)))
    m_new = m_t[L - 1:L, :]
    b_last = b_col[L - 1:L, :]
    decay = jnp.exp(b_last - b_col + ig_col - m_new)
    carry = jnp.exp(b_last + m_prev - m_new)
    kd = kf * decay
    C_new = carry * C + _dg(mxu(kd), vh, _TN)
    n_new = carry * n_row + jnp.sum(kd, axis=0, keepdims=True)
    return hh, C_new, n_new, m_new


def _head_out(hh, nw, og):
    hn = hh * lax.rsqrt(jnp.mean(hh * hh, axis=1, keepdims=True) + LN_EPS)
    return hn * nw * jax.nn.sigmoid(og)


def _mlstm_prompt_body(q_ref, k_ref, v_ref, og_ref, g_ref, gt_ref, nw_ref,
                       h_ref, c_ref, n_ref, m_ref, *, H, DK, DV):
    @pl.when(pl.program_id(1) == 0)
    def _():
        c_ref[...] = jnp.zeros_like(c_ref)
        n_ref[...] = jnp.zeros_like(n_ref)
        m_ref[...] = jnp.zeros_like(m_ref)

    g = g_ref[...]
    gt = gt_ref[...]
    for h in range(H):
        kq = slice(h * DK, (h + 1) * DK)
        vv = slice(h * DV, (h + 1) * DV)
        hh, C_new, n_new, m_new = _mlstm_head(
            q_ref[:, kq], k_ref[:, kq], v_ref[:, vv],
            g[:, h:h + 1], g[:, H + h:H + h + 1], gt[h:h + 1, :], gt[H + h:H + h + 1, :],
            c_ref[0, h], n_ref[0, h:h + 1, :], m_ref[0, :, h:h + 1])
        c_ref[0, h] = C_new
        n_ref[0, h:h + 1, :] = n_new
        m_ref[0, :, h:h + 1] = m_new
        h_ref[:, vv] = _head_out(hh, nw_ref[:, vv], og_ref[:, vv]).astype(BF16)


def _mlstm_prompt(qkv, og, g, gt, norm_w, B, T, H, DK, DV):
    L = MLSTM_CHUNK
    assert T % L == 0 and (2 * H * DK) % (H * DV) == 0
    nc = T // L
    v_blk = 2 * DK // DV
    row = lambda b, c: (b * nc + c, 0)
    return pl.pallas_call(
        functools.partial(_mlstm_prompt_body, H=H, DK=DK, DV=DV),
        grid=(B, nc),
        in_specs=[pl.BlockSpec((L, H * DK), row), pl.BlockSpec((L, H * DK), lambda b, c: (b * nc + c, 1)),
                  pl.BlockSpec((L, H * DV), lambda b, c: (b * nc + c, v_blk)), pl.BlockSpec((L, H * DV), row),
                  pl.BlockSpec((L, 2 * H), row),
                  pl.BlockSpec((2 * H, L), lambda b, c: (0, b * nc + c)),
                  pl.BlockSpec((1, H * DV), lambda b, c: (0, 0))],
        out_specs=[pl.BlockSpec((L, H * DV), row),
                   pl.BlockSpec((1, H, DK, DV), lambda b, c: (b, 0, 0, 0)),
                   pl.BlockSpec((1, H, DK), lambda b, c: (b, 0, 0)),
                   pl.BlockSpec((1, 1, H), lambda b, c: (b, 0, 0))],
        out_shape=[jax.ShapeDtypeStruct((B * T, H * DV), BF16),
                   jax.ShapeDtypeStruct((B, H, DK, DV), F32),
                   jax.ShapeDtypeStruct((B, H, DK), F32),
                   jax.ShapeDtypeStruct((B, 1, H), F32)],
        compiler_params=_cparams("parallel", "arbitrary"),
        name="mlstm_prompt",
    )(qkv, qkv, qkv, og, g, gt, norm_w)


def _mlstm_sample_body(q_ref, k_ref, v_ref, og_ref, g_ref, gt_ref, nw_ref, c0_ref, n0_ref, m0_ref,
                       h_ref, c_ref, n_ref, m_ref, *, H, DK, DV, TS):
    q = q_ref[...].astype(F32)
    k = k_ref[...].astype(F32)
    v = v_ref[...].astype(F32)
    og = og_ref[...]
    g = g_ref[...]
    gt = gt_ref[0]
    HT = H * TS
    r_idx = lax.broadcasted_iota(jnp.int32, (HT, HT), 0)
    c_idx = lax.broadcasted_iota(jnp.int32, (HT, HT), 1)
    same_head = (r_idx // TS) == (c_idx // TS)
    causal = same_head & (c_idx <= r_idx)
    upto = same_head & (r_idx <= c_idx)

    def stack(parts):
        return jnp.concatenate(parts, axis=0)

    def mxu(x):
        return x.astype(BF16)

    for bb in range(SAMPLE_BS):
        rows = slice(bb * TS, (bb + 1) * TS)
        hrows = [slice(h * TS, (h + 1) * TS) for h in range(H)]
        kq = [slice(h * DK, (h + 1) * DK) for h in range(H)]
        vv = [slice(h * DV, (h + 1) * DV) for h in range(H)]
        qf = stack([q[rows, kq[h]] for h in range(H)])
        kf = stack([k[rows, kq[h]] for h in range(H)])
        v_h = [v[rows, vv[h]].astype(BF16) for h in range(H)]
        ig_col = stack([g[rows, h:h + 1] for h in range(H)])
        lf_col = stack([g[rows, H + h:H + h + 1] for h in range(H)])
        ig_row = jnp.concatenate([gt[h:h + 1, rows] for h in range(H)], axis=1)
        lf_row = jnp.concatenate([gt[H + h:H + h + 1, rows] for h in range(H)], axis=1)
        m_prev = [m0_ref[0, bb:bb + 1, h:h + 1] for h in range(H)]
        m_prev_col = stack([jnp.broadcast_to(m, (TS, 1)) for m in m_prev])

        b_col = jnp.sum(jnp.where(causal, lf_row, 0.0), axis=1, keepdims=True)
        b_row = jnp.sum(jnp.where(upto, lf_col, 0.0), axis=0, keepdims=True)
        dmat = jnp.where(causal, b_col - b_row + ig_row, -jnp.inf)
        a_col = b_col + m_prev_col
        m_t = jnp.maximum(a_col, jnp.max(dmat, axis=1, keepdims=True))
        wmat = jnp.exp(dmat - m_t) * _dg(mxu(qf), mxu(kf), _NT)
        inter = jnp.exp(a_col - m_t)
        C0 = [c0_ref[bb, h] for h in range(H)]
        n0 = [n0_ref[bb, h:h + 1, :] for h in range(H)]
        q_c = stack([_dot(mxu(qf[hrows[h]]), mxu(C0[h])) for h in range(H)])
        n_rows = stack([jnp.broadcast_to(n, (TS, DK)) for n in n0])
        num = inter * q_c + _dot(mxu(wmat), stack(v_h))
        den = inter * jnp.sum(qf * n_rows, axis=1, keepdims=True) + jnp.sum(wmat, axis=1, keepdims=True)
        hh = num * (1.0 / jnp.maximum(jnp.abs(den), jnp.exp(-m_t)))

        m_new = [m_t[(h + 1) * TS - 1:(h + 1) * TS, :] for h in range(H)]
        b_last = [b_col[(h + 1) * TS - 1:(h + 1) * TS, :] for h in range(H)]
        m_new_col = stack([jnp.broadcast_to(m, (TS, 1)) for m in m_new])
        b_last_col = stack([jnp.broadcast_to(b, (TS, 1)) for b in b_last])
        kd = kf * jnp.exp(b_last_col - b_col + ig_col - m_new_col)

        og_s = stack([og[rows, vv[h]] for h in range(H)])
        nw_s = stack([jnp.broadcast_to(nw_ref[:, vv[h]], (TS, DV)) for h in range(H)])
        out = _head_out(hh, nw_s, og_s)
        for h in range(H):
            carry = jnp.exp(b_last[h] + m_prev[h] - m_new[h])
            c_ref[bb, h] = carry * C0[h] + _dg(mxu(kd[hrows[h]]), v_h[h], _TN)
            n_ref[bb, h:h + 1, :] = carry * n0[h] + jnp.sum(kd[hrows[h]], axis=0, keepdims=True)
            m_ref[0, bb:bb + 1, h:h + 1] = m_new[h]
            h_ref[rows, vv[h]] = out[hrows[h]].astype(BF16)


def _mlstm_sample(qkv, og, g, gt, norm_w, C0, n0, m0, NP, TS):
    Bs, H, DK, DV = C0.shape
    BS = SAMPLE_BS
    assert Bs % BS == 0 and NP % (BS * TS) == 0
    R = BS * TS
    off = NP // R
    row = lambda i: (off + i, 0)
    gt_s = gt[:, NP:].reshape(2 * H, Bs // BS, R).transpose(1, 0, 2)
    m0_s = m0.reshape(Bs // BS, BS, H)
    outs = pl.pallas_call(
        functools.partial(_mlstm_sample_body, H=H, DK=DK, DV=DV, TS=TS),
        grid=(Bs // BS,),
        in_specs=[pl.BlockSpec((R, H * DK), row), pl.BlockSpec((R, H * DK), lambda i: (off + i, 1)),
                  pl.BlockSpec((R, H * DV), lambda i: (off + i, 2 * DK // DV)), pl.BlockSpec((R, H * DV), row),
                  pl.BlockSpec((R, 2 * H), row),
                  pl.BlockSpec((1, 2 * H, R), lambda i: (i, 0, 0)),
                  pl.BlockSpec((1, H * DV), lambda i: (0, 0)),
                  pl.BlockSpec((BS, H, DK, DV), lambda i: (i, 0, 0, 0)),
                  pl.BlockSpec((BS, H, DK), lambda i: (i, 0, 0)),
                  pl.BlockSpec((1, BS, H), lambda i: (i, 0, 0))],
        out_specs=[pl.BlockSpec((R, H * DV), lambda i: (i, 0)),
                   pl.BlockSpec((BS, H, DK, DV), lambda i: (i, 0, 0, 0)),
                   pl.BlockSpec((BS, H, DK), lambda i: (i, 0, 0)),
                   pl.BlockSpec((1, BS, H), lambda i: (i, 0, 0))],
        out_shape=[jax.ShapeDtypeStruct((Bs * TS, H * DV), BF16),
                   jax.ShapeDtypeStruct((Bs, H, DK, DV), F32),
                   jax.ShapeDtypeStruct((Bs, H, DK), F32),
                   jax.ShapeDtypeStruct((Bs // BS, BS, H), F32)],
        compiler_params=_cparams("parallel"),
        name="mlstm_sample",
    )(qkv, qkv, qkv, og, g, gt_s, norm_w, C0, n0, m0_s)
    return outs[0], outs[1], outs[2], outs[3].reshape(Bs, H)


def _layer_norm(z, gain, bias):
    mu = jnp.mean(z, axis=1, keepdims=True)
    zc = z - mu
    var = jnp.mean(zc * zc, axis=1, keepdims=True)
    return zc * lax.rsqrt(var + LN_EPS) * gain + bias


def _proj_norm_body(ap_ref, as_ref, w_ref, *refs, alpha, npb, x_split):
    prompt = pl.program_id(0) < npb
    a = jnp.where(prompt, ap_ref[...], as_ref[...])
    if x_split:
        xp_ref, xs_ref, gain_ref, bias_ref, o_ref, ob_ref = refs
        x = jnp.where(prompt, xp_ref[...], xs_ref[...])
    else:
        x_ref, gain_ref, bias_ref, o_ref, ob_ref = refs
        x = x_ref[...]
    y = _layer_norm(alpha * x + _dot(a, w_ref[...]), gain_ref[...], bias_ref[...])
    o_ref[...] = y
    ob_ref[...] = y.astype(BF16)


def _proj_norm(a_prompt, a_sample, w, x, gain, bias, alpha):
    NP, K = a_prompt.shape
    NS = a_sample.shape[0]
    N, D = NP + NS, w.shape[1]
    tm = _pick(NS, ROW_TILES)
    assert NP % tm == 0
    npb = NP // tm
    first = lambda i: (jnp.minimum(i, npb - 1), 0)
    second = lambda i: (jnp.maximum(i - npb, 0), 0)
    x_split = isinstance(x, tuple)
    x_ops = tuple(x) if x_split else (x,)
    x_specs = ([pl.BlockSpec((tm, D), first), pl.BlockSpec((tm, D), second)] if x_split
               else [pl.BlockSpec((tm, D), lambda i: (i, 0))])
    return pl.pallas_call(
        functools.partial(_proj_norm_body, alpha=alpha, npb=npb, x_split=x_split),
        grid=(N // tm,),
        in_specs=[pl.BlockSpec((tm, K), first), pl.BlockSpec((tm, K), second),
                  pl.BlockSpec((K, D), lambda i: (0, 0))] + x_specs +
                 [pl.BlockSpec((1, D), lambda i: (0, 0)),
                  pl.BlockSpec((1, D), lambda i: (0, 0))],
        out_specs=[pl.BlockSpec((tm, D), lambda i: (i, 0)), pl.BlockSpec((tm, D), lambda i: (i, 0))],
        out_shape=[jax.ShapeDtypeStruct((N, D), F32), jax.ShapeDtypeStruct((N, D), BF16)],
        compiler_params=_cparams("parallel"),
        name="proj_norm",
    )(a_prompt, a_sample, w, *x_ops, gain[None, :], bias[None, :])


def _static_loop(n, fn):
    for r in range(n):
        fn(r)


def _rows_loop(n, fn, unroll=8):
    def body(r, c):
        fn(r)
        return c
    lax.fori_loop(0, n, body, 0, unroll=unroll)


def _combine_norm_body(slot_ref, slot_next_ref, gate_ref, x_ref, gain_ref, bias_ref, ys_hbm, oa_ref, ob_ref,
                       ybuf, sem, *, alpha, nblk, split_blk):
    i = pl.program_id(0)
    cur = lax.rem(i, 2)
    tm, D = x_ref.shape

    def row_copy(idx_ref, k, r, s):
        return pltpu.make_async_copy(ys_hbm.at[pl.ds(idx_ref[k, r], 1)], ybuf.at[s, k, pl.ds(r, 1)], sem.at[s])

    def start_block(idx_ref, s):
        for k in range(2):
            _rows_loop(tm, lambda r: row_copy(idx_ref, k, r, s).start())

    @pl.when(i == 0)
    def _():
        start_block(slot_ref, 0)

    @pl.when(i + 1 < nblk)
    def _():
        start_block(slot_next_ref, 1 - cur)

    for k in range(2):
        _static_loop(tm, lambda r: row_copy(slot_ref, k, r, cur).wait())
    g = gate_ref[...]
    moe = g[:, 0:1] * ybuf[cur, 0].reshape(tm, D) + g[:, 1:2] * ybuf[cur, 1].reshape(tm, D)
    y = _layer_norm(alpha * x_ref[...] + moe, gain_ref[...], bias_ref[...])
    if split_blk is None:
        oa_ref[...] = y
        ob_ref[...] = y.astype(BF16)
    else:
        @pl.when(i < split_blk)
        def _():
            oa_ref[...] = y

        @pl.when(i >= split_blk)
        def _():
            ob_ref[...] = y


def _combine_norm(x, ys, slot, gate_col, gain, bias, alpha, split_rows=None):
    N, D = x.shape
    tm = _pick(N if split_rows is None else math.gcd(split_rows, N - split_rows), (256, 128))
    nblk = N // tm
    idx_spec = lambda off: pl.BlockSpec((2, tm), lambda i: (0, jnp.minimum(i + off, nblk - 1)),
                                        memory_space=pltpu.SMEM)
    if split_rows is None:
        split_blk = None
        out_specs = [pl.BlockSpec((tm, D), lambda i: (i, 0)), pl.BlockSpec((tm, D), lambda i: (i, 0))]
        out_shape = [jax.ShapeDtypeStruct((N, D), F32), jax.ShapeDtypeStruct((N, D), BF16)]
    else:
        split_blk = split_rows // tm
        out_specs = [pl.BlockSpec((tm, D), lambda i: (jnp.minimum(i, split_blk - 1), 0)),
                     pl.BlockSpec((tm, D), lambda i: (jnp.maximum(i - split_blk, 0), 0))]
        out_shape = [jax.ShapeDtypeStruct((split_rows, D), F32), jax.ShapeDtypeStruct((N - split_rows, D), F32)]
    return pl.pallas_call(
        functools.partial(_combine_norm_body, alpha=alpha, nblk=nblk, split_blk=split_blk),
        grid=(nblk,),
        in_specs=[idx_spec(0), idx_spec(1),
                  pl.BlockSpec((tm, gate_col.shape[1]), lambda i: (i, 0)),
                  pl.BlockSpec((tm, D), lambda i: (i, 0)),
                  pl.BlockSpec((1, D), lambda i: (0, 0)),
                  pl.BlockSpec((1, D), lambda i: (0, 0)),
                  pl.BlockSpec(memory_space=pl.ANY)],
        out_specs=out_specs,
        out_shape=out_shape,
        scratch_shapes=[pltpu.VMEM((2, 2, tm) + ys.shape[1:], F32), pltpu.SemaphoreType.DMA((2,))],
        compiler_params=_cparams("arbitrary"),
        name="combine_norm",
    )(slot, slot, gate_col, x, gain[None, :], bias[None, :], ys)


def _router_body(x_ref, wt_ref, b_ref, eid_ref, gate_ref, *, E):
    logits = _dot3(wt_ref[...], x_ref[...], _NT)
    ex = jnp.exp(logits - jnp.max(logits, axis=0, keepdims=True))
    probs = ex / jnp.sum(ex, axis=0, keepdims=True)
    sel = probs + b_ref[...]
    per = E // N_GROUPS
    s = [sel[e:e + 1, :] for e in range(E)]
    p = [probs[e:e + 1, :] for e in range(E)]

    def top2_sum(vals):
        best = None
        for a in range(len(vals)):
            for b in range(a + 1, len(vals)):
                pair = vals[a] + vals[b]
                best = pair if best is None else jnp.maximum(best, pair)
        return best

    g_best = top2_sum(s[:per])
    g_idx = jnp.zeros_like(g_best, dtype=jnp.int32)
    for gi in range(1, N_GROUPS):
        score = top2_sum(s[gi * per:(gi + 1) * per])
        better = score > g_best
        g_idx = jnp.where(better, gi, g_idx)
        g_best = jnp.where(better, score, g_best)

    def in_group(vals, j):
        out = vals[j]
        for gi in range(1, N_GROUPS):
            out = jnp.where(g_idx == gi, vals[gi * per + j], out)
        return out

    sg = [in_group(s, j) for j in range(per)]
    pg = [in_group(p, j) for j in range(per)]
    i1 = jnp.zeros_like(g_idx)
    b1 = sg[0]
    for j in range(1, per):
        better = sg[j] > b1
        i1 = jnp.where(better, j, i1)
        b1 = jnp.where(better, sg[j], b1)
    i2 = jnp.zeros_like(g_idx)
    b2 = jnp.full_like(b1, -jnp.inf)
    for j in range(per):
        better = (i1 != j) & (sg[j] > b2)
        i2 = jnp.where(better, j, i2)
        b2 = jnp.where(better, sg[j], b2)
    p1 = pg[0]
    p2 = pg[0]
    for j in range(1, per):
        p1 = jnp.where(i1 == j, pg[j], p1)
        p2 = jnp.where(i2 == j, pg[j], p2)
    tot = p1 + p2
    eid_ref[...] = jnp.concatenate([g_idx * per + i1, g_idx * per + i2], axis=0)
    gates = jnp.concatenate([p1 / tot, p2 / tot, jnp.zeros((GATE_COLS - 2, p1.shape[1]), F32)], axis=0)
    gate_ref[...] = gates.T


def _router(x, w_router, b_router):
    N, D = x.shape
    E = w_router.shape[1]
    assert E % N_GROUPS == 0 and E // N_GROUPS >= 2
    tm = _pick(N, ROW_TILES)
    return pl.pallas_call(
        functools.partial(_router_body, E=E),
        grid=(N // tm,),
        in_specs=[pl.BlockSpec((tm, D), lambda i: (i, 0)),
                  pl.BlockSpec((E, D), lambda i: (0, 0)),
                  pl.BlockSpec((E, 1), lambda i: (0, 0))],
        out_specs=[pl.BlockSpec((2, tm), lambda i: (0, i)), pl.BlockSpec((tm, GATE_COLS), lambda i: (i, 0))],
        out_shape=[jax.ShapeDtypeStruct((2, N), jnp.int32), jax.ShapeDtypeStruct((N, GATE_COLS), F32)],
        compiler_params=_cparams("parallel"),
        name="router",
    )(x, w_router.T, b_router.astype(F32)[:, None])


def _plan_body(e_ref, slot_ref, cnt_ref, *, E, TM):
    e = e_ref[...]
    R = e.shape[0]
    upper = (lax.broadcasted_iota(jnp.int32, (LANES, LANES), 0)
             <= lax.broadcasted_iota(jnp.int32, (LANES, LANES), 1)).astype(BF16)
    below = (lax.broadcasted_iota(jnp.int32, (R, R), 1) < lax.broadcasted_iota(jnp.int32, (R, R), 0)).astype(BF16)
    lane = lax.broadcasted_iota(jnp.int32, (1, LANES), 1)
    slot = jnp.zeros(e.shape, F32)
    counts = jnp.zeros((1, LANES), F32)
    start = jnp.zeros((1, 1), F32)
    for ex in range(E):
        hit = e == ex
        within = _dot(hit.astype(BF16), upper)
        rows_before = _dot(below, within.astype(BF16))[:, LANES - 1:LANES]
        rank = within - 1.0 + rows_before
        count = rows_before[R - 1:R, :] + within[R - 1:R, LANES - 1:LANES]
        slot = jnp.where(hit, start + rank, slot)
        counts = jnp.where(lane == ex, count, counts)
        start = start + jnp.floor((count + (TM - 1)) * (1.0 / TM)) * TM
    slot_ref[...] = slot.astype(jnp.int32)
    cnt_ref[...] = jnp.broadcast_to(counts, cnt_ref.shape).astype(jnp.int32)


def _plan(eid, E, TM):
    N = eid.shape[1]
    A = 2 * N
    assert A % LANES == 0 and (TM & (TM - 1)) == 0
    R = A // LANES
    Rp = -(-R // LANES) * LANES
    e2d = jnp.pad(eid.reshape(R, LANES), ((0, Rp - R), (0, 0)), constant_values=-1)
    slot2d, cnt = pl.pallas_call(
        functools.partial(_plan_body, E=E, TM=TM),
        out_shape=[jax.ShapeDtypeStruct((Rp, LANES), jnp.int32), jax.ShapeDtypeStruct((8, LANES), jnp.int32)],
        compiler_params=pltpu.CompilerParams(vmem_limit_bytes=VMEM_LIMIT_BYTES),
        name="route_plan",
    )(e2d)
    return slot2d[:R].reshape(2, N), cnt[0, :E]


def _group_table(counts, TM, n_blocks):
    E = counts.shape[0]
    GB = MOE_GROUP_BLOCKS
    nblk = (counts + TM - 1) // TM
    blk_end = jnp.cumsum(nblk)
    blk_start = blk_end - nblk
    ngrp = (nblk + GB - 1) // GB
    grp_end = jnp.cumsum(ngrp)
    n_groups_max = E + n_blocks // GB
    g = jnp.arange(n_groups_max, dtype=jnp.int32)
    g_expert = jnp.minimum(jnp.sum((grp_end[None, :] <= g[:, None]).astype(jnp.int32), axis=1), E - 1)
    j = g - (grp_end - ngrp)[g_expert]
    g_start = blk_start[g_expert] + j * GB
    g_nblk = jnp.clip(nblk[g_expert] - j * GB, 0, GB)
    live = g < grp_end[-1]
    g_nblk = jnp.where(live, g_nblk, 0)
    meta = jnp.stack([grp_end[-1], blk_end[-1]]).astype(jnp.int32)
    pad_start = blk_start * TM + counts
    pad_count = nblk * TM - counts
    return (g_expert.astype(jnp.int32), g_start.astype(jnp.int32), g_nblk.astype(jnp.int32), meta,
            pad_start.astype(jnp.int32), pad_count.astype(jnp.int32))


def _pack_halves(x):
    half = x.shape[1] // 2
    bits = lax.bitcast_convert_type(x.astype(BF16).astype(F32), jnp.uint32)
    return (bits[:, :half] >> 16) | bits[:, half:]


def _unpack_halves(p):
    lo = lax.bitcast_convert_type(p << 16, F32).astype(BF16)
    hi = lax.bitcast_convert_type(p & jnp.uint32(0xFFFF0000), F32).astype(BF16)
    return lo, hi


def _dispatch_body(pad_start_ref, pad_count_ref, meta_ref, slot_ref, x_ref, xs_hbm, pbuf, zbuf, sem, zsem,
                   *, nblk, E, TM, n_blocks):
    i = pl.program_id(0)
    cur = lax.rem(i, 2)
    tm = x_ref.shape[0]

    def row_copy(k, r, s):
        return pltpu.make_async_copy(pbuf.at[s, pl.ds(r, 1)], xs_hbm.at[pl.ds(slot_ref[k, r], 1)], sem.at[s])

    def wait_block(s):
        for k in range(2):
            _static_loop(tm, lambda r: row_copy(k, r, s).wait())

    def zero_copy(row):
        return pltpu.make_async_copy(zbuf.at[pl.ds(0, 1)], xs_hbm.at[pl.ds(row, 1)], zsem.at[0])

    @pl.when(i == 0)
    def _():
        zbuf[...] = jnp.zeros_like(zbuf)
        for ex in range(E):
            _rows_loop(pad_count_ref[ex], lambda j: zero_copy(pad_start_ref[ex] + j).start(), unroll=1)
        tail0 = meta_ref[1] * TM
        ntail = n_blocks * TM - tail0
        _rows_loop(ntail, lambda j: zero_copy(tail0 + j).start(), unroll=1)
        for ex in range(E):
            _rows_loop(pad_count_ref[ex], lambda j: zero_copy(pad_start_ref[ex] + j).wait(), unroll=1)
        _rows_loop(ntail, lambda j: zero_copy(tail0 + j).wait(), unroll=1)

    @pl.when(i >= 2)
    def _():
        wait_block(cur)

    pbuf[cur] = _pack_halves(x_ref[...]).reshape(pbuf.shape[1:])
    for k in range(2):
        _rows_loop(tm, lambda r: row_copy(k, r, cur).start())

    @pl.when(i == nblk - 1)
    def _():
        wait_block(cur)
        if nblk >= 2:
            wait_block(1 - cur)


def _dispatch(x, slot, pad_start, pad_count, meta, TM, n_blocks):
    N, D = x.shape
    E = pad_start.shape[0]
    tm = _pick(N, (256, 128))
    nblk = N // tm
    row_tile = (D // 2 // LANES, LANES)
    grid_spec = pltpu.PrefetchScalarGridSpec(
        num_scalar_prefetch=3,
        grid=(nblk,),
        in_specs=[pl.BlockSpec((2, tm), lambda i, *_: (0, i), memory_space=pltpu.SMEM),
                  pl.BlockSpec((tm, D), lambda i, *_: (i, 0))],
        out_specs=pl.BlockSpec(memory_space=pl.ANY),
        scratch_shapes=[pltpu.VMEM((2, tm) + row_tile, jnp.uint32), pltpu.VMEM((1,) + row_tile, jnp.uint32),
                        pltpu.SemaphoreType.DMA((2,)), pltpu.SemaphoreType.DMA((1,))],
    )
    return pl.pallas_call(
        functools.partial(_dispatch_body, nblk=nblk, E=E, TM=TM, n_blocks=n_blocks),
        grid_spec=grid_spec,
        out_shape=jax.ShapeDtypeStruct((n_blocks * TM,) + row_tile, jnp.uint32),
        compiler_params=_cparams("arbitrary"),
        name="moe_dispatch",
    )(pad_start, pad_count, meta, slot, x)


def _experts_body(ge_ref, gs_ref, gn_ref, meta_ref, xs_hbm, wg_hbm, wu_hbm, wd_hbm, ys_hbm,
                  xraw, xg, acc, stage_g, stage_u, stage_d, wgu, wdb, obuf, xsem, wsem, osem,
                  *, layer, chunks, TM, n_blocks):
    g = pl.program_id(0)
    n_groups = meta_ref[0]
    half = xg.shape[2]
    n_chunks = len(chunks)

    def x_copy(b0, rb):
        return pltpu.make_async_copy(xs_hbm.at[pl.ds((b0 + rb) * TM, TM)], xraw.at[rb], xsem.at[0])

    def w_copies(ex, c, s):
        f0, fw = chunks[c]
        return (pltpu.make_async_copy(wg_hbm.at[layer, ex, :, pl.ds(f0, fw)], stage_g.at[s, :, pl.ds(0, fw)],
                                      wsem.at[s, 0]),
                pltpu.make_async_copy(wu_hbm.at[layer, ex, :, pl.ds(f0, fw)], stage_u.at[s, :, pl.ds(0, fw)],
                                      wsem.at[s, 1]),
                pltpu.make_async_copy(wd_hbm.at[layer, ex, pl.ds(f0, fw), :], stage_d.at[s, pl.ds(0, fw), :],
                                      wsem.at[s, 2]))

    def out_copy(b0, rb, s):
        return pltpu.make_async_copy(obuf.at[s], ys_hbm.at[pl.ds((b0 + rb) * TM, TM)], osem.at[s])

    @pl.when(g < n_groups)
    def _():
        ex = ge_ref[g]
        b0 = gs_ref[g]
        nb = gn_ref[g]
        _rows_loop(nb, lambda rb: x_copy(b0, rb).start(), unroll=1)

        def start_weights(expert, c, s):
            for idx, cp in enumerate(w_copies(expert, c, s)):
                cp.start(priority=idx % 2)

        @pl.when(g == 0)
        def _():
            start_weights(ex, 0, 0)

        _rows_loop(nb, lambda rb: x_copy(b0, rb).wait(), unroll=1)

        def untile(rb):
            xg[rb] = xraw[rb].reshape(TM, half)

        _rows_loop(nb, untile, unroll=1)

        for c, (f0, fw) in enumerate(chunks):
            s = c % 2
            for cp in w_copies(ex, c, s):
                cp.wait()
            if c + 1 < n_chunks:
                start_weights(ex, c + 1, 1 - s)
            else:
                @pl.when(g + 1 < n_groups)
                def _():
                    start_weights(ge_ref[g + 1], 0, 1 - s)
            wgu[:, 0:fw] = stage_g[s, :, 0:fw].astype(BF16)
            wgu[:, fw:2 * fw] = stage_u[s, :, 0:fw].astype(BF16)
            wdb[0:fw, :] = stage_d[s, 0:fw, :].astype(BF16)

            def block(rb, carry, c=c, fw=fw):
                last = c + 1 == n_chunks
                buf = lax.rem(rb, 2)
                if last:
                    @pl.when(rb >= 2)
                    def _():
                        out_copy(b0, rb, buf).wait()
                xl, xh = _unpack_halves(xg[rb])
                gu = _dot(xl, wgu[0:half, 0:2 * fw]) + _dot(xh, wgu[half:2 * half, 0:2 * fw])
                hidden = jax.nn.silu(gu[:, 0:fw]) * gu[:, fw:2 * fw]
                part = _dot(hidden.astype(BF16), wdb[0:fw, :])
                if c == 0:
                    acc[rb] = part
                elif not last:
                    acc[rb] = acc[rb] + part
                else:
                    obuf[buf] = (acc[rb] + part).reshape(obuf.shape[1:])
                    out_copy(b0, rb, buf).start()
                return carry

            lax.fori_loop(0, nb, block, 0)

        out_copy(b0, 0, lax.rem(nb - 1, 2)).wait()

        @pl.when(nb >= 2)
        def _():
            out_copy(b0, 0, lax.rem(nb, 2)).wait()

    @pl.when(g == pl.num_programs(0) - 1)
    def _():
        used = meta_ref[1]
        obuf[0] = jnp.zeros(obuf.shape[1:], F32)
        _rows_loop(n_blocks - used, lambda j: out_copy(used, j, 0).start(), unroll=1)
        _rows_loop(n_blocks - used, lambda j: out_copy(used, j, 0).wait(), unroll=1)


def _experts(xs, g_expert, g_start, g_nblk, meta, w_gate, w_up, w_down, layer, TM, n_blocks):
    _, E, D, F = w_gate.shape
    GB = MOE_GROUP_BLOCKS
    FC = MOE_FC
    chunks = tuple((f0, min(FC, F - f0)) for f0 in range(0, F, FC))
    assert len(chunks) % 2 == 0 and all(fw % LANES == 0 for _, fw in chunks)
    out_tile = (D // LANES, LANES)
    grid_spec = pltpu.PrefetchScalarGridSpec(
        num_scalar_prefetch=4,
        grid=(g_expert.shape[0],),
        in_specs=[pl.BlockSpec(memory_space=pl.ANY)] * 4,
        out_specs=pl.BlockSpec(memory_space=pl.ANY),
        scratch_shapes=[pltpu.VMEM((GB, TM) + xs.shape[1:], jnp.uint32),
                        pltpu.VMEM((GB, TM, D // 2), jnp.uint32), pltpu.VMEM((GB, TM, D), F32),
                        pltpu.VMEM((2, D, FC), F32), pltpu.VMEM((2, D, FC), F32), pltpu.VMEM((2, FC, D), F32),
                        pltpu.VMEM((D, 2 * FC), BF16), pltpu.VMEM((FC, D), BF16),
                        pltpu.VMEM((2, TM) + out_tile, F32),
                        pltpu.SemaphoreType.DMA((1,)), pltpu.SemaphoreType.DMA((2, 3)),
                        pltpu.SemaphoreType.DMA((2,))],
    )
    return pl.pallas_call(
        functools.partial(_experts_body, layer=layer, chunks=chunks, TM=TM, n_blocks=n_blocks),
        grid_spec=grid_spec,
        out_shape=jax.ShapeDtypeStruct((n_blocks * TM,) + out_tile, F32),
        compiler_params=_cparams("arbitrary"),
        name="moe_experts",
    )(g_expert, g_start, g_nblk, meta, xs, w_gate, w_up, w_down)


def _moe_layer(x, w_router, b_router, w_gate, w_up, w_down, layer, gain, bias, alpha, split_rows=None):
    N = x.shape[0]
    E = w_router.shape[1]
    TM = MOE_TM
    n_blocks = -(-(2 * N + E * (TM - 1)) // TM)
    eid, gate_col = _router(x, w_router, b_router)
    slot, counts = _plan(eid, E, TM)
    g_expert, g_start, g_nblk, meta, pad_start, pad_count = _group_table(counts, TM, n_blocks)
    xs = _dispatch(x, slot, pad_start, pad_count, meta, TM, n_blocks)
    ys = _experts(xs, g_expert, g_start, g_nblk, meta, w_gate, w_up, w_down, layer, TM, n_blocks)
    return _combine_norm(x, ys, slot, gate_col, gain, bias, alpha, split_rows)


def _pair_attention(q_pairs, kc, vc, bias, sinks_lo, sinks_hi):
    P = len(q_pairs)
    R, lanes = q_pairs[0].shape
    lo = lax.broadcasted_iota(jnp.int32, (R, lanes), 1) < lanes // 2
    zero = jnp.zeros((R, lanes), BF16)
    qs = jnp.concatenate([jnp.where(lo, qp, zero) for qp in q_pairs]
                         + [jnp.where(lo, zero, qp) for qp in q_pairs], axis=0)
    sink_col = jnp.concatenate([jnp.full((R, 1), s, F32) for s in list(sinks_lo) + list(sinks_hi)], axis=0)
    s = _dg(qs, kc, _NT) + bias
    mx = jnp.maximum(jnp.max(s, axis=1, keepdims=True), sink_col)
    e = jnp.exp(s - mx)
    den = jnp.sum(e, axis=1, keepdims=True) + jnp.exp(sink_col - mx)
    o = _dot(e.astype(BF16), vc) * (1.0 / den)
    return [jnp.where(lo, o[p * R:(p + 1) * R], o[(P + p) * R:(P + p + 1) * R]) for p in range(P)]


def _swa_prompt_body(sink_ref, q_ref, kp_ref, kc_ref, vp_ref, vc_ref, o_ref, *, KVH, G):
    W = q_ref.shape[0]
    P = G // 2
    first = pl.program_id(1) == 0
    r = lax.broadcasted_iota(jnp.int32, (W, 2 * W), 0)
    kj = lax.broadcasted_iota(jnp.int32, (W, 2 * W), 1)
    diff = r + W - kj
    valid = (diff >= 0) & (diff < W) & (jnp.logical_not(first) | (kj >= W))
    bias = jnp.concatenate([jnp.where(valid, 0.0, -jnp.inf)] * G, axis=0)
    for g in range(KVH):
        ks = slice(g * LANES, (g + 1) * LANES)
        kc = jnp.concatenate([kp_ref[:, ks], kc_ref[:, ks]], axis=0)
        vc = jnp.concatenate([vp_ref[:, ks], vc_ref[:, ks]], axis=0)
        cols = [slice((g * P + p) * LANES, (g * P + p + 1) * LANES) for p in range(P)]
        outs = _pair_attention([q_ref[:, c] for c in cols], kc, vc, bias,
                               [sink_ref[g * G + 2 * p] for p in range(P)],
                               [sink_ref[g * G + 2 * p + 1] for p in range(P)])
        for c, o in zip(cols, outs):
            o_ref[:, c] = o.astype(BF16)


def _swa_prompt(q, kdup, vdup, sinks, B, T, KVH):
    Hq = sinks.shape[0]
    W = WINDOW
    assert T % W == 0 and kdup.shape[1] == KVH * LANES and q.shape[1] == Hq // 2 * LANES
    nb = T // W
    cur = lambda b, n: (b * nb + n, 0)
    prev = lambda b, n: (b * nb + jnp.maximum(n - 1, 0), 0)
    return pl.pallas_call(
        functools.partial(_swa_prompt_body, KVH=KVH, G=Hq // KVH),
        grid=(B, nb),
        in_specs=[pl.BlockSpec(memory_space=pltpu.SMEM),
                  pl.BlockSpec((W, q.shape[1]), cur),
                  pl.BlockSpec((W, KVH * LANES), prev), pl.BlockSpec((W, KVH * LANES), cur),
                  pl.BlockSpec((W, KVH * LANES), prev), pl.BlockSpec((W, KVH * LANES), cur)],
        out_specs=pl.BlockSpec((W, q.shape[1]), cur),
        out_shape=jax.ShapeDtypeStruct((B * T, q.shape[1]), BF16),
        compiler_params=_cparams("parallel", "parallel"),
        name="swa_prompt",
    )(sinks, q, kdup, kdup, vdup, vdup)


def _swa_sample_body(sink_ref, q_ref, ck_ref, cv_ref, kn_ref, vn_ref, o_ref, *, KVH, G, TS, WB, K):
    P = G // 2
    NPAIR = KVH * P
    HD = LANES // 2
    q = q_ref[...].astype(F32)
    r = lax.broadcasted_iota(jnp.int32, (2 * NPAIR * TS, KVH * K), 0)
    c = lax.broadcasted_iota(jnp.int32, (2 * NPAIR * TS, KVH * K), 1)
    diff = WB + r % TS - c % K
    own = (r % (NPAIR * TS)) // (P * TS) == c // K
    bias = jnp.where(own & (diff >= 0) & (diff < WINDOW), 0.0, -jnp.inf)
    cols = [slice(p * LANES, (p + 1) * LANES) for p in range(NPAIR)]
    sinks_lo = [sink_ref[2 * p] for p in range(NPAIR)]
    sinks_hi = [sink_ref[2 * p + 1] for p in range(NPAIR)]
    pad = jnp.zeros((K - WB - TS, HD), F32)

    def stack_heads(window, new):
        parts = []
        for g in range(KVH):
            hs = slice(g * HD, (g + 1) * HD)
            one = jnp.concatenate([window[:, hs], new[:, hs], pad], axis=0)
            parts.append(jnp.concatenate([one, one], axis=1))
        return jnp.concatenate(parts, axis=0).astype(BF16)

    for bb in range(SAMPLE_BS):
        rows = slice(bb * TS, (bb + 1) * TS)
        kc = stack_heads(ck_ref[bb], kn_ref[rows, :])
        vc = stack_heads(cv_ref[bb], vn_ref[rows, :])
        outs = _pair_attention([q[rows, cc].astype(BF16) for cc in cols], kc, vc, bias, sinks_lo, sinks_hi)
        for cc, o in zip(cols, outs):
            o_ref[rows, cc] = o.astype(BF16)


def _swa_sample(q, cache_k, cache_v, k_new, v_new, sinks, NP, TS, KVH):
    Bs, WB, nkv = cache_k.shape
    Hq = sinks.shape[0]
    BS = SAMPLE_BS
    R = BS * TS
    off = NP // R
    row = lambda i: (off + i, 0)
    K = -(-(WB + TS) // LANES) * LANES
    return pl.pallas_call(
        functools.partial(_swa_sample_body, KVH=KVH, G=Hq // KVH, TS=TS, WB=WB, K=K),
        grid=(Bs // BS,),
        in_specs=[pl.BlockSpec(memory_space=pltpu.SMEM),
                  pl.BlockSpec((R, q.shape[1]), row),
                  pl.BlockSpec((BS, WB, nkv), lambda i: (i, 0, 0)),
                  pl.BlockSpec((BS, WB, nkv), lambda i: (i, 0, 0)),
                  pl.BlockSpec((R, nkv), row),
                  pl.BlockSpec((R, nkv), row)],
        out_specs=pl.BlockSpec((R, q.shape[1]), lambda i: (i, 0)),
        out_shape=jax.ShapeDtypeStruct((Bs * TS, q.shape[1]), BF16),
        compiler_params=_cparams("parallel"),
        name="swa_sample",
    )(sinks, q, cache_k, cache_v, k_new, v_new)


def _rope_tables(pos, head_dim, lanes):
    half = head_dim // 2
    inv = ROPE_THETA ** (-jnp.arange(half, dtype=F32) / half)
    ang = pos.astype(F32)[:, None] * inv[None, :]
    reps = lanes // half
    return jnp.tile(jnp.cos(ang), (1, reps)), jnp.tile(jnp.sin(ang), (1, reps))


def kernel(x_prompt, x_sample, state_C, state_n, state_m, cache_k, cache_v, w_in_a, b_i_a, b_f_a, norm_a, w_out_a,
           w_kv, w_q_b, sinks_b, w_out_b, w_router, b_router, w_gate_e, w_up_e, w_down_e, ln_g, ln_b):
    B, T, D = x_prompt.shape
    Bs, TS, _ = x_sample.shape
    depth = ln_g.shape[0]
    assert depth == 2 and w_in_a.shape[0] == 1 and w_q_b.shape[0] == 1
    alpha = (2 * depth) ** 0.25
    NP, NS = B * T, Bs * TS
    H = b_i_a.shape[1]
    DK, DV = state_C.shape[3], state_C.shape[4]
    WB, KVH, HD = cache_k.shape[1], cache_k.shape[2], cache_k.shape[3]
    E = w_router.shape[1]

    x0_p = x_prompt.reshape(NP, D).astype(F32)
    x0_s = x_sample.reshape(NS, D).astype(F32)

    w_in = w_in_a.astype(F32)
    nqk, nv = H * DK, H * DV
    w_gates = w_in[0, :, 2 * nqk + nv + D:]
    g, gt, x0b = _mlstm_gates(x0_p, x0_s, w_gates, b_i_a[0], b_f_a[0])
    qkv_scale = jnp.concatenate([jnp.ones((nqk,), F32), jnp.full((nqk,), DK ** -0.5, F32), jnp.ones((nv,), F32)])
    qkv = _matmul(x0b, w_in, qkv_scale[None, :], BF16, col0=0, ncols=2 * nqk + nv)
    og = _matmul(x0b, w_in, jnp.ones((1, D), F32), F32, col0=2 * nqk + nv, ncols=D)
    norm_w = norm_a[0].astype(F32)[None, :]
    h_p, C_p, n_p, m_p = _mlstm_prompt(qkv, og, g, gt, norm_w, B, T, H, DK, DV)
    h_s, C_s, n_s, m_s = _mlstm_sample(qkv, og, g, gt, norm_w, state_C[0].astype(F32), state_n[0].astype(F32),
                                       state_m[0].astype(F32), NP, TS)
    x1, _ = _proj_norm(h_p, h_s, w_out_a[0].astype(BF16), (x0_p, x0_s), ln_g[0, 0], ln_b[0, 0], alpha)
    x2, x2b = _moe_layer(x1, w_router, b_router, w_gate_e, w_up_e, w_down_e, 0, ln_g[0, 1], ln_b[0, 1], alpha)

    pos = jnp.concatenate([jnp.tile(jnp.arange(T, dtype=jnp.int32), B),
                           jnp.tile(PAST_LEN + jnp.arange(TS, dtype=jnp.int32), Bs)])
    cos, sin = _rope_tables(pos, HD, 128)
    nkv = KVH * HD
    k_new = _matmul_rope(x2b, w_kv[:, :nkv].astype(BF16), cos, sin, HD, F32)
    v_new = _matmul(x2b, w_kv[:, nkv:].astype(BF16), jnp.ones((1, nkv), F32), F32)
    assert 2 * HD == LANES
    qb = _matmul_rope(x2b, w_q_b[0].astype(BF16), cos, sin, HD, BF16, scale=HD ** -0.5)

    def dup_heads(a):
        a4 = a.reshape(a.shape[:-1] + (KVH, HD)).astype(BF16)
        return jnp.concatenate([a4, a4], axis=-1).reshape(a.shape[:-1] + (KVH * LANES,))

    o_p = _swa_prompt(qb, dup_heads(k_new), dup_heads(v_new), sinks_b[0].astype(F32), B, T, KVH)
    ck = cache_k.reshape(Bs, WB, nkv).astype(F32)
    cv = cache_v.reshape(Bs, WB, nkv).astype(F32)
    o_s = _swa_sample(qb, ck, cv, k_new, v_new, sinks_b[0].astype(F32), NP, TS, KVH)
    x3, _ = _proj_norm(o_p, o_s, w_out_b[0].astype(BF16), x2, ln_g[1, 0], ln_b[1, 0], alpha)
    y_p, y_s = _moe_layer(x3, w_router, b_router, w_gate_e, w_up_e, w_down_e, 1, ln_g[1, 1], ln_b[1, 1], alpha,
                          split_rows=NP)
    y_prompt = y_p.reshape(B, T, D)
    y_sample = y_s.reshape(Bs, TS, D)
    k_p = k_new[:NP].reshape(B, T, KVH, HD)[:, -WB:]
    v_p = v_new[:NP].reshape(B, T, KVH, HD)[:, -WB:]
    k_s = jnp.concatenate([ck[:, TS:], k_new[NP:].reshape(Bs, TS, nkv)], axis=1).reshape(Bs, WB, KVH, HD)
    v_s = jnp.concatenate([cv[:, TS:], v_new[NP:].reshape(Bs, TS, nkv)], axis=1).reshape(Bs, WB, KVH, HD)
    return (y_prompt, y_sample, C_p[None], n_p[None], m_p.reshape(1, B, H), k_p, v_p,
            C_s[None], n_s[None], m_s[None], k_s, v_s)
```

```python
import functools
import math

import jax
import jax.numpy as jnp
from jax import lax
from jax.experimental import pallas as pl
from jax.experimental.pallas import tpu as pltpu

F32 = jnp.float32
BF16 = jnp.bfloat16

GATE_CAP = 15.0
LN_EPS = 1e-5
ROPE_THETA = 10000.0
WINDOW = 128
PAST_LEN = 8192
N_GROUPS = 4

VMEM_LIMIT_BYTES = 60 * 1024 * 1024
MLSTM_CHUNK = 256
MOE_TM = 256
MOE_GROUP_BLOCKS = 6
MOE_FC = 256
GATE_COLS = 8
LANES = 128
SAMPLE_BS = 4
ROW_TILES = (512, 256, 128)
MATMUL_ROW_TILES = (1024,) + ROW_TILES

_NT = (((1,), (1,)), ((), ()))
_TN = (((0,), (0,)), ((), ()))


def _pick(n, cands):
    for c in cands:
        if n % c == 0:
            return c
    raise ValueError(f"no tile in {cands} divides {n}")


def _cparams(*sem):
    return pltpu.CompilerParams(dimension_semantics=sem, vmem_limit_bytes=VMEM_LIMIT_BYTES)


def _dot(a, b):
    return jnp.dot(a, b, preferred_element_type=F32)


def _dg(a, b, dims):
    return lax.dot_general(a, b, dims, preferred_element_type=F32)


def _mm_scale_body(a_ref, b_ref, s_ref, o_ref):
    o_ref[...] = (_dot(a_ref[...], b_ref[...].astype(BF16)) * s_ref[...]).astype(o_ref.dtype)


def _matmul(a, b, col_scale, out_dtype, col0=0, ncols=None):
    M, K = a.shape
    N = b.shape[-1] if ncols is None else ncols
    tm = _pick(M, MATMUL_ROW_TILES)
    tn = _pick(math.gcd(N, col0) if col0 else N, (1024, 512, 256))
    j0 = col0 // tn
    return pl.pallas_call(
        _mm_scale_body,
        grid=(M // tm, N // tn),
        in_specs=[pl.BlockSpec((tm, K), lambda i, j: (i, 0)),
                  pl.BlockSpec((None, K, tn), lambda i, j: (0, 0, j0 + j)),
                  pl.BlockSpec((1, tn), lambda i, j: (0, j))],
        out_specs=pl.BlockSpec((tm, tn), lambda i, j: (i, j)),
        out_shape=jax.ShapeDtypeStruct((M, N), out_dtype),
        compiler_params=_cparams("parallel", "parallel"),
        name="matmul",
    )(a, b if b.ndim == 3 else b[None], col_scale)


def _mm_rope_body(a_ref, b_ref, cos_ref, sin_ref, o_ref, *, half, scale):
    acc = _dot(a_ref[...], b_ref[...])
    tn = acc.shape[1]
    reps = tn // cos_ref.shape[1]
    cos = jnp.concatenate([cos_ref[...]] * reps, axis=1)
    sin = jnp.concatenate([sin_ref[...]] * reps, axis=1)
    lane = lax.broadcasted_iota(jnp.int32, acc.shape, 1)
    partner = jnp.where(lane % (2 * half) < half, -pltpu.roll(acc, tn - half, 1), pltpu.roll(acc, half, 1))
    o_ref[...] = ((acc * cos + partner * sin) * scale).astype(o_ref.dtype)


def _matmul_rope(a, b, cos, sin, head_dim, out_dtype, scale=1.0):
    M, K = a.shape
    N = b.shape[1]
    tm = _pick(M, MATMUL_ROW_TILES)
    tn = _pick(N, (1024, 512, 256))
    assert math.frexp(scale)[0] == 0.5
    return pl.pallas_call(
        functools.partial(_mm_rope_body, half=head_dim // 2, scale=scale),
        grid=(M // tm, N // tn),
        in_specs=[pl.BlockSpec((tm, K), lambda i, j: (i, 0)),
                  pl.BlockSpec((K, tn), lambda i, j: (0, j)),
                  pl.BlockSpec((tm, cos.shape[1]), lambda i, j: (i, 0)),
                  pl.BlockSpec((tm, sin.shape[1]), lambda i, j: (i, 0))],
        out_specs=pl.BlockSpec((tm, tn), lambda i, j: (i, j)),
        out_shape=jax.ShapeDtypeStruct((M, N), out_dtype),
        compiler_params=_cparams("parallel", "parallel"),
        name="matmul_rope",
    )(a, b, cos, sin)


def _split_bf16(x):
    hi = x.astype(BF16)
    lo = (x - hi.astype(F32)).astype(BF16)
    return hi, lo


def _dot3(a, b, dims):
    ah, al = _split_bf16(a)
    bh, bl = _split_bf16(b)
    return _dg(ah, bh, dims) + _dg(al, bh, dims) + _dg(ah, bl, dims)


def _softcap(z):
    return GATE_CAP * jnp.tanh(z / GATE_CAP)


def _log_sigmoid(z):
    return jnp.minimum(z, 0.0) - jnp.log1p(jnp.exp(-jnp.abs(z)))


def _gates_body(xp_ref, xs_ref, w_ref, wt_ref, brow_ref, bcol_ref, g_ref, gt_ref, xb_ref, *, H, npb):
    x = jnp.where(pl.program_id(0) < npb, xp_ref[...], xs_ref[...])
    xb_ref[...] = x.astype(BF16)
    pre = _dot3(x, w_ref[...], (((1,), (0,)), ((), ()))) + brow_ref[...]
    pre_t = _dot3(wt_ref[...], x, _NT) + bcol_ref[...]
    z = _softcap(pre)
    zt = _softcap(pre_t)
    is_in = lax.broadcasted_iota(jnp.int32, z.shape, 1) < H
    is_in_t = lax.broadcasted_iota(jnp.int32, zt.shape, 0) < H
    g_ref[...] = jnp.where(is_in, z, _log_sigmoid(z))
    gt_ref[...] = jnp.where(is_in_t, zt, _log_sigmoid(zt))


def _mlstm_gates(x_prompt, x_sample, w_gates, b_i, b_f):
    NP, D = x_prompt.shape
    NS = x_sample.shape[0]
    N = NP + NS
    H = b_i.shape[0]
    tm = _pick(NS, ROW_TILES)
    assert NP % tm == 0
    npb = NP // tm
    bias = jnp.concatenate([b_i, b_f]).astype(F32)
    return pl.pallas_call(
        functools.partial(_gates_body, H=H, npb=npb),
        grid=(N // tm,),
        in_specs=[pl.BlockSpec((tm, D), lambda i: (jnp.minimum(i, npb - 1), 0)),
                  pl.BlockSpec((tm, D), lambda i: (jnp.maximum(i - npb, 0), 0)),
                  pl.BlockSpec((D, 2 * H), lambda i: (0, 0)),
                  pl.BlockSpec((2 * H, D), lambda i: (0, 0)),
                  pl.BlockSpec((1, 2 * H), lambda i: (0, 0)),
                  pl.BlockSpec((2 * H, 1), lambda i: (0, 0))],
        out_specs=[pl.BlockSpec((tm, 2 * H), lambda i: (i, 0)),
                   pl.BlockSpec((2 * H, tm), lambda i: (0, i)),
                   pl.BlockSpec((tm, D), lambda i: (i, 0))],
        out_shape=[jax.ShapeDtypeStruct((N, 2 * H), F32), jax.ShapeDtypeStruct((2 * H, N), F32),
                   jax.ShapeDtypeStruct((N, D), BF16)],
        compiler_params=_cparams("parallel"),
        name="mlstm_gates",
    )(x_prompt, x_sample, w_gates, w_gates.T, bias[None, :], bias[:, None])


def _mlstm_head(qh, kh, vh, ig_col, lf_col, ig_row, lf_row, C, n_row, m_prev):
    L = qh.shape[0]
    t_idx = lax.broadcasted_iota(jnp.int32, (L, L), 0)
    s_idx = lax.broadcasted_iota(jnp.int32, (L, L), 1)
    causal = s_idx <= t_idx

    def mxu(x):
        return x.astype(BF16)

    b_col = jnp.sum(jnp.where(causal, lf_row, 0.0), axis=1, keepdims=True)
    b_row = jnp.sum(jnp.where(t_idx <= s_idx, lf_col, 0.0), axis=0, keepdims=True)
    dmat = jnp.where(causal, b_col - b_row + ig_row, -jnp.inf)
    a_col = b_col + m_prev
    m_t = jnp.maximum(a_col, jnp.max(dmat, axis=1, keepdims=True))
    wmat = jnp.exp(dmat - m_t) * _dg(qh, kh, _NT)
    inter = jnp.exp(a_col - m_t)
    qf = qh.astype(F32)
    kf = kh.astype(F32)
    num = inter * _dot(qh, mxu(C)) + _dot(mxu(wmat), vh)
    den = inter * jnp.sum(qf * n_row, axis=1, keepdims=True) + jnp.sum(wmat, axis=1, keepdims=True)
    hh = num * (1.0 / jnp.maximum(jnp.abs(den), jnp.exp(-m_t)))
    m_new = m_t[L - 1:L, :]
    b_last = b_col[L - 1:L, :]
    decay = jnp.exp(b_last - b_col + ig_col - m_new)
    carry = jnp.exp(b_last + m_prev - m_new)
    kd = kf * decay
    C_new = carry * C + _dg(mxu(kd), vh, _TN)
    n_new = carry * n_row + jnp.sum(kd, axis=0, keepdims=True)
    return hh, C_new, n_new, m_new


def _head_out(hh, nw, og):
    hn = hh * lax.rsqrt(jnp.mean(hh * hh, axis=1, keepdims=True) + LN_EPS)
    return hn * nw * jax.nn.sigmoid(og)


def _mlstm_prompt_body(q_ref, k_ref, v_ref, og_ref, g_ref, gt_ref, nw_ref,
                       h_ref, c_ref, n_ref, m_ref, *, H, DK, DV):
    @pl.when(pl.program_id(1) == 0)
    def _():
        c_ref[...] = jnp.zeros_like(c_ref)
        n_ref[...] = jnp.zeros_like(n_ref)
        m_ref[...] = jnp.zeros_like(m_ref)

    g = g_ref[...]
    gt = gt_ref[...]
    for h in range(H):
        kq = slice(h * DK, (h + 1) * DK)
        vv = slice(h * DV, (h + 1) * DV)
        hh, C_new, n_new, m_new = _mlstm_head(
            q_ref[:, kq], k_ref[:, kq], v_ref[:, vv],
            g[:, h:h + 1], g[:, H + h:H + h + 1], gt[h:h + 1, :], gt[H + h:H + h + 1, :],
            c_ref[0, h], n_ref[0, h:h + 1, :], m_ref[0, :, h:h + 1])
        c_ref[0, h] = C_new
        n_ref[0, h:h + 1, :] = n_new
        m_ref[0, :, h:h + 1] = m_new
        h_ref[:, vv] = _head_out(hh, nw_ref[:, vv], og_ref[:, vv]).astype(BF16)


def _mlstm_prompt(qkv, og, g, gt, norm_w, B, T, H, DK, DV):
    L = MLSTM_CHUNK
    assert T % L == 0 and (2 * H * DK) % (H * DV) == 0
    nc = T // L
    v_blk = 2 * DK // DV
    row = lambda b, c: (b * nc + c, 0)
    return pl.pallas_call(
        functools.partial(_mlstm_prompt_body, H=H, DK=DK, DV=DV),
        grid=(B, nc),
        in_specs=[pl.BlockSpec((L, H * DK), row), pl.BlockSpec((L, H * DK), lambda b, c: (b * nc + c, 1)),
                  pl.BlockSpec((L, H * DV), lambda b, c: (b * nc + c, v_blk)), pl.BlockSpec((L, H * DV), row),
                  pl.BlockSpec((L, 2 * H), row),
                  pl.BlockSpec((2 * H, L), lambda b, c: (0, b * nc + c)),
                  pl.BlockSpec((1, H * DV), lambda b, c: (0, 0))],
        out_specs=[pl.BlockSpec((L, H * DV), row),
                   pl.BlockSpec((1, H, DK, DV), lambda b, c: (b, 0, 0, 0)),
                   pl.BlockSpec((1, H, DK), lambda b, c: (b, 0, 0)),
                   pl.BlockSpec((1, 1, H), lambda b, c: (b, 0, 0))],
        out_shape=[jax.ShapeDtypeStruct((B * T, H * DV), BF16),
                   jax.ShapeDtypeStruct((B, H, DK, DV), F32),
                   jax.ShapeDtypeStruct((B, H, DK), F32),
                   jax.ShapeDtypeStruct((B, 1, H), F32)],
        compiler_params=_cparams("parallel", "arbitrary"),
        name="mlstm_prompt",
    )(qkv, qkv, qkv, og, g, gt, norm_w)


def _mlstm_sample_body(q_ref, k_ref, v_ref, og_ref, g_ref, gt_ref, nw_ref, c0_ref, n0_ref, m0_ref,
                       h_ref, c_ref, n_ref, m_ref, *, H, DK, DV, TS):
    q = q_ref[...].astype(F32)
    k = k_ref[...].astype(F32)
    v = v_ref[...].astype(F32)
    og = og_ref[...]
    g = g_ref[...]
    gt = gt_ref[0]
    HT = H * TS
    r_idx = lax.broadcasted_iota(jnp.int32, (HT, HT), 0)
    c_idx = lax.broadcasted_iota(jnp.int32, (HT, HT), 1)
    same_head = (r_idx // TS) == (c_idx // TS)
    causal = same_head & (c_idx <= r_idx)
    upto = same_head & (r_idx <= c_idx)

    def stack(parts):
        return jnp.concatenate(parts, axis=0)

    def mxu(x):
        return x.astype(BF16)

    for bb in range(SAMPLE_BS):
        rows = slice(bb * TS, (bb + 1) * TS)
        hrows = [slice(h * TS, (h + 1) * TS) for h in range(H)]
        kq = [slice(h * DK, (h + 1) * DK) for h in range(H)]
        vv = [slice(h * DV, (h + 1) * DV) for h in range(H)]
        qf = stack([q[rows, kq[h]] for h in range(H)])
        kf = stack([k[rows, kq[h]] for h in range(H)])
        v_h = [v[rows, vv[h]].astype(BF16) for h in range(H)]
        ig_col = stack([g[rows, h:h + 1] for h in range(H)])
        lf_col = stack([g[rows, H + h:H + h + 1] for h in range(H)])
        ig_row = jnp.concatenate([gt[h:h + 1, rows] for h in range(H)], axis=1)
        lf_row = jnp.concatenate([gt[H + h:H + h + 1, rows] for h in range(H)], axis=1)
        m_prev = [m0_ref[0, bb:bb + 1, h:h + 1] for h in range(H)]
        m_prev_col = stack([jnp.broadcast_to(m, (TS, 1)) for m in m_prev])

        b_col = jnp.sum(jnp.where(causal, lf_row, 0.0), axis=1, keepdims=True)
        b_row = jnp.sum(jnp.where(upto, lf_col, 0.0), axis=0, keepdims=True)
        dmat = jnp.where(causal, b_col - b_row + ig_row, -jnp.inf)
        a_col = b_col + m_prev_col
        m_t = jnp.maximum(a_col, jnp.max(dmat, axis=1, keepdims=True))
        wmat = jnp.exp(dmat - m_t) * _dg(mxu(qf), mxu(kf), _NT)
        inter = jnp.exp(a_col - m_t)
        C0 = [c0_ref[bb, h] for h in range(H)]
        n0 = [n0_ref[bb, h:h + 1, :] for h in range(H)]
        q_c = stack([_dot(mxu(qf[hrows[h]]), mxu(C0[h])) for h in range(H)])
        n_rows = stack([jnp.broadcast_to(n, (TS, DK)) for n in n0])
        num = inter * q_c + _dot(mxu(wmat), stack(v_h))
        den = inter * jnp.sum(qf * n_rows, axis=1, keepdims=True) + jnp.sum(wmat, axis=1, keepdims=True)
        hh = num * (1.0 / jnp.maximum(jnp.abs(den), jnp.exp(-m_t)))

        m_new = [m_t[(h + 1) * TS - 1:(h + 1) * TS, :] for h in range(H)]
        b_last = [b_col[(h + 1) * TS - 1:(h + 1) * TS, :] for h in range(H)]
        m_new_col = stack([jnp.broadcast_to(m, (TS, 1)) for m in m_new])
        b_last_col = stack([jnp.broadcast_to(b, (TS, 1)) for b in b_last])
        kd = kf * jnp.exp(b_last_col - b_col + ig_col - m_new_col)

        og_s = stack([og[rows, vv[h]] for h in range(H)])
        nw_s = stack([jnp.broadcast_to(nw_ref[:, vv[h]], (TS, DV)) for h in range(H)])
        out = _head_out(hh, nw_s, og_s)
        for h in range(H):
            carry = jnp.exp(b_last[h] + m_prev[h] - m_new[h])
            c_ref[bb, h] = carry * C0[h] + _dg(mxu(kd[hrows[h]]), v_h[h], _TN)
            n_ref[bb, h:h + 1, :] = carry * n0[h] + jnp.sum(kd[hrows[h]], axis=0, keepdims=True)
            m_ref[0, bb:bb + 1, h:h + 1] = m_new[h]
            h_ref[rows, vv[h]] = out[hrows[h]].astype(BF16)


def _mlstm_sample(qkv, og, g, gt, norm_w, C0, n0, m0, NP, TS):
    Bs, H, DK, DV = C0.shape
    BS = SAMPLE_BS
    assert Bs % BS == 0 and NP % (BS * TS) == 0
    R = BS * TS
    off = NP // R
    row = lambda i: (off + i, 0)
    gt_s = gt[:, NP:].reshape(2 * H, Bs // BS, R).transpose(1, 0, 2)
    m0_s = m0.reshape(Bs // BS, BS, H)
    outs = pl.pallas_call(
        functools.partial(_mlstm_sample_body, H=H, DK=DK, DV=DV, TS=TS),
        grid=(Bs // BS,),
        in_specs=[pl.BlockSpec((R, H * DK), row), pl.BlockSpec((R, H * DK), lambda i: (off + i, 1)),
                  pl.BlockSpec((R, H * DV), lambda i: (off + i, 2 * DK // DV)), pl.BlockSpec((R, H * DV), row),
                  pl.BlockSpec((R, 2 * H), row),
                  pl.BlockSpec((1, 2 * H, R), lambda i: (i, 0, 0)),
                  pl.BlockSpec((1, H * DV), lambda i: (0, 0)),
                  pl.BlockSpec((BS, H, DK, DV), lambda i: (i, 0, 0, 0)),
                  pl.BlockSpec((BS, H, DK), lambda i: (i, 0, 0)),
                  pl.BlockSpec((1, BS, H), lambda i: (i, 0, 0))],
        out_specs=[pl.BlockSpec((R, H * DV), lambda i: (i, 0)),
                   pl.BlockSpec((BS, H, DK, DV), lambda i: (i, 0, 0, 0)),
                   pl.BlockSpec((BS, H, DK), lambda i: (i, 0, 0)),
                   pl.BlockSpec((1, BS, H), lambda i: (i, 0, 0))],
        out_shape=[jax.ShapeDtypeStruct((Bs * TS, H * DV), BF16),
                   jax.ShapeDtypeStruct((Bs, H, DK, DV), F32),
                   jax.ShapeDtypeStruct((Bs, H, DK), F32),
                   jax.ShapeDtypeStruct((Bs // BS, BS, H), F32)],
        compiler_params=_cparams("parallel"),
        name="mlstm_sample",
    )(qkv, qkv, qkv, og, g, gt_s, norm_w, C0, n0, m0_s)
    return outs[0], outs[1], outs[2], outs[3].reshape(Bs, H)


def _layer_norm(z, gain, bias):
    mu = jnp.mean(z, axis=1, keepdims=True)
    zc = z - mu
    var = jnp.mean(zc * zc, axis=1, keepdims=True)
    return zc * lax.rsqrt(var + LN_EPS) * gain + bias


def _proj_norm_body(ap_ref, as_ref, w_ref, *refs, alpha, npb, x_split):
    prompt = pl.program_id(0) < npb
    a = jnp.where(prompt, ap_ref[...], as_ref[...])
    if x_split:
        xp_ref, xs_ref, gain_ref, bias_ref, o_ref, ob_ref = refs
        x = jnp.where(prompt, xp_ref[...], xs_ref[...])
    else:
        x_ref, gain_ref, bias_ref, o_ref, ob_ref = refs
        x = x_ref[...]
    y = _layer_norm(alpha * x + _dot(a, w_ref[...]), gain_ref[...], bias_ref[...])
    o_ref[...] = y
    ob_ref[...] = y.astype(BF16)


def _proj_norm(a_prompt, a_sample, w, x, gain, bias, alpha):
    NP, K = a_prompt.shape
    NS = a_sample.shape[0]
    N, D = NP + NS, w.shape[1]
    tm = _pick(NS, ROW_TILES)
    assert NP % tm == 0
    npb = NP // tm
    first = lambda i: (jnp.minimum(i, npb - 1), 0)
    second = lambda i: (jnp.maximum(i - npb, 0), 0)
    x_split = isinstance(x, tuple)
    x_ops = tuple(x) if x_split else (x,)
    x_specs = ([pl.BlockSpec((tm, D), first), pl.BlockSpec((tm, D), second)] if x_split
               else [pl.BlockSpec((tm, D), lambda i: (i, 0))])
    return pl.pallas_call(
        functools.partial(_proj_norm_body, alpha=alpha, npb=npb, x_split=x_split),
        grid=(N // tm,),
        in_specs=[pl.BlockSpec((tm, K), first), pl.BlockSpec((tm, K), second),
                  pl.BlockSpec((K, D), lambda i: (0, 0))] + x_specs +
                 [pl.BlockSpec((1, D), lambda i: (0, 0)),
                  pl.BlockSpec((1, D), lambda i: (0, 0))],
        out_specs=[pl.BlockSpec((tm, D), lambda i: (i, 0)), pl.BlockSpec((tm, D), lambda i: (i, 0))],
        out_shape=[jax.ShapeDtypeStruct((N, D), F32), jax.ShapeDtypeStruct((N, D), BF16)],
        compiler_params=_cparams("parallel"),
        name="proj_norm",
    )(a_prompt, a_sample, w, *x_ops, gain[None, :], bias[None, :])


def _static_loop(n, fn):
    for r in range(n):
        fn(r)


def _rows_loop(n, fn, unroll=8):
    def body(r, c):
        fn(r)
        return c
    lax.fori_loop(0, n, body, 0, unroll=unroll)


def _combine_norm_body(slot_ref, slot_next_ref, gate_ref, x_ref, gain_ref, bias_ref, ys_hbm, oa_ref, ob_ref,
                       ybuf, sem, *, alpha, nblk, split_blk):
    i = pl.program_id(0)
    cur = lax.rem(i, 2)
    tm, D = x_ref.shape

    def row_copy(idx_ref, k, r, s):
        return pltpu.make_async_copy(ys_hbm.at[pl.ds(idx_ref[k, r], 1)], ybuf.at[s, k, pl.ds(r, 1)], sem.at[s])

    def start_block(idx_ref, s):
        for k in range(2):
            _rows_loop(tm, lambda r: row_copy(idx_ref, k, r, s).start())

    @pl.when(i == 0)
    def _():
        start_block(slot_ref, 0)

    @pl.when(i + 1 < nblk)
    def _():
        start_block(slot_next_ref, 1 - cur)

    for k in range(2):
        _static_loop(tm, lambda r: row_copy(slot_ref, k, r, cur).wait())
    g = gate_ref[...]
    moe = g[:, 0:1] * ybuf[cur, 0].reshape(tm, D) + g[:, 1:2] * ybuf[cur, 1].reshape(tm, D)
    y = _layer_norm(alpha * x_ref[...] + moe, gain_ref[...], bias_ref[...])
    if split_blk is None:
        oa_ref[...] = y
        ob_ref[...] = y.astype(BF16)
    else:
        @pl.when(i < split_blk)
        def _():
            oa_ref[...] = y

        @pl.when(i >= split_blk)
        def _():
            ob_ref[...] = y


def _combine_norm(x, ys, slot, gate_col, gain, bias, alpha, split_rows=None):
    N, D = x.shape
    tm = _pick(N if split_rows is None else math.gcd(split_rows, N - split_rows), (256, 128))
    nblk = N // tm
    idx_spec = lambda off: pl.BlockSpec((2, tm), lambda i: (0, jnp.minimum(i + off, nblk - 1)),
                                        memory_space=pltpu.SMEM)
    if split_rows is None:
        split_blk = None
        out_specs = [pl.BlockSpec((tm, D), lambda i: (i, 0)), pl.BlockSpec((tm, D), lambda i: (i, 0))]
        out_shape = [jax.ShapeDtypeStruct((N, D), F32), jax.ShapeDtypeStruct((N, D), BF16)]
    else:
        split_blk = split_rows // tm
        out_specs = [pl.BlockSpec((tm, D), lambda i: (jnp.minimum(i, split_blk - 1), 0)),
                     pl.BlockSpec((tm, D), lambda i: (jnp.maximum(i - split_blk, 0), 0))]
        out_shape = [jax.ShapeDtypeStruct((split_rows, D), F32), jax.ShapeDtypeStruct((N - split_rows, D), F32)]
    return pl.pallas_call(
        functools.partial(_combine_norm_body, alpha=alpha, nblk=nblk, split_blk=split_blk),
        grid=(nblk,),
        in_specs=[idx_spec(0), idx_spec(1),
                  pl.BlockSpec((tm, gate_col.shape[1]), lambda i: (i, 0)),
                  pl.BlockSpec((tm, D), lambda i: (i, 0)),
                  pl.BlockSpec((1, D), lambda i: (0, 0)),
                  pl.BlockSpec((1, D), lambda i: (0, 0)),
                  pl.BlockSpec(memory_space=pl.ANY)],
        out_specs=out_specs,
        out_shape=out_shape,
        scratch_shapes=[pltpu.VMEM((2, 2, tm) + ys.shape[1:], F32), pltpu.SemaphoreType.DMA((2,))],
        compiler_params=_cparams("arbitrary"),
        name="combine_norm",
    )(slot, slot, gate_col, x, gain[None, :], bias[None, :], ys)


def _router_body(x_ref, wt_ref, b_ref, eid_ref, gate_ref, *, E):
    logits = _dot3(wt_ref[...], x_ref[...], _NT)
    ex = jnp.exp(logits - jnp.max(logits, axis=0, keepdims=True))
    probs = ex / jnp.sum(ex, axis=0, keepdims=True)
    sel = probs + b_ref[...]
    per = E // N_GROUPS
    s = [sel[e:e + 1, :] for e in range(E)]
    p = [probs[e:e + 1, :] for e in range(E)]

    def top2_sum(vals):
        best = None
        for a in range(len(vals)):
            for b in range(a + 1, len(vals)):
                pair = vals[a] + vals[b]
                best = pair if best is None else jnp.maximum(best, pair)
        return best

    g_best = top2_sum(s[:per])
    g_idx = jnp.zeros_like(g_best, dtype=jnp.int32)
    for gi in range(1, N_GROUPS):
        score = top2_sum(s[gi * per:(gi + 1) * per])
        better = score > g_best
        g_idx = jnp.where(better, gi, g_idx)
        g_best = jnp.where(better, score, g_best)

    def in_group(vals, j):
        out = vals[j]
        for gi in range(1, N_GROUPS):
            out = jnp.where(g_idx == gi, vals[gi * per + j], out)
        return out

    sg = [in_group(s, j) for j in range(per)]
    pg = [in_group(p, j) for j in range(per)]
    i1 = jnp.zeros_like(g_idx)
    b1 = sg[0]
    for j in range(1, per):
        better = sg[j] > b1
        i1 = jnp.where(better, j, i1)
        b1 = jnp.where(better, sg[j], b1)
    i2 = jnp.zeros_like(g_idx)
    b2 = jnp.full_like(b1, -jnp.inf)
    for j in range(per):
        better = (i1 != j) & (sg[j] > b2)
        i2 = jnp.where(better, j, i2)
        b2 = jnp.where(better, sg[j], b2)
    p1 = pg[0]
    p2 = pg[0]
    for j in range(1, per):
        p1 = jnp.where(i1 == j, pg[j], p1)
        p2 = jnp.where(i2 == j, pg[j], p2)
    tot = p1 + p2
    eid_ref[...] = jnp.concatenate([g_idx * per + i1, g_idx * per + i2], axis=0)
    gates = jnp.concatenate([p1 / tot, p2 / tot, jnp.zeros((GATE_COLS - 2, p1.shape[1]), F32)], axis=0)
    gate_ref[...] = gates.T


def _router(x, w_router, b_router):
    N, D = x.shape
    E = w_router.shape[1]
    assert E % N_GROUPS == 0 and E // N_GROUPS >= 2
    tm = _pick(N, ROW_TILES)
    return pl.pallas_call(
        functools.partial(_router_body, E=E),
        grid=(N // tm,),
        in_specs=[pl.BlockSpec((tm, D), lambda i: (i, 0)),
                  pl.BlockSpec((E, D), lambda i: (0, 0)),
                  pl.BlockSpec((E, 1), lambda i: (0, 0))],
        out_specs=[pl.BlockSpec((2, tm), lambda i: (0, i)), pl.BlockSpec((tm, GATE_COLS), lambda i: (i, 0))],
        out_shape=[jax.ShapeDtypeStruct((2, N), jnp.int32), jax.ShapeDtypeStruct((N, GATE_COLS), F32)],
        compiler_params=_cparams("parallel"),
        name="router",
    )(x, w_router.T, b_router.astype(F32)[:, None])


def _plan_body(e_ref, slot_ref, cnt_ref, *, E, TM):
    e = e_ref[...]
    R = e.shape[0]
    upper = (lax.broadcasted_iota(jnp.int32, (LANES, LANES), 0)
             <= lax.broadcasted_iota(jnp.int32, (LANES, LANES), 1)).astype(BF16)
    below = (lax.broadcasted_iota(jnp.int32, (R, R), 1) < lax.broadcasted_iota(jnp.int32, (R, R), 0)).astype(BF16)
    lane = lax.broadcasted_iota(jnp.int32, (1, LANES), 1)
    slot = jnp.zeros(e.shape, F32)
    counts = jnp.zeros((1, LANES), F32)
    start = jnp.zeros((1, 1), F32)
    for ex in range(E):
        hit = e == ex
        within = _dot(hit.astype(BF16), upper)
        rows_before = _dot(below, within.astype(BF16))[:, LANES - 1:LANES]
        rank = within - 1.0 + rows_before
        count = rows_before[R - 1:R, :] + within[R - 1:R, LANES - 1:LANES]
        slot = jnp.where(hit, start + rank, slot)
        counts = jnp.where(lane == ex, count, counts)
        start = start + jnp.floor((count + (TM - 1)) * (1.0 / TM)) * TM
    slot_ref[...] = slot.astype(jnp.int32)
    cnt_ref[...] = jnp.broadcast_to(counts, cnt_ref.shape).astype(jnp.int32)


def _plan(eid, E, TM):
    N = eid.shape[1]
    A = 2 * N
    assert A % LANES == 0 and (TM & (TM - 1)) == 0
    R = A // LANES
    Rp = -(-R // LANES) * LANES
    e2d = jnp.pad(eid.reshape(R, LANES), ((0, Rp - R), (0, 0)), constant_values=-1)
    slot2d, cnt = pl.pallas_call(
        functools.partial(_plan_body, E=E, TM=TM),
        out_shape=[jax.ShapeDtypeStruct((Rp, LANES), jnp.int32), jax.ShapeDtypeStruct((8, LANES), jnp.int32)],
        compiler_params=pltpu.CompilerParams(vmem_limit_bytes=VMEM_LIMIT_BYTES),
        name="route_plan",
    )(e2d)
    return slot2d[:R].reshape(2, N), cnt[0, :E]


def _group_table(counts, TM, n_blocks):
    E = counts.shape[0]
    GB = MOE_GROUP_BLOCKS
    nblk = (counts + TM - 1) // TM
    blk_end = jnp.cumsum(nblk)
    blk_start = blk_end - nblk
    ngrp = (nblk + GB - 1) // GB
    grp_end = jnp.cumsum(ngrp)
    n_groups_max = E + n_blocks // GB
    g = jnp.arange(n_groups_max, dtype=jnp.int32)
    g_expert = jnp.minimum(jnp.sum((grp_end[None, :] <= g[:, None]).astype(jnp.int32), axis=1), E - 1)
    j = g - (grp_end - ngrp)[g_expert]
    g_start = blk_start[g_expert] + j * GB
    g_nblk = jnp.clip(nblk[g_expert] - j * GB, 0, GB)
    live = g < grp_end[-1]
    g_nblk = jnp.where(live, g_nblk, 0)
    meta = jnp.stack([grp_end[-1], blk_end[-1]]).astype(jnp.int32)
    pad_start = blk_start * TM + counts
    pad_count = nblk * TM - counts
    return (g_expert.astype(jnp.int32), g_start.astype(jnp.int32), g_nblk.astype(jnp.int32), meta,
            pad_start.astype(jnp.int32), pad_count.astype(jnp.int32))


def _pack_halves(x):
    half = x.shape[1] // 2
    bits = lax.bitcast_convert_type(x.astype(BF16).astype(F32), jnp.uint32)
    return (bits[:, :half] >> 16) | bits[:, half:]


def _unpack_halves(p):
    lo = lax.bitcast_convert_type(p << 16, F32).astype(BF16)
    hi = lax.bitcast_convert_type(p & jnp.uint32(0xFFFF0000), F32).astype(BF16)
    return lo, hi


def _dispatch_body(pad_start_ref, pad_count_ref, meta_ref, slot_ref, x_ref, xs_hbm, pbuf, zbuf, sem, zsem,
                   *, nblk, E, TM, n_blocks):
    i = pl.program_id(0)
    cur = lax.rem(i, 2)
    tm = x_ref.shape[0]

    def row_copy(k, r, s):
        return pltpu.make_async_copy(pbuf.at[s, pl.ds(r, 1)], xs_hbm.at[pl.ds(slot_ref[k, r], 1)], sem.at[s])

    def wait_block(s):
        for k in range(2):
            _static_loop(tm, lambda r: row_copy(k, r, s).wait())

    def zero_copy(row):
        return pltpu.make_async_copy(zbuf.at[pl.ds(0, 1)], xs_hbm.at[pl.ds(row, 1)], zsem.at[0])

    @pl.when(i == 0)
    def _():
        zbuf[...] = jnp.zeros_like(zbuf)
        for ex in range(E):
            _rows_loop(pad_count_ref[ex], lambda j: zero_copy(pad_start_ref[ex] + j).start(), unroll=1)
        tail0 = meta_ref[1] * TM
        ntail = n_blocks * TM - tail0
        _rows_loop(ntail, lambda j: zero_copy(tail0 + j).start(), unroll=1)
        for ex in range(E):
            _rows_loop(pad_count_ref[ex], lambda j: zero_copy(pad_start_ref[ex] + j).wait(), unroll=1)
        _rows_loop(ntail, lambda j: zero_copy(tail0 + j).wait(), unroll=1)

    @pl.when(i >= 2)
    def _():
        wait_block(cur)

    pbuf[cur] = _pack_halves(x_ref[...]).reshape(pbuf.shape[1:])
    for k in range(2):
        _rows_loop(tm, lambda r: row_copy(k, r, cur).start())

    @pl.when(i == nblk - 1)
    def _():
        wait_block(cur)
        if nblk >= 2:
            wait_block(1 - cur)


def _dispatch(x, slot, pad_start, pad_count, meta, TM, n_blocks):
    N, D = x.shape
    E = pad_start.shape[0]
    tm = _pick(N, (256, 128))
    nblk = N // tm
    row_tile = (D // 2 // LANES, LANES)
    grid_spec = pltpu.PrefetchScalarGridSpec(
        num_scalar_prefetch=3,
        grid=(nblk,),
        in_specs=[pl.BlockSpec((2, tm), lambda i, *_: (0, i), memory_space=pltpu.SMEM),
                  pl.BlockSpec((tm, D), lambda i, *_: (i, 0))],
        out_specs=pl.BlockSpec(memory_space=pl.ANY),
        scratch_shapes=[pltpu.VMEM((2, tm) + row_tile, jnp.uint32), pltpu.VMEM((1,) + row_tile, jnp.uint32),
                        pltpu.SemaphoreType.DMA((2,)), pltpu.SemaphoreType.DMA((1,))],
    )
    return pl.pallas_call(
        functools.partial(_dispatch_body, nblk=nblk, E=E, TM=TM, n_blocks=n_blocks),
        grid_spec=grid_spec,
        out_shape=jax.ShapeDtypeStruct((n_blocks * TM,) + row_tile, jnp.uint32),
        compiler_params=_cparams("arbitrary"),
        name="moe_dispatch",
    )(pad_start, pad_count, meta, slot, x)


def _experts_body(ge_ref, gs_ref, gn_ref, meta_ref, xs_hbm, wg_hbm, wu_hbm, wd_hbm, ys_hbm,
                  xraw, xg, acc, stage_g, stage_u, stage_d, wgu, wdb, obuf, xsem, wsem, osem,
                  *, layer, chunks, TM, n_blocks):
    g = pl.program_id(0)
    n_groups = meta_ref[0]
    half = xg.shape[2]
    n_chunks = len(chunks)

    def x_copy(b0, rb):
        return pltpu.make_async_copy(xs_hbm.at[pl.ds((b0 + rb) * TM, TM)], xraw.at[rb], xsem.at[0])

    def w_copies(ex, c, s):
        f0, fw = chunks[c]
        return (pltpu.make_async_copy(wg_hbm.at[layer, ex, :, pl.ds(f0, fw)], stage_g.at[s, :, pl.ds(0, fw)],
                                      wsem.at[s, 0]),
                pltpu.make_async_copy(wu_hbm.at[layer, ex, :, pl.ds(f0, fw)], stage_u.at[s, :, pl.ds(0, fw)],
                                      wsem.at[s, 1]),
                pltpu.make_async_copy(wd_hbm.at[layer, ex, pl.ds(f0, fw), :], stage_d.at[s, pl.ds(0, fw), :],
                                      wsem.at[s, 2]))

    def out_copy(b0, rb, s):
        return pltpu.make_async_copy(obuf.at[s], ys_hbm.at[pl.ds((b0 + rb) * TM, TM)], osem.at[s])

    @pl.when(g < n_groups)
    def _():
        ex = ge_ref[g]
        b0 = gs_ref[g]
        nb = gn_ref[g]
        n_pairs = lax.shift_right_logical(nb, 1)
        odd = (nb & 1) == 1
        D = acc.shape[2]
        _rows_loop(nb, lambda rb: x_copy(b0, rb).start(), unroll=1)

        def start_weights(expert, c, s):
            for idx, cp in enumerate(w_copies(expert, c, s)):
                cp.start(priority=idx % 2)

        @pl.when(g == 0)
        def _():
            start_weights(ex, 0, 0)

        _rows_loop(nb, lambda rb: x_copy(b0, rb).wait(), unroll=1)

        def untile(rb):
            xg[rb] = xraw[rb].reshape(TM, half)

        _rows_loop(nb, untile, unroll=1)

        for c, (f0, fw) in enumerate(chunks):
            s = c % 2
            for cp in w_copies(ex, c, s):
                cp.wait()
            if c + 1 < n_chunks:
                start_weights(ex, c + 1, 1 - s)
            else:
                @pl.when(g + 1 < n_groups)
                def _():
                    start_weights(ge_ref[g + 1], 0, 1 - s)
            wgu[:, 0:fw] = stage_g[s, :, 0:fw].astype(BF16)
            wgu[:, fw:2 * fw] = stage_u[s, :, 0:fw].astype(BF16)
            wdb[0:fw, :] = stage_d[s, 0:fw, :].astype(BF16)

            last = c + 1 == n_chunks

            def work(rb0, n, c=c, fw=fw, last=last):
                rows = pl.ds(rb0, n)
                xl, xh = _unpack_halves(xg[rows].reshape(n * TM, half))
                gu = _dot(xl, wgu[0:half, 0:2 * fw]) + _dot(xh, wgu[half:2 * half, 0:2 * fw])
                hidden = jax.nn.silu(gu[:, 0:fw]) * gu[:, fw:2 * fw]
                part = _dot(hidden.astype(BF16), wdb[0:fw, :]).reshape(n, TM, D)
                if c == 0:
                    acc[rows] = part
                elif not last:
                    acc[rows] = acc[rows] + part
                else:
                    fin = acc[rows] + part
                    for t in range(n):
                        obuf[t] = fin[t].reshape(obuf.shape[1:])
                        out_copy(b0, rb0 + t, t).start()

            def pair(i, carry, last=last):
                if last:
                    @pl.when(i >= 1)
                    def _():
                        out_copy(b0, 0, 0).wait()
                        out_copy(b0, 0, 1).wait()
                work(2 * i, 2)
                return carry

            lax.fori_loop(0, n_pairs, pair, 0)

            @pl.when(odd)
            def _(last=last):
                if last:
                    @pl.when(n_pairs >= 1)
                    def _():
                        out_copy(b0, 0, 0).wait()
                work(nb - 1, 1)

        out_copy(b0, 0, 0).wait()

        @pl.when(n_pairs >= 1)
        def _():
            out_copy(b0, 0, 1).wait()

    @pl.when(g == pl.num_programs(0) - 1)
    def _():
        used = meta_ref[1]
        obuf[0] = jnp.zeros(obuf.shape[1:], F32)
        _rows_loop(n_blocks - used, lambda j: out_copy(used, j, 0).start(), unroll=1)
        _rows_loop(n_blocks - used, lambda j: out_copy(used, j, 0).wait(), unroll=1)


def _experts(xs, g_expert, g_start, g_nblk, meta, w_gate, w_up, w_down, layer, TM, n_blocks):
    _, E, D, F = w_gate.shape
    GB = MOE_GROUP_BLOCKS
    FC = MOE_FC
    chunks = tuple((f0, min(FC, F - f0)) for f0 in range(0, F, FC))
    assert len(chunks) % 2 == 0 and all(fw % LANES == 0 for _, fw in chunks)
    out_tile = (D // LANES, LANES)
    grid_spec = pltpu.PrefetchScalarGridSpec(
        num_scalar_prefetch=4,
        grid=(g_expert.shape[0],),
        in_specs=[pl.BlockSpec(memory_space=pl.ANY)] * 4,
        out_specs=pl.BlockSpec(memory_space=pl.ANY),
        scratch_shapes=[pltpu.VMEM((GB, TM) + xs.shape[1:], jnp.uint32),
                        pltpu.VMEM((GB, TM, D // 2), jnp.uint32), pltpu.VMEM((GB, TM, D), F32),
                        pltpu.VMEM((2, D, FC), F32), pltpu.VMEM((2, D, FC), F32), pltpu.VMEM((2, FC, D), F32),
                        pltpu.VMEM((D, 2 * FC), BF16), pltpu.VMEM((FC, D), BF16),
                        pltpu.VMEM((2, TM) + out_tile, F32),
                        pltpu.SemaphoreType.DMA((1,)), pltpu.SemaphoreType.DMA((2, 3)),
                        pltpu.SemaphoreType.DMA((2,))],
    )
    return pl.pallas_call(
        functools.partial(_experts_body, layer=layer, chunks=chunks, TM=TM, n_blocks=n_blocks),
        grid_spec=grid_spec,
        out_shape=jax.ShapeDtypeStruct((n_blocks * TM,) + out_tile, F32),
        compiler_params=_cparams("arbitrary"),
        name="moe_experts",
    )(g_expert, g_start, g_nblk, meta, xs, w_gate, w_up, w_down)


def _moe_layer(x, w_router, b_router, w_gate, w_up, w_down, layer, gain, bias, alpha, split_rows=None):
    N = x.shape[0]
    E = w_router.shape[1]
    TM = MOE_TM
    n_blocks = -(-(2 * N + E * (TM - 1)) // TM)
    eid, gate_col = _router(x, w_router, b_router)
    slot, counts = _plan(eid, E, TM)
    g_expert, g_start, g_nblk, meta, pad_start, pad_count = _group_table(counts, TM, n_blocks)
    xs = _dispatch(x, slot, pad_start, pad_count, meta, TM, n_blocks)
    ys = _experts(xs, g_expert, g_start, g_nblk, meta, w_gate, w_up, w_down, layer, TM, n_blocks)
    return _combine_norm(x, ys, slot, gate_col, gain, bias, alpha, split_rows)


def _pair_attention(q_pairs, kc, vc, bias, sinks_lo, sinks_hi):
    P = len(q_pairs)
    R, lanes = q_pairs[0].shape
    lo = lax.broadcasted_iota(jnp.int32, (R, lanes), 1) < lanes // 2
    zero = jnp.zeros((R, lanes), BF16)
    qs = jnp.concatenate([jnp.where(lo, qp, zero) for qp in q_pairs]
                         + [jnp.where(lo, zero, qp) for qp in q_pairs], axis=0)
    sink_col = jnp.concatenate([jnp.full((R, 1), s, F32) for s in list(sinks_lo) + list(sinks_hi)], axis=0)
    s = _dg(qs, kc, _NT) + bias
    mx = jnp.maximum(jnp.max(s, axis=1, keepdims=True), sink_col)
    e = jnp.exp(s - mx)
    den = jnp.sum(e, axis=1, keepdims=True) + jnp.exp(sink_col - mx)
    o = _dot(e.astype(BF16), vc) * (1.0 / den)
    return [jnp.where(lo, o[p * R:(p + 1) * R], o[(P + p) * R:(P + p + 1) * R]) for p in range(P)]


def _swa_prompt_body(sink_ref, q_ref, kp_ref, kc_ref, vp_ref, vc_ref, o_ref, *, KVH, G):
    W = q_ref.shape[0]
    P = G // 2
    first = pl.program_id(1) == 0
    r = lax.broadcasted_iota(jnp.int32, (W, 2 * W), 0)
    kj = lax.broadcasted_iota(jnp.int32, (W, 2 * W), 1)
    diff = r + W - kj
    valid = (diff >= 0) & (diff < W) & (jnp.logical_not(first) | (kj >= W))
    bias = jnp.concatenate([jnp.where(valid, 0.0, -jnp.inf)] * G, axis=0)
    for g in range(KVH):
        ks = slice(g * LANES, (g + 1) * LANES)
        kc = jnp.concatenate([kp_ref[:, ks], kc_ref[:, ks]], axis=0)
        vc = jnp.concatenate([vp_ref[:, ks], vc_ref[:, ks]], axis=0)
        cols = [slice((g * P + p) * LANES, (g * P + p + 1) * LANES) for p in range(P)]
        outs = _pair_attention([q_ref[:, c] for c in cols], kc, vc, bias,
                               [sink_ref[g * G + 2 * p] for p in range(P)],
                               [sink_ref[g * G + 2 * p + 1] for p in range(P)])
        for c, o in zip(cols, outs):
            o_ref[:, c] = o.astype(BF16)


def _swa_prompt(q, kdup, vdup, sinks, B, T, KVH):
    Hq = sinks.shape[0]
    W = WINDOW
    assert T % W == 0 and kdup.shape[1] == KVH * LANES and q.shape[1] == Hq // 2 * LANES
    nb = T // W
    cur = lambda b, n: (b * nb + n, 0)
    prev = lambda b, n: (b * nb + jnp.maximum(n - 1, 0), 0)
    return pl.pallas_call(
        functools.partial(_swa_prompt_body, KVH=KVH, G=Hq // KVH),
        grid=(B, nb),
        in_specs=[pl.BlockSpec(memory_space=pltpu.SMEM),
                  pl.BlockSpec((W, q.shape[1]), cur),
                  pl.BlockSpec((W, KVH * LANES), prev), pl.BlockSpec((W, KVH * LANES), cur),
                  pl.BlockSpec((W, KVH * LANES), prev), pl.BlockSpec((W, KVH * LANES), cur)],
        out_specs=pl.BlockSpec((W, q.shape[1]), cur),
        out_shape=jax.ShapeDtypeStruct((B * T, q.shape[1]), BF16),
        compiler_params=_cparams("parallel", "parallel"),
        name="swa_prompt",
    )(sinks, q, kdup, kdup, vdup, vdup)


def _swa_sample_body(sink_ref, q_ref, ck_ref, cv_ref, kn_ref, vn_ref, o_ref, *, KVH, G, TS, WB, K):
    P = G // 2
    NPAIR = KVH * P
    HD = LANES // 2
    q = q_ref[...].astype(F32)
    r = lax.broadcasted_iota(jnp.int32, (2 * NPAIR * TS, KVH * K), 0)
    c = lax.broadcasted_iota(jnp.int32, (2 * NPAIR * TS, KVH * K), 1)
    diff = WB + r % TS - c % K
    own = (r % (NPAIR * TS)) // (P * TS) == c // K
    bias = jnp.where(own & (diff >= 0) & (diff < WINDOW), 0.0, -jnp.inf)
    cols = [slice(p * LANES, (p + 1) * LANES) for p in range(NPAIR)]
    sinks_lo = [sink_ref[2 * p] for p in range(NPAIR)]
    sinks_hi = [sink_ref[2 * p + 1] for p in range(NPAIR)]
    pad = jnp.zeros((K - WB - TS, HD), F32)

    def stack_heads(window, new):
        parts = []
        for g in range(KVH):
            hs = slice(g * HD, (g + 1) * HD)
            one = jnp.concatenate([window[:, hs], new[:, hs], pad], axis=0)
            parts.append(jnp.concatenate([one, one], axis=1))
        return jnp.concatenate(parts, axis=0).astype(BF16)

    for bb in range(SAMPLE_BS):
        rows = slice(bb * TS, (bb + 1) * TS)
        kc = stack_heads(ck_ref[bb], kn_ref[rows, :])
        vc = stack_heads(cv_ref[bb], vn_ref[rows, :])
        outs = _pair_attention([q[rows, cc].astype(BF16) for cc in cols], kc, vc, bias, sinks_lo, sinks_hi)
        for cc, o in zip(cols, outs):
            o_ref[rows, cc] = o.astype(BF16)


def _swa_sample(q, cache_k, cache_v, k_new, v_new, sinks, NP, TS, KVH):
    Bs, WB, nkv = cache_k.shape
    Hq = sinks.shape[0]
    BS = SAMPLE_BS
    R = BS * TS
    off = NP // R
    row = lambda i: (off + i, 0)
    K = -(-(WB + TS) // LANES) * LANES
    return pl.pallas_call(
        functools.partial(_swa_sample_body, KVH=KVH, G=Hq // KVH, TS=TS, WB=WB, K=K),
        grid=(Bs // BS,),
        in_specs=[pl.BlockSpec(memory_space=pltpu.SMEM),
                  pl.BlockSpec((R, q.shape[1]), row),
                  pl.BlockSpec((BS, WB, nkv), lambda i: (i, 0, 0)),
                  pl.BlockSpec((BS, WB, nkv), lambda i: (i, 0, 0)),
                  pl.BlockSpec((R, nkv), row),
                  pl.BlockSpec((R, nkv), row)],
        out_specs=pl.BlockSpec((R, q.shape[1]), lambda i: (i, 0)),
        out_shape=jax.ShapeDtypeStruct((Bs * TS, q.shape[1]), BF16),
        compiler_params=_cparams("parallel"),
        name="swa_sample",
    )(sinks, q, cache_k, cache_v, k_new, v_new)


def _rope_tables(pos, head_dim, lanes):
    half = head_dim // 2
    inv = ROPE_THETA ** (-jnp.arange(half, dtype=F32) / half)
    ang = pos.astype(F32)[:, None] * inv[None, :]
    reps = lanes // half
    return jnp.tile(jnp.cos(ang), (1, reps)), jnp.tile(jnp.sin(ang), (1, reps))


def kernel(x_prompt, x_sample, state_C, state_n, state_m, cache_k, cache_v, w_in_a, b_i_a, b_f_a, norm_a, w_out_a,
           w_kv, w_q_b, sinks_b, w_out_b, w_router, b_router, w_gate_e, w_up_e, w_down_e, ln_g, ln_b):
    B, T, D = x_prompt.shape
    Bs, TS, _ = x_sample.shape
    depth = ln_g.shape[0]
    assert depth == 2 and w_in_a.shape[0] == 1 and w_q_b.shape[0] == 1
    alpha = (2 * depth) ** 0.25
    NP, NS = B * T, Bs * TS
    H = b_i_a.shape[1]
    DK, DV = state_C.shape[3], state_C.shape[4]
    WB, KVH, HD = cache_k.shape[1], cache_k.shape[2], cache_k.shape[3]
    E = w_router.shape[1]

    x0_p = x_prompt.reshape(NP, D).astype(F32)
    x0_s = x_sample.reshape(NS, D).astype(F32)

    w_in = w_in_a.astype(F32)
    nqk, nv = H * DK, H * DV
    w_gates = w_in[0, :, 2 * nqk + nv + D:]
    g, gt, x0b = _mlstm_gates(x0_p, x0_s, w_gates, b_i_a[0], b_f_a[0])
    qkv_scale = jnp.concatenate([jnp.ones((nqk,), F32), jnp.full((nqk,), DK ** -0.5, F32), jnp.ones((nv,), F32)])
    qkv = _matmul(x0b, w_in, qkv_scale[None, :], BF16, col0=0, ncols=2 * nqk + nv)
    og = _matmul(x0b, w_in, jnp.ones((1, D), F32), F32, col0=2 * nqk + nv, ncols=D)
    norm_w = norm_a[0].astype(F32)[None, :]
    h_p, C_p, n_p, m_p = _mlstm_prompt(qkv, og, g, gt, norm_w, B, T, H, DK, DV)
    h_s, C_s, n_s, m_s = _mlstm_sample(qkv, og, g, gt, norm_w, state_C[0].astype(F32), state_n[0].astype(F32),
                                       state_m[0].astype(F32), NP, TS)
    x1, _ = _proj_norm(h_p, h_s, w_out_a[0].astype(BF16), (x0_p, x0_s), ln_g[0, 0], ln_b[0, 0], alpha)
    x2, x2b = _moe_layer(x1, w_router, b_router, w_gate_e, w_up_e, w_down_e, 0, ln_g[0, 1], ln_b[0, 1], alpha)

    pos = jnp.concatenate([jnp.tile(jnp.arange(T, dtype=jnp.int32), B),
                           jnp.tile(PAST_LEN + jnp.arange(TS, dtype=jnp.int32), Bs)])
    cos, sin = _rope_tables(pos, HD, 128)
    nkv = KVH * HD
    k_new = _matmul_rope(x2b, w_kv[:, :nkv].astype(BF16), cos, sin, HD, F32)
    v_new = _matmul(x2b, w_kv[:, nkv:].astype(BF16), jnp.ones((1, nkv), F32), F32)
    assert 2 * HD == LANES
    qb = _matmul_rope(x2b, w_q_b[0].astype(BF16), cos, sin, HD, BF16, scale=HD ** -0.5)

    def dup_heads(a):
        a4 = a.reshape(a.shape[:-1] + (KVH, HD)).astype(BF16)
        return jnp.concatenate([a4, a4], axis=-1).reshape(a.shape[:-1] + (KVH * LANES,))

    o_p = _swa_prompt(qb, dup_heads(k_new), dup_heads(v_new), sinks_b[0].astype(F32), B, T, KVH)
    ck = cache_k.reshape(Bs, WB, nkv).astype(F32)
    cv = cache_v.reshape(Bs, WB, nkv).astype(F32)
    o_s = _swa_sample(qb, ck, cv, k_new, v_new, sinks_b[0].astype(F32), NP, TS, KVH)
    x3, _ = _proj_norm(o_p, o_s, w_out_b[0].astype(BF16), x2, ln_g[1, 0], ln_b[1, 0], alpha)
    y_p, y_s = _moe_layer(x3, w_router, b_router, w_gate_e, w_up_e, w_down_e, 1, ln_g[1, 1], ln_b[1, 1], alpha,
                          split_rows=NP)
    y_prompt = y_p.reshape(B, T, D)
    y_sample = y_s.reshape(Bs, TS, D)
    k_p = k_new[:NP].reshape(B, T, KVH, HD)[:, -WB:]
    v_p = v_new[:NP].reshape(B, T, KVH, HD)[:, -WB:]
    k_s = jnp.concatenate([ck[:, TS:], k_new[NP:].reshape(Bs, TS, nkv)], axis=1).reshape(Bs, WB, KVH, HD)
    v_s = jnp.concatenate([cv[:, TS:], v_new[NP:].reshape(Bs, TS, nkv)], axis=1).reshape(Bs, WB, KVH, HD)
    return (y_prompt, y_sample, C_p[None], n_p[None], m_p.reshape(1, B, H), k_p, v_p,
            C_s[None], n_s[None], m_s[None], k_s, v_s)
```

```python
import functools
import math

import jax
import jax.numpy as jnp
from jax import lax
from jax.experimental import pallas as pl
from jax.experimental.pallas import tpu as pltpu

F32 = jnp.float32
BF16 = jnp.bfloat16

GATE_CAP = 15.0
LN_EPS = 1e-5
ROPE_THETA = 10000.0
WINDOW = 128
PAST_LEN = 8192
N_GROUPS = 4

VMEM_LIMIT_BYTES = 60 * 1024 * 1024
MLSTM_CHUNK = 256
MOE_TM = 256
MOE_GROUP_BLOCKS = 6
MOE_FC = 256
GATE_COLS = 8
LANES = 128
SAMPLE_BS = 4
ROW_TILES = (512, 256, 128)
MATMUL_ROW_TILES = (1024,) + ROW_TILES

_NT = (((1,), (1,)), ((), ()))
_TN = (((0,), (0,)), ((), ()))


def _pick(n, cands):
    for c in cands:
        if n % c == 0:
            return c
    raise ValueError(f"no tile in {cands} divides {n}")


def _cparams(*sem):
    return pltpu.CompilerParams(dimension_semantics=sem, vmem_limit_bytes=VMEM_LIMIT_BYTES)


def _dot(a, b):
    return jnp.dot(a, b, preferred_element_type=F32)


def _dg(a, b, dims):
    return lax.dot_general(a, b, dims, preferred_element_type=F32)


def _mm_scale_body(a_ref, b_ref, s_ref, o_ref):
    o_ref[...] = (_dot(a_ref[...], b_ref[...].astype(BF16)) * s_ref[...]).astype(o_ref.dtype)


def _matmul(a, b, col_scale, out_dtype, col0=0, ncols=None):
    M, K = a.shape
    N = b.shape[-1] if ncols is None else ncols
    tm = _pick(M, MATMUL_ROW_TILES)
    tn = _pick(math.gcd(N, col0) if col0 else N, (1024, 512, 256))
    j0 = col0 // tn
    return pl.pallas_call(
        _mm_scale_body,
        grid=(M // tm, N // tn),
        in_specs=[pl.BlockSpec((tm, K), lambda i, j: (i, 0)),
                  pl.BlockSpec((None, K, tn), lambda i, j: (0, 0, j0 + j)),
                  pl.BlockSpec((1, tn), lambda i, j: (0, j))],
        out_specs=pl.BlockSpec((tm, tn), lambda i, j: (i, j)),
        out_shape=jax.ShapeDtypeStruct((M, N), out_dtype),
        compiler_params=_cparams("parallel", "parallel"),
        name="matmul",
    )(a, b if b.ndim == 3 else b[None], col_scale)


def _mm_rope_body(a_ref, b_ref, cos_ref, sin_ref, o_ref, *, half, scale):
    acc = _dot(a_ref[...], b_ref[...])
    tn = acc.shape[1]
    reps = tn // cos_ref.shape[1]
    cos = jnp.concatenate([cos_ref[...]] * reps, axis=1)
    sin = jnp.concatenate([sin_ref[...]] * reps, axis=1)
    lane = lax.broadcasted_iota(jnp.int32, acc.shape, 1)
    partner = jnp.where(lane % (2 * half) < half, -pltpu.roll(acc, tn - half, 1), pltpu.roll(acc, half, 1))
    o_ref[...] = ((acc * cos + partner * sin) * scale).astype(o_ref.dtype)


def _matmul_rope(a, b, cos, sin, head_dim, out_dtype, scale=1.0):
    M, K = a.shape
    N = b.shape[1]
    tm = _pick(M, MATMUL_ROW_TILES)
    tn = _pick(N, (1024, 512, 256))
    assert math.frexp(scale)[0] == 0.5
    return pl.pallas_call(
        functools.partial(_mm_rope_body, half=head_dim // 2, scale=scale),
        grid=(M // tm, N // tn),
        in_specs=[pl.BlockSpec((tm, K), lambda i, j: (i, 0)),
                  pl.BlockSpec((K, tn), lambda i, j: (0, j)),
                  pl.BlockSpec((tm, cos.shape[1]), lambda i, j: (i, 0)),
                  pl.BlockSpec((tm, sin.shape[1]), lambda i, j: (i, 0))],
        out_specs=pl.BlockSpec((tm, tn), lambda i, j: (i, j)),
        out_shape=jax.ShapeDtypeStruct((M, N), out_dtype),
        compiler_params=_cparams("parallel", "parallel"),
        name="matmul_rope",
    )(a, b, cos, sin)


def _split_bf16(x):
    hi = x.astype(BF16)
    lo = (x - hi.astype(F32)).astype(BF16)
    return hi, lo


def _dot3(a, b, dims):
    ah, al = _split_bf16(a)
    bh, bl = _split_bf16(b)
    return _dg(ah, bh, dims) + _dg(al, bh, dims) + _dg(ah, bl, dims)


def _softcap(z):
    return GATE_CAP * jnp.tanh(z / GATE_CAP)


def _log_sigmoid(z):
    return jnp.minimum(z, 0.0) - jnp.log1p(jnp.exp(-jnp.abs(z)))


def _gates_body(xp_ref, xs_ref, w_ref, wt_ref, brow_ref, bcol_ref, g_ref, gt_ref, xb_ref, *, H, npb):
    x = jnp.where(pl.program_id(0) < npb, xp_ref[...], xs_ref[...])
    xb_ref[...] = x.astype(BF16)
    pre = _dot3(x, w_ref[...], (((1,), (0,)), ((), ()))) + brow_ref[...]
    pre_t = _dot3(wt_ref[...], x, _NT) + bcol_ref[...]
    z = _softcap(pre)
    zt = _softcap(pre_t)
    is_in = lax.broadcasted_iota(jnp.int32, z.shape, 1) < H
    is_in_t = lax.broadcasted_iota(jnp.int32, zt.shape, 0) < H
    g_ref[...] = jnp.where(is_in, z, _log_sigmoid(z))
    gt_ref[...] = jnp.where(is_in_t, zt, _log_sigmoid(zt))


def _mlstm_gates(x_prompt, x_sample, w_gates, b_i, b_f):
    NP, D = x_prompt.shape
    NS = x_sample.shape[0]
    N = NP + NS
    H = b_i.shape[0]
    tm = _pick(NS, ROW_TILES)
    assert NP % tm == 0
    npb = NP // tm
    bias = jnp.concatenate([b_i, b_f]).astype(F32)
    return pl.pallas_call(
        functools.partial(_gates_body, H=H, npb=npb),
        grid=(N // tm,),
        in_specs=[pl.BlockSpec((tm, D), lambda i: (jnp.minimum(i, npb - 1), 0)),
                  pl.BlockSpec((tm, D), lambda i: (jnp.maximum(i - npb, 0), 0)),
                  pl.BlockSpec((D, 2 * H), lambda i: (0, 0)),
                  pl.BlockSpec((2 * H, D), lambda i: (0, 0)),
                  pl.BlockSpec((1, 2 * H), lambda i: (0, 0)),
                  pl.BlockSpec((2 * H, 1), lambda i: (0, 0))],
        out_specs=[pl.BlockSpec((tm, 2 * H), lambda i: (i, 0)),
                   pl.BlockSpec((2 * H, tm), lambda i: (0, i)),
                   pl.BlockSpec((tm, D), lambda i: (i, 0))],
        out_shape=[jax.ShapeDtypeStruct((N, 2 * H), F32), jax.ShapeDtypeStruct((2 * H, N), F32),
                   jax.ShapeDtypeStruct((N, D), BF16)],
        compiler_params=_cparams("parallel"),
        name="mlstm_gates",
    )(x_prompt, x_sample, w_gates, w_gates.T, bias[None, :], bias[:, None])


def _mlstm_head(qh, kh, vh, ig_col, lf_col, ig_row, lf_row, C, n_row, m_prev):
    L = qh.shape[0]
    t_idx = lax.broadcasted_iota(jnp.int32, (L, L), 0)
    s_idx = lax.broadcasted_iota(jnp.int32, (L, L), 1)
    causal = s_idx <= t_idx

    def mxu(x):
        return x.astype(BF16)

    b_col = jnp.sum(jnp.where(causal, lf_row, 0.0), axis=1, keepdims=True)
    b_row = jnp.sum(jnp.where(t_idx <= s_idx, lf_col, 0.0), axis=0, keepdims=True)
    dmat = jnp.where(causal, b_col - b_row + ig_row, -jnp.inf)
    a_col = b_col + m_prev
    m_t = jnp.maximum(a_col, jnp.max(dmat, axis=1, keepdims=True))
    wmat = jnp.exp(dmat - m_t) * _dg(qh, kh, _NT)
    inter = jnp.exp(a_col - m_t)
    qf = qh.astype(F32)
    kf = kh.astype(F32)
    num = inter * _dot(qh, mxu(C)) + _dot(mxu(wmat), vh)
    den = inter * jnp.sum(qf * n_row, axis=1, keepdims=True) + jnp.sum(wmat, axis=1, keepdims=True)
    hh = num * (1.0 / jnp.maximum(jnp.abs(den), jnp.exp(-m_t)))
    m_new = m_t[L - 1:L, :]
    b_last = b_col[L - 1:L, :]
    decay = jnp.exp(b_last - b_col + ig_col - m_new)
    carry = jnp.exp(b_last + m_prev - m_new)
    kd = kf * decay
    C_new = carry * C + _dg(mxu(kd), vh, _TN)
    n_new = carry * n_row + jnp.sum(kd, axis=0, keepdims=True)
    return hh, C_new, n_new, m_new


def _head_out(hh, nw, og):
    hn = hh * lax.rsqrt(jnp.mean(hh * hh, axis=1, keepdims=True) + LN_EPS)
    return hn * nw * jax.nn.sigmoid(og)


def _mlstm_prompt_body(q_ref, k_ref, v_ref, og_ref, g_ref, gt_ref, nw_ref,
                       h_ref, c_ref, n_ref, m_ref, *, H, DK, DV):
    @pl.when(pl.program_id(1) == 0)
    def _():
        c_ref[...] = jnp.zeros_like(c_ref)
        n_ref[...] = jnp.zeros_like(n_ref)
        m_ref[...] = jnp.zeros_like(m_ref)

    g = g_ref[...]
    gt = gt_ref[...]
    for h in range(H):
        kq = slice(h * DK, (h + 1) * DK)
        vv = slice(h * DV, (h + 1) * DV)
        hh, C_new, n_new, m_new = _mlstm_head(
            q_ref[:, kq], k_ref[:, kq], v_ref[:, vv],
            g[:, h:h + 1], g[:, H + h:H + h + 1], gt[h:h + 1, :], gt[H + h:H + h + 1, :],
            c_ref[0, h], n_ref[0, h:h + 1, :], m_ref[0, :, h:h + 1])
        c_ref[0, h] = C_new
        n_ref[0, h:h + 1, :] = n_new
        m_ref[0, :, h:h + 1] = m_new
        h_ref[:, vv] = _head_out(hh, nw_ref[:, vv], og_ref[:, vv]).astype(BF16)


def _mlstm_prompt(qkv, og, g, gt, norm_w, B, T, H, DK, DV):
    L = MLSTM_CHUNK
    assert T % L == 0 and (2 * H * DK) % (H * DV) == 0
    nc = T // L
    v_blk = 2 * DK // DV
    row = lambda b, c: (b * nc + c, 0)
    return pl.pallas_call(
        functools.partial(_mlstm_prompt_body, H=H, DK=DK, DV=DV),
        grid=(B, nc),
        in_specs=[pl.BlockSpec((L, H * DK), row), pl.BlockSpec((L, H * DK), lambda b, c: (b * nc + c, 1)),
                  pl.BlockSpec((L, H * DV), lambda b, c: (b * nc + c, v_blk)), pl.BlockSpec((L, H * DV), row),
                  pl.BlockSpec((L, 2 * H), row),
                  pl.BlockSpec((2 * H, L), lambda b, c: (0, b * nc + c)),
                  pl.BlockSpec((1, H * DV), lambda b, c: (0, 0))],
        out_specs=[pl.BlockSpec((L, H * DV), row),
                   pl.BlockSpec((1, H, DK, DV), lambda b, c: (b, 0, 0, 0)),
                   pl.BlockSpec((1, H, DK), lambda b, c: (b, 0, 0)),
                   pl.BlockSpec((1, 1, H), lambda b, c: (b, 0, 0))],
        out_shape=[jax.ShapeDtypeStruct((B * T, H * DV), BF16),
                   jax.ShapeDtypeStruct((B, H, DK, DV), F32),
                   jax.ShapeDtypeStruct((B, H, DK), F32),
                   jax.ShapeDtypeStruct((B, 1, H), F32)],
        compiler_params=_cparams("parallel", "arbitrary"),
        name="mlstm_prompt",
    )(qkv, qkv, qkv, og, g, gt, norm_w)


def _mlstm_sample_body(q_ref, k_ref, v_ref, og_ref, g_ref, gt_ref, nw_ref, c0_ref, n0_ref, m0_ref,
                       h_ref, c_ref, n_ref, m_ref, *, H, DK, DV, TS):
    q = q_ref[...].astype(F32)
    k = k_ref[...].astype(F32)
    v = v_ref[...].astype(F32)
    og = og_ref[...]
    g = g_ref[...]
    gt = gt_ref[0]
    HT = H * TS
    r_idx = lax.broadcasted_iota(jnp.int32, (HT, HT), 0)
    c_idx = lax.broadcasted_iota(jnp.int32, (HT, HT), 1)
    same_head = (r_idx // TS) == (c_idx // TS)
    causal = same_head & (c_idx <= r_idx)
    upto = same_head & (r_idx <= c_idx)

    def stack(parts):
        return jnp.concatenate(parts, axis=0)

    def mxu(x):
        return x.astype(BF16)

    for bb in range(SAMPLE_BS):
        rows = slice(bb * TS, (bb + 1) * TS)
        hrows = [slice(h * TS, (h + 1) * TS) for h in range(H)]
        kq = [slice(h * DK, (h + 1) * DK) for h in range(H)]
        vv = [slice(h * DV, (h + 1) * DV) for h in range(H)]
        qf = stack([q[rows, kq[h]] for h in range(H)])
        kf = stack([k[rows, kq[h]] for h in range(H)])
        v_h = [v[rows, vv[h]].astype(BF16) for h in range(H)]
        ig_col = stack([g[rows, h:h + 1] for h in range(H)])
        lf_col = stack([g[rows, H + h:H + h + 1] for h in range(H)])
        ig_row = jnp.concatenate([gt[h:h + 1, rows] for h in range(H)], axis=1)
        lf_row = jnp.concatenate([gt[H + h:H + h + 1, rows] for h in range(H)], axis=1)
        m_prev = [m0_ref[0, bb:bb + 1, h:h + 1] for h in range(H)]
        m_prev_col = stack([jnp.broadcast_to(m, (TS, 1)) for m in m_prev])

        b_col = jnp.sum(jnp.where(causal, lf_row, 0.0), axis=1, keepdims=True)
        b_row = jnp.sum(jnp.where(upto, lf_col, 0.0), axis=0, keepdims=True)
        dmat = jnp.where(causal, b_col - b_row + ig_row, -jnp.inf)
        a_col = b_col + m_prev_col
        m_t = jnp.maximum(a_col, jnp.max(dmat, axis=1, keepdims=True))
        wmat = jnp.exp(dmat - m_t) * _dg(mxu(qf), mxu(kf), _NT)
        inter = jnp.exp(a_col - m_t)
        C0 = [c0_ref[bb, h] for h in range(H)]
        n0 = [n0_ref[bb, h:h + 1, :] for h in range(H)]
        q_c = stack([_dot(mxu(qf[hrows[h]]), mxu(C0[h])) for h in range(H)])
        n_rows = stack([jnp.broadcast_to(n, (TS, DK)) for n in n0])
        num = inter * q_c + _dot(mxu(wmat), stack(v_h))
        den = inter * jnp.sum(qf * n_rows, axis=1, keepdims=True) + jnp.sum(wmat, axis=1, keepdims=True)
        hh = num * (1.0 / jnp.maximum(jnp.abs(den), jnp.exp(-m_t)))

        m_new = [m_t[(h + 1) * TS - 1:(h + 1) * TS, :] for h in range(H)]
        b_last = [b_col[(h + 1) * TS - 1:(h + 1) * TS, :] for h in range(H)]
        m_new_col = stack([jnp.broadcast_to(m, (TS, 1)) for m in m_new])
        b_last_col = stack([jnp.broadcast_to(b, (TS, 1)) for b in b_last])
        kd = kf * jnp.exp(b_last_col - b_col + ig_col - m_new_col)

        og_s = stack([og[rows, vv[h]] for h in range(H)])
        nw_s = stack([jnp.broadcast_to(nw_ref[:, vv[h]], (TS, DV)) for h in range(H)])
        out = _head_out(hh, nw_s, og_s)
        for h in range(H):
            carry = jnp.exp(b_last[h] + m_prev[h] - m_new[h])
            c_ref[bb, h] = carry * C0[h] + _dg(mxu(kd[hrows[h]]), v_h[h], _TN)
            n_ref[bb, h:h + 1, :] = carry * n0[h] + jnp.sum(kd[hrows[h]], axis=0, keepdims=True)
            m_ref[0, bb:bb + 1, h:h + 1] = m_new[h]
            h_ref[rows, vv[h]] = out[hrows[h]].astype(BF16)


def _mlstm_sample(qkv, og, g, gt, norm_w, C0, n0, m0, NP, TS):
    Bs, H, DK, DV = C0.shape
    BS = SAMPLE_BS
    assert Bs % BS == 0 and NP % (BS * TS) == 0
    R = BS * TS
    off = NP // R
    row = lambda i: (off + i, 0)
    gt_s = gt[:, NP:].reshape(2 * H, Bs // BS, R).transpose(1, 0, 2)
    m0_s = m0.reshape(Bs // BS, BS, H)
    outs = pl.pallas_call(
        functools.partial(_mlstm_sample_body, H=H, DK=DK, DV=DV, TS=TS),
        grid=(Bs // BS,),
        in_specs=[pl.BlockSpec((R, H * DK), row), pl.BlockSpec((R, H * DK), lambda i: (off + i, 1)),
                  pl.BlockSpec((R, H * DV), lambda i: (off + i, 2 * DK // DV)), pl.BlockSpec((R, H * DV), row),
                  pl.BlockSpec((R, 2 * H), row),
                  pl.BlockSpec((1, 2 * H, R), lambda i: (i, 0, 0)),
                  pl.BlockSpec((1, H * DV), lambda i: (0, 0)),
                  pl.BlockSpec((BS, H, DK, DV), lambda i: (i, 0, 0, 0)),
                  pl.BlockSpec((BS, H, DK), lambda i: (i, 0, 0)),
                  pl.BlockSpec((1, BS, H), lambda i: (i, 0, 0))],
        out_specs=[pl.BlockSpec((R, H * DV), lambda i: (i, 0)),
                   pl.BlockSpec((BS, H, DK, DV), lambda i: (i, 0, 0, 0)),
                   pl.BlockSpec((BS, H, DK), lambda i: (i, 0, 0)),
                   pl.BlockSpec((1, BS, H), lambda i: (i, 0, 0))],
        out_shape=[jax.ShapeDtypeStruct((Bs * TS, H * DV), BF16),
                   jax.ShapeDtypeStruct((Bs, H, DK, DV), F32),
                   jax.ShapeDtypeStruct((Bs, H, DK), F32),
                   jax.ShapeDtypeStruct((Bs // BS, BS, H), F32)],
        compiler_params=_cparams("parallel"),
        name="mlstm_sample",
    )(qkv, qkv, qkv, og, g, gt_s, norm_w, C0, n0, m0_s)
    return outs[0], outs[1], outs[2], outs[3].reshape(Bs, H)


def _layer_norm(z, gain, bias):
    mu = jnp.mean(z, axis=1, keepdims=True)
    zc = z - mu
    var = jnp.mean(zc * zc, axis=1, keepdims=True)
    return zc * lax.rsqrt(var + LN_EPS) * gain + bias


def _proj_norm_body(ap_ref, as_ref, w_ref, *refs, alpha, npb, x_split):
    prompt = pl.program_id(0) < npb
    a = jnp.where(prompt, ap_ref[...], as_ref[...])
    if x_split:
        xp_ref, xs_ref, gain_ref, bias_ref, o_ref, ob_ref = refs
        x = jnp.where(prompt, xp_ref[...], xs_ref[...])
    else:
        x_ref, gain_ref, bias_ref, o_ref, ob_ref = refs
        x = x_ref[...]
    y = _layer_norm(alpha * x + _dot(a, w_ref[...]), gain_ref[...], bias_ref[...])
    o_ref[...] = y
    ob_ref[...] = y.astype(BF16)


def _proj_norm(a_prompt, a_sample, w, x, gain, bias, alpha):
    NP, K = a_prompt.shape
    NS = a_sample.shape[0]
    N, D = NP + NS, w.shape[1]
    tm = _pick(NS, ROW_TILES)
    assert NP % tm == 0
    npb = NP // tm
    first = lambda i: (jnp.minimum(i, npb - 1), 0)
    second = lambda i: (jnp.maximum(i - npb, 0), 0)
    x_split = isinstance(x, tuple)
    x_ops = tuple(x) if x_split else (x,)
    x_specs = ([pl.BlockSpec((tm, D), first), pl.BlockSpec((tm, D), second)] if x_split
               else [pl.BlockSpec((tm, D), lambda i: (i, 0))])
    return pl.pallas_call(
        functools.partial(_proj_norm_body, alpha=alpha, npb=npb, x_split=x_split),
        grid=(N // tm,),
        in_specs=[pl.BlockSpec((tm, K), first), pl.BlockSpec((tm, K), second),
                  pl.BlockSpec((K, D), lambda i: (0, 0))] + x_specs +
                 [pl.BlockSpec((1, D), lambda i: (0, 0)),
                  pl.BlockSpec((1, D), lambda i: (0, 0))],
        out_specs=[pl.BlockSpec((tm, D), lambda i: (i, 0)), pl.BlockSpec((tm, D), lambda i: (i, 0))],
        out_shape=[jax.ShapeDtypeStruct((N, D), F32), jax.ShapeDtypeStruct((N, D), BF16)],
        compiler_params=_cparams("parallel"),
        name="proj_norm",
    )(a_prompt, a_sample, w, *x_ops, gain[None, :], bias[None, :])


def _static_loop(n, fn):
    for r in range(n):
        fn(r)


def _rows_loop(n, fn, unroll=8):
    def body(r, c):
        fn(r)
        return c
    lax.fori_loop(0, n, body, 0, unroll=unroll)


def _rows_loop_2q(n, fn, unroll=4):
    assert n % 2 == 0

    def body(i, c):
        fn(2 * i, 0)
        fn(2 * i + 1, 1)
        return c
    lax.fori_loop(0, n // 2, body, 0, unroll=unroll)


def _combine_norm_body(slot_ref, slot_next_ref, gate_ref, x_ref, gain_ref, bias_ref, ys_hbm, oa_ref, ob_ref,
                       ybuf, sem, *, alpha, nblk, split_blk):
    i = pl.program_id(0)
    cur = lax.rem(i, 2)
    tm, D = x_ref.shape

    def row_copy(idx_ref, k, r, s):
        return pltpu.make_async_copy(ys_hbm.at[pl.ds(idx_ref[k, r], 1)], ybuf.at[s, k, pl.ds(r, 1)], sem.at[s])

    def start_block(idx_ref, s):
        for k in range(2):
            _rows_loop_2q(tm, lambda r, q: row_copy(idx_ref, k, r, s).start(priority=q))

    @pl.when(i == 0)
    def _():
        start_block(slot_ref, 0)

    @pl.when(i + 1 < nblk)
    def _():
        start_block(slot_next_ref, 1 - cur)

    for k in range(2):
        _static_loop(tm, lambda r: row_copy(slot_ref, k, r, cur).wait())
    g = gate_ref[...]
    moe = g[:, 0:1] * ybuf[cur, 0].reshape(tm, D) + g[:, 1:2] * ybuf[cur, 1].reshape(tm, D)
    y = _layer_norm(alpha * x_ref[...] + moe, gain_ref[...], bias_ref[...])
    if split_blk is None:
        oa_ref[...] = y
        ob_ref[...] = y.astype(BF16)
    else:
        @pl.when(i < split_blk)
        def _():
            oa_ref[...] = y

        @pl.when(i >= split_blk)
        def _():
            ob_ref[...] = y


def _combine_norm(x, ys, slot, gate_col, gain, bias, alpha, split_rows=None):
    N, D = x.shape
    tm = _pick(N if split_rows is None else math.gcd(split_rows, N - split_rows), (256, 128))
    nblk = N // tm
    idx_spec = lambda off: pl.BlockSpec((2, tm), lambda i: (0, jnp.minimum(i + off, nblk - 1)),
                                        memory_space=pltpu.SMEM)
    if split_rows is None:
        split_blk = None
        out_specs = [pl.BlockSpec((tm, D), lambda i: (i, 0)), pl.BlockSpec((tm, D), lambda i: (i, 0))]
        out_shape = [jax.ShapeDtypeStruct((N, D), F32), jax.ShapeDtypeStruct((N, D), BF16)]
    else:
        split_blk = split_rows // tm
        out_specs = [pl.BlockSpec((tm, D), lambda i: (jnp.minimum(i, split_blk - 1), 0)),
                     pl.BlockSpec((tm, D), lambda i: (jnp.maximum(i - split_blk, 0), 0))]
        out_shape = [jax.ShapeDtypeStruct((split_rows, D), F32), jax.ShapeDtypeStruct((N - split_rows, D), F32)]
    return pl.pallas_call(
        functools.partial(_combine_norm_body, alpha=alpha, nblk=nblk, split_blk=split_blk),
        grid=(nblk,),
        in_specs=[idx_spec(0), idx_spec(1),
                  pl.BlockSpec((tm, gate_col.shape[1]), lambda i: (i, 0)),
                  pl.BlockSpec((tm, D), lambda i: (i, 0)),
                  pl.BlockSpec((1, D), lambda i: (0, 0)),
                  pl.BlockSpec((1, D), lambda i: (0, 0)),
                  pl.BlockSpec(memory_space=pl.ANY)],
        out_specs=out_specs,
        out_shape=out_shape,
        scratch_shapes=[pltpu.VMEM((2, 2, tm) + ys.shape[1:], F32), pltpu.SemaphoreType.DMA((2,))],
        compiler_params=_cparams("arbitrary"),
        name="combine_norm",
    )(slot, slot, gate_col, x, gain[None, :], bias[None, :], ys)


def _router_body(x_ref, wt_ref, b_ref, eid_ref, gate_ref, *, E):
    logits = _dot3(wt_ref[...], x_ref[...], _NT)
    ex = jnp.exp(logits - jnp.max(logits, axis=0, keepdims=True))
    probs = ex / jnp.sum(ex, axis=0, keepdims=True)
    sel = probs + b_ref[...]
    per = E // N_GROUPS
    s = [sel[e:e + 1, :] for e in range(E)]
    p = [probs[e:e + 1, :] for e in range(E)]

    def top2_sum(vals):
        best = None
        for a in range(len(vals)):
            for b in range(a + 1, len(vals)):
                pair = vals[a] + vals[b]
                best = pair if best is None else jnp.maximum(best, pair)
        return best

    g_best = top2_sum(s[:per])
    g_idx = jnp.zeros_like(g_best, dtype=jnp.int32)
    for gi in range(1, N_GROUPS):
        score = top2_sum(s[gi * per:(gi + 1) * per])
        better = score > g_best
        g_idx = jnp.where(better, gi, g_idx)
        g_best = jnp.where(better, score, g_best)

    def in_group(vals, j):
        out = vals[j]
        for gi in range(1, N_GROUPS):
            out = jnp.where(g_idx == gi, vals[gi * per + j], out)
        return out

    sg = [in_group(s, j) for j in range(per)]
    pg = [in_group(p, j) for j in range(per)]
    i1 = jnp.zeros_like(g_idx)
    b1 = sg[0]
    for j in range(1, per):
        better = sg[j] > b1
        i1 = jnp.where(better, j, i1)
        b1 = jnp.where(better, sg[j], b1)
    i2 = jnp.zeros_like(g_idx)
    b2 = jnp.full_like(b1, -jnp.inf)
    for j in range(per):
        better = (i1 != j) & (sg[j] > b2)
        i2 = jnp.where(better, j, i2)
        b2 = jnp.where(better, sg[j], b2)
    p1 = pg[0]
    p2 = pg[0]
    for j in range(1, per):
        p1 = jnp.where(i1 == j, pg[j], p1)
        p2 = jnp.where(i2 == j, pg[j], p2)
    tot = p1 + p2
    eid_ref[...] = jnp.concatenate([g_idx * per + i1, g_idx * per + i2], axis=0)
    gates = jnp.concatenate([p1 / tot, p2 / tot, jnp.zeros((GATE_COLS - 2, p1.shape[1]), F32)], axis=0)
    gate_ref[...] = gates.T


def _router(x, w_router, b_router):
    N, D = x.shape
    E = w_router.shape[1]
    assert E % N_GROUPS == 0 and E // N_GROUPS >= 2
    tm = _pick(N, ROW_TILES)
    return pl.pallas_call(
        functools.partial(_router_body, E=E),
        grid=(N // tm,),
        in_specs=[pl.BlockSpec((tm, D), lambda i: (i, 0)),
                  pl.BlockSpec((E, D), lambda i: (0, 0)),
                  pl.BlockSpec((E, 1), lambda i: (0, 0))],
        out_specs=[pl.BlockSpec((2, tm), lambda i: (0, i)), pl.BlockSpec((tm, GATE_COLS), lambda i: (i, 0))],
        out_shape=[jax.ShapeDtypeStruct((2, N), jnp.int32), jax.ShapeDtypeStruct((N, GATE_COLS), F32)],
        compiler_params=_cparams("parallel"),
        name="router",
    )(x, w_router.T, b_router.astype(F32)[:, None])


def _plan_body(e_ref, slot_ref, cnt_ref, *, E, TM):
    e = e_ref[...]
    R = e.shape[0]
    upper = (lax.broadcasted_iota(jnp.int32, (LANES, LANES), 0)
             <= lax.broadcasted_iota(jnp.int32, (LANES, LANES), 1)).astype(BF16)
    below = (lax.broadcasted_iota(jnp.int32, (R, R), 1) < lax.broadcasted_iota(jnp.int32, (R, R), 0)).astype(BF16)
    lane = lax.broadcasted_iota(jnp.int32, (1, LANES), 1)
    slot = jnp.zeros(e.shape, F32)
    counts = jnp.zeros((1, LANES), F32)
    start = jnp.zeros((1, 1), F32)
    for ex in range(E):
        hit = e == ex
        within = _dot(hit.astype(BF16), upper)
        rows_before = _dot(below, within.astype(BF16))[:, LANES - 1:LANES]
        rank = within - 1.0 + rows_before
        count = rows_before[R - 1:R, :] + within[R - 1:R, LANES - 1:LANES]
        slot = jnp.where(hit, start + rank, slot)
        counts = jnp.where(lane == ex, count, counts)
        start = start + jnp.floor((count + (TM - 1)) * (1.0 / TM)) * TM
    slot_ref[...] = slot.astype(jnp.int32)
    cnt_ref[...] = jnp.broadcast_to(counts, cnt_ref.shape).astype(jnp.int32)


def _plan(eid, E, TM):
    N = eid.shape[1]
    A = 2 * N
    assert A % LANES == 0 and (TM & (TM - 1)) == 0
    R = A // LANES
    Rp = -(-R // LANES) * LANES
    e2d = jnp.pad(eid.reshape(R, LANES), ((0, Rp - R), (0, 0)), constant_values=-1)
    slot2d, cnt = pl.pallas_call(
        functools.partial(_plan_body, E=E, TM=TM),
        out_shape=[jax.ShapeDtypeStruct((Rp, LANES), jnp.int32), jax.ShapeDtypeStruct((8, LANES), jnp.int32)],
        compiler_params=pltpu.CompilerParams(vmem_limit_bytes=VMEM_LIMIT_BYTES),
        name="route_plan",
    )(e2d)
    return slot2d[:R].reshape(2, N), cnt[0, :E]


def _group_table(counts, TM, n_blocks):
    E = counts.shape[0]
    GB = MOE_GROUP_BLOCKS
    nblk = (counts + TM - 1) // TM
    blk_end = jnp.cumsum(nblk)
    blk_start = blk_end - nblk
    ngrp = (nblk + GB - 1) // GB
    grp_end = jnp.cumsum(ngrp)
    n_groups_max = E + n_blocks // GB
    g = jnp.arange(n_groups_max, dtype=jnp.int32)
    g_expert = jnp.minimum(jnp.sum((grp_end[None, :] <= g[:, None]).astype(jnp.int32), axis=1), E - 1)
    j = g - (grp_end - ngrp)[g_expert]
    g_start = blk_start[g_expert] + j * GB
    g_nblk = jnp.clip(nblk[g_expert] - j * GB, 0, GB)
    live = g < grp_end[-1]
    g_nblk = jnp.where(live, g_nblk, 0)
    meta = jnp.stack([grp_end[-1], blk_end[-1]]).astype(jnp.int32)
    pad_start = blk_start * TM + counts
    pad_count = nblk * TM - counts
    return (g_expert.astype(jnp.int32), g_start.astype(jnp.int32), g_nblk.astype(jnp.int32), meta,
            pad_start.astype(jnp.int32), pad_count.astype(jnp.int32))


def _pack_halves(x):
    half = x.shape[1] // 2
    bits = lax.bitcast_convert_type(x.astype(BF16).astype(F32), jnp.uint32)
    return (bits[:, :half] >> 16) | bits[:, half:]


def _unpack_halves(p):
    lo = lax.bitcast_convert_type(p << 16, F32).astype(BF16)
    hi = lax.bitcast_convert_type(p & jnp.uint32(0xFFFF0000), F32).astype(BF16)
    return lo, hi


def _dispatch_body(pad_start_ref, pad_count_ref, meta_ref, slot_ref, x_ref, xs_hbm, pbuf, zbuf, sem, zsem,
                   *, nblk, E, TM, n_blocks):
    i = pl.program_id(0)
    cur = lax.rem(i, 2)
    tm = x_ref.shape[0]

    def row_copy(k, r, s):
        return pltpu.make_async_copy(pbuf.at[s, pl.ds(r, 1)], xs_hbm.at[pl.ds(slot_ref[k, r], 1)], sem.at[s])

    def wait_block(s):
        for k in range(2):
            _static_loop(tm, lambda r: row_copy(k, r, s).wait())

    def zero_copy(row):
        return pltpu.make_async_copy(zbuf.at[pl.ds(0, 1)], xs_hbm.at[pl.ds(row, 1)], zsem.at[0])

    @pl.when(i == 0)
    def _():
        zbuf[...] = jnp.zeros_like(zbuf)
        for ex in range(E):
            _rows_loop(pad_count_ref[ex], lambda j: zero_copy(pad_start_ref[ex] + j).start(), unroll=1)
        tail0 = meta_ref[1] * TM
        ntail = n_blocks * TM - tail0
        _rows_loop(ntail, lambda j: zero_copy(tail0 + j).start(), unroll=1)
        for ex in range(E):
            _rows_loop(pad_count_ref[ex], lambda j: zero_copy(pad_start_ref[ex] + j).wait(), unroll=1)
        _rows_loop(ntail, lambda j: zero_copy(tail0 + j).wait(), unroll=1)

    @pl.when(i >= 2)
    def _():
        wait_block(cur)

    pbuf[cur] = _pack_halves(x_ref[...]).reshape(pbuf.shape[1:])
    for k in range(2):
        _rows_loop_2q(tm, lambda r, q: row_copy(k, r, cur).start(priority=q))

    @pl.when(i == nblk - 1)
    def _():
        wait_block(cur)
        if nblk >= 2:
            wait_block(1 - cur)


def _dispatch(x, slot, pad_start, pad_count, meta, TM, n_blocks):
    N, D = x.shape
    E = pad_start.shape[0]
    tm = _pick(N, (256, 128))
    nblk = N // tm
    row_tile = (D // 2 // LANES, LANES)
    grid_spec = pltpu.PrefetchScalarGridSpec(
        num_scalar_prefetch=3,
        grid=(nblk,),
        in_specs=[pl.BlockSpec((2, tm), lambda i, *_: (0, i), memory_space=pltpu.SMEM),
                  pl.BlockSpec((tm, D), lambda i, *_: (i, 0))],
        out_specs=pl.BlockSpec(memory_space=pl.ANY),
        scratch_shapes=[pltpu.VMEM((2, tm) + row_tile, jnp.uint32), pltpu.VMEM((1,) + row_tile, jnp.uint32),
                        pltpu.SemaphoreType.DMA((2,)), pltpu.SemaphoreType.DMA((1,))],
    )
    return pl.pallas_call(
        functools.partial(_dispatch_body, nblk=nblk, E=E, TM=TM, n_blocks=n_blocks),
        grid_spec=grid_spec,
        out_shape=jax.ShapeDtypeStruct((n_blocks * TM,) + row_tile, jnp.uint32),
        compiler_params=_cparams("arbitrary"),
        name="moe_dispatch",
    )(pad_start, pad_count, meta, slot, x)


def _experts_body(ge_ref, gs_ref, gn_ref, meta_ref, xs_hbm, wg_hbm, wu_hbm, wd_hbm, ys_hbm,
                  xraw, xg, acc, stage_g, stage_u, stage_d, wgu, wdb, obuf, xsem, wsem, osem,
                  *, layer, chunks, TM, n_blocks):
    g = pl.program_id(0)
    n_groups = meta_ref[0]
    half = xg.shape[2]
    n_chunks = len(chunks)

    def x_copy(b0, rb):
        return pltpu.make_async_copy(xs_hbm.at[pl.ds((b0 + rb) * TM, TM)], xraw.at[rb], xsem.at[0])

    def w_copies(ex, c, s):
        f0, fw = chunks[c]
        return (pltpu.make_async_copy(wg_hbm.at[layer, ex, :, pl.ds(f0, fw)], stage_g.at[s, :, pl.ds(0, fw)],
                                      wsem.at[s, 0]),
                pltpu.make_async_copy(wu_hbm.at[layer, ex, :, pl.ds(f0, fw)], stage_u.at[s, :, pl.ds(0, fw)],
                                      wsem.at[s, 1]),
                pltpu.make_async_copy(wd_hbm.at[layer, ex, pl.ds(f0, fw), :], stage_d.at[s, pl.ds(0, fw), :],
                                      wsem.at[s, 2]))

    def out_copy(b0, rb, s):
        return pltpu.make_async_copy(obuf.at[s], ys_hbm.at[pl.ds((b0 + rb) * TM, TM)], osem.at[s])

    @pl.when(g < n_groups)
    def _():
        ex = ge_ref[g]
        b0 = gs_ref[g]
        nb = gn_ref[g]
        n_pairs = lax.shift_right_logical(nb, 1)
        odd = (nb & 1) == 1
        D = acc.shape[2]
        _rows_loop(nb, lambda rb: x_copy(b0, rb).start(), unroll=1)

        def start_weights(expert, c, s):
            for idx, cp in enumerate(w_copies(expert, c, s)):
                cp.start(priority=idx % 2)

        @pl.when(g == 0)
        def _():
            start_weights(ex, 0, 0)

        _rows_loop(nb, lambda rb: x_copy(b0, rb).wait(), unroll=1)

        def untile(rb):
            xg[rb] = xraw[rb].reshape(TM, half)

        _rows_loop(nb, untile, unroll=1)

        for c, (f0, fw) in enumerate(chunks):
            s = c % 2
            for cp in w_copies(ex, c, s):
                cp.wait()
            if c + 1 < n_chunks:
                start_weights(ex, c + 1, 1 - s)
            else:
                @pl.when(g + 1 < n_groups)
                def _():
                    start_weights(ge_ref[g + 1], 0, 1 - s)
            wgu[:, 0:fw] = stage_g[s, :, 0:fw].astype(BF16)
            wgu[:, fw:2 * fw] = stage_u[s, :, 0:fw].astype(BF16)
            wdb[0:fw, :] = stage_d[s, 0:fw, :].astype(BF16)

            last = c + 1 == n_chunks

            def work(rb0, n, c=c, fw=fw, last=last):
                rows = pl.ds(rb0, n)
                xl, xh = _unpack_halves(xg[rows].reshape(n * TM, half))
                gu = _dot(xl, wgu[0:half, 0:2 * fw]) + _dot(xh, wgu[half:2 * half, 0:2 * fw])
                hidden = jax.nn.silu(gu[:, 0:fw]) * gu[:, fw:2 * fw]
                part = _dot(hidden.astype(BF16), wdb[0:fw, :]).reshape(n, TM, D)
                if c == 0:
                    acc[rows] = part
                elif not last:
                    acc[rows] = acc[rows] + part
                else:
                    fin = acc[rows] + part
                    for t in range(n):
                        obuf[t] = fin[t].reshape(obuf.shape[1:])
                        out_copy(b0, rb0 + t, t).start()

            def pair(i, carry, last=last):
                if last:
                    @pl.when(i >= 1)
                    def _():
                        out_copy(b0, 0, 0).wait()
                        out_copy(b0, 0, 1).wait()
                work(2 * i, 2)
                return carry

            lax.fori_loop(0, n_pairs, pair, 0)

            @pl.when(odd)
            def _(last=last):
                if last:
                    @pl.when(n_pairs >= 1)
                    def _():
                        out_copy(b0, 0, 0).wait()
                work(nb - 1, 1)

        out_copy(b0, 0, 0).wait()

        @pl.when(n_pairs >= 1)
        def _():
            out_copy(b0, 0, 1).wait()

    @pl.when(g == pl.num_programs(0) - 1)
    def _():
        used = meta_ref[1]
        obuf[0] = jnp.zeros(obuf.shape[1:], F32)
        _rows_loop(n_blocks - used, lambda j: out_copy(used, j, 0).start(), unroll=1)
        _rows_loop(n_blocks - used, lambda j: out_copy(used, j, 0).wait(), unroll=1)


def _experts(xs, g_expert, g_start, g_nblk, meta, w_gate, w_up, w_down, layer, TM, n_blocks):
    _, E, D, F = w_gate.shape
    GB = MOE_GROUP_BLOCKS
    FC = MOE_FC
    chunks = tuple((f0, min(FC, F - f0)) for f0 in range(0, F, FC))
    assert len(chunks) % 2 == 0 and all(fw % LANES == 0 for _, fw in chunks)
    out_tile = (D // LANES, LANES)
    grid_spec = pltpu.PrefetchScalarGridSpec(
        num_scalar_prefetch=4,
        grid=(g_expert.shape[0],),
        in_specs=[pl.BlockSpec(memory_space=pl.ANY)] * 4,
        out_specs=pl.BlockSpec(memory_space=pl.ANY),
        scratch_shapes=[pltpu.VMEM((GB, TM) + xs.shape[1:], jnp.uint32),
                        pltpu.VMEM((GB, TM, D // 2), jnp.uint32), pltpu.VMEM((GB, TM, D), F32),
                        pltpu.VMEM((2, D, FC), F32), pltpu.VMEM((2, D, FC), F32), pltpu.VMEM((2, FC, D), F32),
                        pltpu.VMEM((D, 2 * FC), BF16), pltpu.VMEM((FC, D), BF16),
                        pltpu.VMEM((2, TM) + out_tile, F32),
                        pltpu.SemaphoreType.DMA((1,)), pltpu.SemaphoreType.DMA((2, 3)),
                        pltpu.SemaphoreType.DMA((2,))],
    )
    return pl.pallas_call(
        functools.partial(_experts_body, layer=layer, chunks=chunks, TM=TM, n_blocks=n_blocks),
        grid_spec=grid_spec,
        out_shape=jax.ShapeDtypeStruct((n_blocks * TM,) + out_tile, F32),
        compiler_params=_cparams("arbitrary"),
        name="moe_experts",
    )(g_expert, g_start, g_nblk, meta, xs, w_gate, w_up, w_down)


def _moe_layer(x, w_router, b_router, w_gate, w_up, w_down, layer, gain, bias, alpha, split_rows=None):
    N = x.shape[0]
    E = w_router.shape[1]
    TM = MOE_TM
    n_blocks = -(-(2 * N + E * (TM - 1)) // TM)
    eid, gate_col = _router(x, w_router, b_router)
    slot, counts = _plan(eid, E, TM)
    g_expert, g_start, g_nblk, meta, pad_start, pad_count = _group_table(counts, TM, n_blocks)
    xs = _dispatch(x, slot, pad_start, pad_count, meta, TM, n_blocks)
    ys = _experts(xs, g_expert, g_start, g_nblk, meta, w_gate, w_up, w_down, layer, TM, n_blocks)
    return _combine_norm(x, ys, slot, gate_col, gain, bias, alpha, split_rows)


def _pair_attention(q_pairs, kc, vc, bias, sinks_lo, sinks_hi):
    P = len(q_pairs)
    R, lanes = q_pairs[0].shape
    lo = lax.broadcasted_iota(jnp.int32, (R, lanes), 1) < lanes // 2
    zero = jnp.zeros((R, lanes), BF16)
    qs = jnp.concatenate([jnp.where(lo, qp, zero) for qp in q_pairs]
                         + [jnp.where(lo, zero, qp) for qp in q_pairs], axis=0)
    sink_col = jnp.concatenate([jnp.full((R, 1), s, F32) for s in list(sinks_lo) + list(sinks_hi)], axis=0)
    s = _dg(qs, kc, _NT) + bias
    mx = jnp.maximum(jnp.max(s, axis=1, keepdims=True), sink_col)
    e = jnp.exp(s - mx)
    den = jnp.sum(e, axis=1, keepdims=True) + jnp.exp(sink_col - mx)
    o = _dot(e.astype(BF16), vc) * (1.0 / den)
    return [jnp.where(lo, o[p * R:(p + 1) * R], o[(P + p) * R:(P + p + 1) * R]) for p in range(P)]


def _swa_prompt_body(sink_ref, q_ref, kp_ref, kc_ref, vp_ref, vc_ref, o_ref, *, KVH, G):
    W = q_ref.shape[0]
    P = G // 2
    first = pl.program_id(1) == 0
    r = lax.broadcasted_iota(jnp.int32, (W, 2 * W), 0)
    kj = lax.broadcasted_iota(jnp.int32, (W, 2 * W), 1)
    diff = r + W - kj
    valid = (diff >= 0) & (diff < W) & (jnp.logical_not(first) | (kj >= W))
    bias = jnp.concatenate([jnp.where(valid, 0.0, -jnp.inf)] * G, axis=0)
    for g in range(KVH):
        ks = slice(g * LANES, (g + 1) * LANES)
        kc = jnp.concatenate([kp_ref[:, ks], kc_ref[:, ks]], axis=0)
        vc = jnp.concatenate([vp_ref[:, ks], vc_ref[:, ks]], axis=0)
        cols = [slice((g * P + p) * LANES, (g * P + p + 1) * LANES) for p in range(P)]
        outs = _pair_attention([q_ref[:, c] for c in cols], kc, vc, bias,
                               [sink_ref[g * G + 2 * p] for p in range(P)],
                               [sink_ref[g * G + 2 * p + 1] for p in range(P)])
        for c, o in zip(cols, outs):
            o_ref[:, c] = o.astype(BF16)


def _swa_prompt(q, kdup, vdup, sinks, B, T, KVH):
    Hq = sinks.shape[0]
    W = WINDOW
    assert T % W == 0 and kdup.shape[1] == KVH * LANES and q.shape[1] == Hq // 2 * LANES
    nb = T // W
    cur = lambda b, n: (b * nb + n, 0)
    prev = lambda b, n: (b * nb + jnp.maximum(n - 1, 0), 0)
    return pl.pallas_call(
        functools.partial(_swa_prompt_body, KVH=KVH, G=Hq // KVH),
        grid=(B, nb),
        in_specs=[pl.BlockSpec(memory_space=pltpu.SMEM),
                  pl.BlockSpec((W, q.shape[1]), cur),
                  pl.BlockSpec((W, KVH * LANES), prev), pl.BlockSpec((W, KVH * LANES), cur),
                  pl.BlockSpec((W, KVH * LANES), prev), pl.BlockSpec((W, KVH * LANES), cur)],
        out_specs=pl.BlockSpec((W, q.shape[1]), cur),
        out_shape=jax.ShapeDtypeStruct((B * T, q.shape[1]), BF16),
        compiler_params=_cparams("parallel", "parallel"),
        name="swa_prompt",
    )(sinks, q, kdup, kdup, vdup, vdup)


def _swa_sample_body(sink_ref, q_ref, ck_ref, cv_ref, kn_ref, vn_ref, o_ref, *, KVH, G, TS, WB, K):
    P = G // 2
    NPAIR = KVH * P
    HD = LANES // 2
    q = q_ref[...].astype(F32)
    r = lax.broadcasted_iota(jnp.int32, (2 * NPAIR * TS, KVH * K), 0)
    c = lax.broadcasted_iota(jnp.int32, (2 * NPAIR * TS, KVH * K), 1)
    diff = WB + r % TS - c % K
    own = (r % (NPAIR * TS)) // (P * TS) == c // K
    bias = jnp.where(own & (diff >= 0) & (diff < WINDOW), 0.0, -jnp.inf)
    cols = [slice(p * LANES, (p + 1) * LANES) for p in range(NPAIR)]
    sinks_lo = [sink_ref[2 * p] for p in range(NPAIR)]
    sinks_hi = [sink_ref[2 * p + 1] for p in range(NPAIR)]
    pad = jnp.zeros((K - WB - TS, HD), F32)

    def stack_heads(window, new):
        parts = []
        for g in range(KVH):
            hs = slice(g * HD, (g + 1) * HD)
            one = jnp.concatenate([window[:, hs], new[:, hs], pad], axis=0)
            parts.append(jnp.concatenate([one, one], axis=1))
        return jnp.concatenate(parts, axis=0).astype(BF16)

    for bb in range(SAMPLE_BS):
        rows = slice(bb * TS, (bb + 1) * TS)
        kc = stack_heads(ck_ref[bb], kn_ref[rows, :])
        vc = stack_heads(cv_ref[bb], vn_ref[rows, :])
        outs = _pair_attention([q[rows, cc].astype(BF16) for cc in cols], kc, vc, bias, sinks_lo, sinks_hi)
        for cc, o in zip(cols, outs):
            o_ref[rows, cc] = o.astype(BF16)


def _swa_sample(q, cache_k, cache_v, k_new, v_new, sinks, NP, TS, KVH):
    Bs, WB, nkv = cache_k.shape
    Hq = sinks.shape[0]
    BS = SAMPLE_BS
    R = BS * TS
    off = NP // R
    row = lambda i: (off + i, 0)
    K = -(-(WB + TS) // LANES) * LANES
    return pl.pallas_call(
        functools.partial(_swa_sample_body, KVH=KVH, G=Hq // KVH, TS=TS, WB=WB, K=K),
        grid=(Bs // BS,),
        in_specs=[pl.BlockSpec(memory_space=pltpu.SMEM),
                  pl.BlockSpec((R, q.shape[1]), row),
                  pl.BlockSpec((BS, WB, nkv), lambda i: (i, 0, 0)),
                  pl.BlockSpec((BS, WB, nkv), lambda i: (i, 0, 0)),
                  pl.BlockSpec((R, nkv), row),
                  pl.BlockSpec((R, nkv), row)],
        out_specs=pl.BlockSpec((R, q.shape[1]), lambda i: (i, 0)),
        out_shape=jax.ShapeDtypeStruct((Bs * TS, q.shape[1]), BF16),
        compiler_params=_cparams("parallel"),
        name="swa_sample",
    )(sinks, q, cache_k, cache_v, k_new, v_new)


def _rope_tables(pos, head_dim, lanes):
    half = head_dim // 2
    inv = ROPE_THETA ** (-jnp.arange(half, dtype=F32) / half)
    ang = pos.astype(F32)[:, None] * inv[None, :]
    reps = lanes // half
    return jnp.tile(jnp.cos(ang), (1, reps)), jnp.tile(jnp.sin(ang), (1, reps))


def kernel(x_prompt, x_sample, state_C, state_n, state_m, cache_k, cache_v, w_in_a, b_i_a, b_f_a, norm_a, w_out_a,
           w_kv, w_q_b, sinks_b, w_out_b, w_router, b_router, w_gate_e, w_up_e, w_down_e, ln_g, ln_b):
    B, T, D = x_prompt.shape
    Bs, TS, _ = x_sample.shape
    depth = ln_g.shape[0]
    assert depth == 2 and w_in_a.shape[0] == 1 and w_q_b.shape[0] == 1
    alpha = (2 * depth) ** 0.25
    NP, NS = B * T, Bs * TS
    H = b_i_a.shape[1]
    DK, DV = state_C.shape[3], state_C.shape[4]
    WB, KVH, HD = cache_k.shape[1], cache_k.shape[2], cache_k.shape[3]
    E = w_router.shape[1]

    x0_p = x_prompt.reshape(NP, D).astype(F32)
    x0_s = x_sample.reshape(NS, D).astype(F32)

    w_in = w_in_a.astype(F32)
    nqk, nv = H * DK, H * DV
    w_gates = w_in[0, :, 2 * nqk + nv + D:]
    g, gt, x0b = _mlstm_gates(x0_p, x0_s, w_gates, b_i_a[0], b_f_a[0])
    qkv_scale = jnp.concatenate([jnp.ones((nqk,), F32), jnp.full((nqk,), DK ** -0.5, F32), jnp.ones((nv,), F32)])
    qkv = _matmul(x0b, w_in, qkv_scale[None, :], BF16, col0=0, ncols=2 * nqk + nv)
    og = _matmul(x0b, w_in, jnp.ones((1, D), F32), F32, col0=2 * nqk + nv, ncols=D)
    norm_w = norm_a[0].astype(F32)[None, :]
    h_p, C_p, n_p, m_p = _mlstm_prompt(qkv, og, g, gt, norm_w, B, T, H, DK, DV)
    h_s, C_s, n_s, m_s = _mlstm_sample(qkv, og, g, gt, norm_w, state_C[0].astype(F32), state_n[0].astype(F32),
                                       state_m[0].astype(F32), NP, TS)
    x1, _ = _proj_norm(h_p, h_s, w_out_a[0].astype(BF16), (x0_p, x0_s), ln_g[0, 0], ln_b[0, 0], alpha)
    x2, x2b = _moe_layer(x1, w_router, b_router, w_gate_e, w_up_e, w_down_e, 0, ln_g[0, 1], ln_b[0, 1], alpha)

    pos = jnp.concatenate([jnp.tile(jnp.arange(T, dtype=jnp.int32), B),
                           jnp.tile(PAST_LEN + jnp.arange(TS, dtype=jnp.int32), Bs)])
    cos, sin = _rope_tables(pos, HD, 128)
    nkv = KVH * HD
    k_new = _matmul_rope(x2b, w_kv[:, :nkv].astype(BF16), cos, sin, HD, F32)
    v_new = _matmul(x2b, w_kv[:, nkv:].astype(BF16), jnp.ones((1, nkv), F32), F32)
    assert 2 * HD == LANES
    qb = _matmul_rope(x2b, w_q_b[0].astype(BF16), cos, sin, HD, BF16, scale=HD ** -0.5)

    def dup_heads(a):
        a4 = a.reshape(a.shape[:-1] + (KVH, HD)).astype(BF16)
        return jnp.concatenate([a4, a4], axis=-1).reshape(a.shape[:-1] + (KVH * LANES,))

    o_p = _swa_prompt(qb, dup_heads(k_new), dup_heads(v_new), sinks_b[0].astype(F32), B, T, KVH)
    ck = cache_k.reshape(Bs, WB, nkv).astype(F32)
    cv = cache_v.reshape(Bs, WB, nkv).astype(F32)
    o_s = _swa_sample(qb, ck, cv, k_new, v_new, sinks_b[0].astype(F32), NP, TS, KVH)
    x3, _ = _proj_norm(o_p, o_s, w_out_b[0].astype(BF16), x2, ln_g[1, 0], ln_b[1, 0], alpha)
    y_p, y_s = _moe_layer(x3, w_router, b_router, w_gate_e, w_up_e, w_down_e, 1, ln_g[1, 1], ln_b[1, 1], alpha,
                          split_rows=NP)
    y_prompt = y_p.reshape(B, T, D)
    y_sample = y_s.reshape(Bs, TS, D)
    k_p = k_new[:NP].reshape(B, T, KVH, HD)[:, -WB:]
    v_p = v_new[:NP].reshape(B, T, KVH, HD)[:, -WB:]
    k_s = jnp.concatenate([ck[:, TS:], k_new[NP:].reshape(Bs, TS, nkv)], axis=1).reshape(Bs, WB, KVH, HD)
    v_s = jnp.concatenate([cv[:, TS:], v_new[NP:].reshape(Bs, TS, nkv)], axis=1).reshape(Bs, WB, KVH, HD)
    return (y_prompt, y_sample, C_p[None], n_p[None], m_p.reshape(1, B, H), k_p, v_p,
            C_s[None], n_s[None], m_s[None], k_s, v_s)
```
